```python
import math
import jax, jax.numpy as jnp
from jax import lax
import numpy as np


D_MODEL = 1024
BATCH = 8
SEQ = 4096
DEPTH = 2
DEC_BATCH = 4
DEC_SEQ = 4096
PAST_LEN = 128

HEAD_DIM = 64
A_HEADS = 8
A_KV_HEADS = 2
WINDOW = 128
BLOCK = 128
B_HEADS = 4
GRID_W = 64
NA_ROWS = 8
NA_COLS = 16
M_HEADS = 4
MEM_LEN = 256
N_EXPERTS = 16
N_GROUPS = 4
TOP_K = 2
D_EXPERT = 512
LN_EPS = 1e-5

A_Q_W = A_HEADS * HEAD_DIM
A_KV_W = A_KV_HEADS * HEAD_DIM
B_W = B_HEADS * HEAD_DIM
M_W = M_HEADS * HEAD_DIM
MIX_W = A_Q_W + B_W + M_W
IN_W = A_Q_W + 2 * A_KV_W + 3 * B_W + M_W
DEEPNORM_ALPHA = (2 * DEPTH) ** 0.25
DEEPNORM_BETA = (8 * DEPTH) ** -0.25
ATTN_SCALE = HEAD_DIM ** -0.5

kernel_name = "hymba_style_hybrid_encoder"


def layer_norm(x, g, b):
    xf = x.astype(jnp.float32)
    mu = jnp.mean(xf, -1, keepdims=True)
    xc = xf - mu
    var = jnp.mean(xc * xc, -1, keepdims=True)
    return (xc * lax.rsqrt(var + LN_EPS) * g.astype(jnp.float32) + b.astype(jnp.float32)).astype(x.dtype)


def group_rms(o, g):
    of = o.astype(jnp.float32)
    return (of * lax.rsqrt(jnp.mean(of * of, -1, keepdims=True) + LN_EPS) * g.astype(jnp.float32)).astype(o.dtype)


def alibi_slopes(n):
    return jnp.exp2(-8.0 * jnp.arange(1, n + 1, dtype=jnp.float32) / n)


def window_attention(q, k, v, sink):
    B, S = q.shape[0], q.shape[1]
    nb = S // BLOCK
    G = A_HEADS // A_KV_HEADS
    qb = q.reshape(B, nb, BLOCK, A_KV_HEADS, G, HEAD_DIM)
    pad = ((0, 0), (BLOCK, BLOCK), (0, 0), (0, 0))

    def bands(t):
        tp = jnp.pad(t, pad).reshape(B, nb + 2, BLOCK, A_KV_HEADS, HEAD_DIM)
        return jnp.concatenate([tp[:, :-2], tp[:, 1:-1], tp[:, 2:]], axis=2)

    kb, vb = bands(k), bands(v)
    s = jnp.einsum('bnqhgd,bnkhd->bnhgqk', qb, kb, preferred_element_type=jnp.float32) * ATTN_SCALE
    a = jnp.arange(BLOCK)
    j = jnp.arange(3 * BLOCK)
    dist = a[:, None] + BLOCK - j[None, :]
    kpos = jnp.arange(nb)[:, None] * BLOCK - BLOCK + j[None, :]
    valid = (jnp.abs(dist) <= WINDOW)[None] & ((kpos >= 0) & (kpos < S))[:, None, :]
    slopes = alibi_slopes(A_HEADS).reshape(A_KV_HEADS, G)[:, :, None, None]
    s = s - slopes * jnp.abs(dist).astype(jnp.float32)
    s = jnp.where(valid[None, :, None, None], s, -jnp.inf)
    sk = sink.astype(jnp.float32).reshape(A_KV_HEADS, G)[:, :, None, None]
    m = jnp.maximum(jnp.max(s, -1, keepdims=True), sk)
    p = jnp.exp(s - m)
    p = p / (jnp.sum(p, -1, keepdims=True) + jnp.exp(sk - m))
    o = jnp.einsum('bnhgqk,bnkhd->bnqhgd', p.astype(v.dtype), vb)
    return o.reshape(B, S, A_Q_W)


def neighborhood_attention(q, k, v, rpb):
    B, S = q.shape[0], q.shape[1]
    rows = S // GRID_W
    kh = min(NA_ROWS, rows)
    qg = q.reshape(B, rows, GRID_W, B_HEADS, HEAD_DIM)
    kg = k.reshape(B, rows, GRID_W, B_HEADS, HEAD_DIM)
    vg = v.reshape(B, rows, GRID_W, B_HEADS, HEAD_DIM)
    r = jnp.arange(rows)
    c = jnp.arange(GRID_W)
    row_start = jnp.clip(r - kh // 2, 0, rows - kh)
    col_idx = jnp.clip(c - NA_COLS // 2, 0, GRID_W - NA_COLS)[:, None] + jnp.arange(NA_COLS)[None]
    dc_idx = col_idx - c[:, None] + NA_COLS - 1
    rpb_f = rpb.astype(jnp.float32)

    def one_row(ri):
        rs = row_start[ri]
        qr = lax.dynamic_index_in_dim(qg, ri, axis=1, keepdims=False)
        kr = lax.dynamic_slice_in_dim(kg, rs, kh, axis=1)[:, :, col_idx]
        vr = lax.dynamic_slice_in_dim(vg, rs, kh, axis=1)[:, :, col_idx]
        dr_idx = rs + jnp.arange(kh) - ri + NA_ROWS - 1
        bias = rpb_f[:, dr_idx[:, None, None], dc_idx[None]]
        s = jnp.einsum('bchd,bicjhd->bhicj', qr, kr, preferred_element_type=jnp.float32) * ATTN_SCALE + bias
        p = jax.nn.softmax(s, axis=(2, 4))
        return jnp.einsum('bhicj,bicjhd->bchd', p.astype(vr.dtype), vr)

    o = lax.map(one_row, r)
    return o.transpose(1, 0, 2, 3, 4).reshape(B, S, B_W)


def memory_attention(q, mk, mv):
    B, S = q.shape[0], q.shape[1]
    s = jnp.einsum('bshd,bmhd->bhsm', q, mk, preferred_element_type=jnp.float32) * ATTN_SCALE
    p = jax.nn.softmax(s, axis=-1)
    o = jnp.einsum('bhsm,bmhd->bshd', p.astype(mv.dtype), mv)
    return o.reshape(B, S, M_W)


def grouped_moe(x, w_router, router_bias, w_gate, w_up, w_down):
    B, S, D = x.shape
    xf = x.reshape(B * S, D)
    scores = jax.nn.sigmoid(jnp.dot(xf, w_router, preferred_element_type=jnp.float32))
    sel = scores + router_bias.astype(jnp.float32)
    eg = N_EXPERTS // N_GROUPS
    grp_score = jnp.sum(lax.top_k(sel.reshape(-1, N_GROUPS, eg), TOP_K)[0], -1)
    gmask = jnp.argmax(grp_score, -1)[:, None] == jnp.arange(N_GROUPS)[None]
    emask = jnp.repeat(gmask, eg, axis=1)
    _, idx = lax.top_k(jnp.where(emask, sel, -jnp.inf), TOP_K)
    w = jnp.take_along_axis(scores, idx, -1)
    w = w / jnp.sum(w, -1, keepdims=True)
    combine = jnp.einsum('nk,nke->ne', w, jax.nn.one_hot(idx, N_EXPERTS, dtype=jnp.float32))
    y = jnp.zeros((B * S, D), jnp.float32)
    for e in range(N_EXPERTS):
        h = jax.nn.silu(jnp.dot(xf, w_gate[e])) * jnp.dot(xf, w_up[e])
        y = y + combine[:, e:e + 1] * jnp.dot(h, w_down[e], preferred_element_type=jnp.float32)
    return y.astype(x.dtype).reshape(B, S, D)


def encoder_layer(x, mem, w_in, w_mem_kv, sink, rpb, grp_g, w_out, ln1_g, ln1_b,
                  w_router, router_bias, w_gate, w_up, w_down, ln2_g, ln2_b):
    B, S, _ = x.shape
    proj = jnp.dot(x, w_in)
    o1 = A_Q_W
    o2 = o1 + A_KV_W
    o3 = o2 + A_KV_W
    o4 = o3 + B_W
    o5 = o4 + B_W
    o6 = o5 + B_W
    qa, ka, va, qb, kb, vb, qm = jnp.split(proj, [o1, o2, o3, o4, o5, o6], axis=-1)
    mkv = jnp.dot(mem, w_mem_kv)
    mk, mv = jnp.split(mkv, 2, axis=-1)
    Mn = mem.shape[1]
    oa = window_attention(qa.reshape(B, S, A_HEADS, HEAD_DIM),
                          ka.reshape(B, S, A_KV_HEADS, HEAD_DIM),
                          va.reshape(B, S, A_KV_HEADS, HEAD_DIM), sink)
    ob = neighborhood_attention(qb.reshape(B, S, B_HEADS, HEAD_DIM),
                                kb.reshape(B, S, B_HEADS, HEAD_DIM),
                                vb.reshape(B, S, B_HEADS, HEAD_DIM), rpb)
    om = memory_attention(qm.reshape(B, S, M_HEADS, HEAD_DIM),
                          mk.reshape(B, Mn, M_HEADS, HEAD_DIM),
                          mv.reshape(B, Mn, M_HEADS, HEAD_DIM))
    mixed = jnp.concatenate([group_rms(oa, grp_g[:A_Q_W]),
                             group_rms(ob, grp_g[A_Q_W:A_Q_W + B_W]),
                             group_rms(om, grp_g[A_Q_W + B_W:])], axis=-1)
    x = layer_norm(DEEPNORM_ALPHA * x + jnp.dot(mixed, w_out), ln1_g, ln1_b)
    x = layer_norm(DEEPNORM_ALPHA * x + grouped_moe(x, w_router, router_bias, w_gate, w_up, w_down), ln2_g, ln2_b)
    return x


def setup_inputs(seed: int = 0) -> dict:
    key = jax.random.key(seed)
    ks = jax.random.split(key, 20)
    f32 = jnp.float32
    sd = D_MODEL ** -0.5
    nrm = lambda k, shp: jax.random.normal(k, shp, f32)
    col_scale = np.ones((IN_W,), np.float32)
    col_scale[A_Q_W + A_KV_W:A_Q_W + 2 * A_KV_W] = DEEPNORM_BETA
    vb0 = A_Q_W + 2 * A_KV_W + 2 * B_W
    col_scale[vb0:vb0 + B_W] = DEEPNORM_BETA
    mem_scale = np.concatenate([np.ones((M_W,), np.float32), np.full((M_W,), DEEPNORM_BETA, np.float32)])
    return {
        "x_prompt": nrm(ks[0], (BATCH, SEQ, D_MODEL)),
        "x_sample": nrm(ks[1], (DEC_BATCH, DEC_SEQ, D_MODEL)),
        "mem_prompt": nrm(ks[2], (BATCH, MEM_LEN, D_MODEL)),
        "mem_sample": nrm(ks[3], (DEC_BATCH, MEM_LEN, D_MODEL)),
        "w_in": nrm(ks[4], (DEPTH, D_MODEL, IN_W)) * sd * jnp.asarray(col_scale),
        "w_mem_kv": nrm(ks[5], (DEPTH, D_MODEL, 2 * M_W)) * sd * jnp.asarray(mem_scale),
        "sink_logits": nrm(ks[6], (DEPTH, A_HEADS)) * 0.5,
        "rpb": nrm(ks[7], (DEPTH, B_HEADS, 2 * NA_ROWS - 1, 2 * NA_COLS - 1)) * 0.02,
        "grp_norm_g": 1.0 + 0.02 * nrm(ks[8], (DEPTH, MIX_W)),
        "w_out": nrm(ks[9], (DEPTH, MIX_W, D_MODEL)) * (MIX_W ** -0.5) * DEEPNORM_BETA,
        "ln1_g": 1.0 + 0.02 * nrm(ks[10], (DEPTH, D_MODEL)),
        "ln1_b": 0.02 * nrm(ks[11], (DEPTH, D_MODEL)),
        "w_router": nrm(ks[12], (D_MODEL, N_EXPERTS)) * sd,
        "router_bias": 0.01 * nrm(ks[13], (N_EXPERTS,)),
        "w_gate": nrm(ks[14], (DEPTH, N_EXPERTS, D_MODEL, D_EXPERT)) * sd,
        "w_up": nrm(ks[15], (DEPTH, N_EXPERTS, D_MODEL, D_EXPERT)) * sd * DEEPNORM_BETA,
        "w_down": nrm(ks[16], (DEPTH, N_EXPERTS, D_EXPERT, D_MODEL)) * (D_EXPERT ** -0.5) * DEEPNORM_BETA,
        "ln2_g": 1.0 + 0.02 * nrm(ks[17], (DEPTH, D_MODEL)),
        "ln2_b": 0.02 * nrm(ks[18], (DEPTH, D_MODEL)),
    }


def reference(x_prompt, x_sample, mem_prompt, mem_sample, w_in, w_mem_kv, sink_logits, rpb,
              grp_norm_g, w_out, ln1_g, ln1_b, w_router, router_bias, w_gate, w_up, w_down,
              ln2_g, ln2_b):
    def trunk(x, mem):
        for l in range(DEPTH):
            x = encoder_layer(x, mem, w_in[l], w_mem_kv[l], sink_logits[l], rpb[l], grp_norm_g[l],
                              w_out[l], ln1_g[l], ln1_b[l], w_router, router_bias,
                              w_gate[l], w_up[l], w_down[l], ln2_g[l], ln2_b[l])
        return x

    y_prompt = trunk(x_prompt, mem_prompt)
    y_sample = trunk(x_sample, mem_sample)
    return (y_prompt, y_sample)
```

```python
import functools

import jax
import jax.numpy as jnp
import numpy as np
from jax import lax
from jax.experimental import pallas as pl
from jax.experimental.pallas import tpu as pltpu

F32 = jnp.float32
BF16 = jnp.bfloat16

D_MODEL = 1024
DEPTH = 2
HEAD_DIM = 64
A_HEADS = 8
WINDOW = 128
B_HEADS = 4
GRID_W = 64
NA_ROWS = 8
NA_COLS = 16
M_HEADS = 4
N_EXPERTS = 16
N_GROUPS = 4
D_EXPERT = 512
LN_EPS = 1e-5
A_Q_W = 512
B_W = 256
M_W = 256
IN_W = 1792
ALPHA = (2 * DEPTH) ** 0.25
SCALE = HEAD_DIM ** -0.5
NEG = -1e30

LANES = 128
NA_TILE_ROWS = 2
NA_KEY_ROWS = 10
VMEM_LIMIT = 56 * 1024 * 1024


def _params(n_axes, vmem=VMEM_LIMIT):
    return pltpu.CompilerParams(dimension_semantics=("arbitrary",) * n_axes, vmem_limit_bytes=vmem)


def _dot_nt(a, b):
    return lax.dot_general(a, b, (((1,), (1,)), ((), ())), preferred_element_type=F32)


def _dot(a, b):
    return jnp.dot(a, b, preferred_element_type=F32)


def _lo_mask():
    return lax.broadcasted_iota(jnp.int32, (1, LANES), 1) < HEAD_DIM


def _mm_kernel(x_ref, w_ref, o_ref):
    o_ref[...] = _dot(x_ref[...].astype(BF16), w_ref[...]).astype(o_ref.dtype)


def _matmul(x, w, tm, name):
    n, k = x.shape
    m = w.shape[1]
    tm = min(tm, n)
    assert n % tm == 0
    return pl.pallas_call(
        _mm_kernel,
        grid=(n // tm,),
        in_specs=[pl.BlockSpec((tm, k), lambda i: (i, 0)), pl.BlockSpec((k, m), lambda i: (0, 0))],
        out_specs=pl.BlockSpec((tm, m), lambda i: (i, 0)),
        out_shape=jax.ShapeDtypeStruct((n, m), BF16),
        compiler_params=_params(1),
        name=name,
    )(x, w)


def _group_rms(of, g):
    ms = jnp.mean(of * of, axis=-1, keepdims=True)
    return of * lax.rsqrt(ms + LN_EPS) * g


def _win_kernel(sink_ref, q_ref, k_ref, v_ref, g_ref, o_ref, ks_ref, vs_ref, *, seq, tq):
    i = pl.program_id(1)
    lo = _lo_mask()
    chunk = 512

    @pl.when(i == 0)
    def _build():
        zeros = jnp.zeros((WINDOW, LANES), BF16)
        for t in range(4):
            for dst in (ks_ref, vs_ref):
                dst[t, 0:WINDOW, :] = zeros
                dst[t, seq + WINDOW:seq + 2 * WINDOW, :] = zeros

        def body(c, carry):
            r0 = pl.multiple_of(c * chunk, chunk)
            for src, dst in ((k_ref, ks_ref), (v_ref, vs_ref)):
                x = src[pl.ds(r0, chunk), :].astype(F32)
                xr = pltpu.roll(x, HEAD_DIM, 1)
                rows = pl.ds(r0 + WINDOW, chunk)
                dst[0, rows, :] = jnp.where(lo, x, 0.0).astype(BF16)
                dst[1, rows, :] = jnp.where(lo, 0.0, xr).astype(BF16)
                dst[2, rows, :] = jnp.where(lo, xr, 0.0).astype(BF16)
                dst[3, rows, :] = jnp.where(lo, 0.0, x).astype(BF16)
            return carry

        lax.fori_loop(0, seq // chunk, body, 0)

    kw = 3 * WINDOW
    a_i = lax.broadcasted_iota(jnp.int32, (WINDOW, kw), 0)
    j_i = lax.broadcasted_iota(jnp.int32, (WINDOW, kw), 1)
    dist = a_i + WINDOW - j_i
    absd = jnp.abs(dist).astype(F32)
    in_window = jnp.abs(dist) <= WINDOW
    g = g_ref[...]

    def sub(j, carry):
        row0 = pl.multiple_of(j * WINDOW, WINDOW)
        qs = pl.multiple_of(i * tq + j * WINDOW, WINDOW)
        kpos = qs - WINDOW + j_i
        valid = in_window & (kpos >= 0) & (kpos < seq)
        win = pl.ds(qs, kw)
        outs = []
        for c in range(4):
            h = c // 2
            q2 = q_ref[pl.ds(row0, WINDOW), c * LANES:(c + 1) * LANES]
            num = None
            invs = []
            for t in range(2):
                head = 2 * c + t
                slope = 2.0 ** (-8.0 * (head + 1) / A_HEADS)
                s = _dot_nt(q2, ks_ref[2 * h + t, win, :]) * SCALE - slope * absd
                s = jnp.where(valid, s, NEG)
                sk = sink_ref[head]
                m = jnp.maximum(jnp.max(s, axis=-1, keepdims=True), sk)
                p = jnp.exp(s - m)
                den = jnp.sum(p, axis=-1, keepdims=True) + jnp.exp(sk - m)
                pv = _dot(p.astype(BF16), vs_ref[2 * h + t, win, :])
                num = pv if num is None else num + pv
                invs.append(1.0 / den)
            outs.append(num * jnp.where(lo, invs[0], invs[1]))
        of = jnp.concatenate(outs, axis=1)
        o_ref[pl.ds(row0, WINDOW), :] = _group_rms(of, g).astype(o_ref.dtype)
        return carry

    lax.fori_loop(0, tq // WINDOW, sub, 0)


def _window_attention(proj, sink, g, tq=512):
    b, s, _ = proj.shape
    kern = functools.partial(_win_kernel, seq=s, tq=tq)
    return pl.pallas_call(
        kern,
        grid=(b, s // tq),
        in_specs=[
            pl.BlockSpec(memory_space=pltpu.SMEM),
            pl.BlockSpec((None, tq, A_Q_W), lambda bi, i: (bi, i, 0)),
            pl.BlockSpec((None, s, LANES), lambda bi, i: (bi, 0, 4)),
            pl.BlockSpec((None, s, LANES), lambda bi, i: (bi, 0, 5)),
            pl.BlockSpec((1, A_Q_W), lambda bi, i: (0, 0)),
        ],
        out_specs=pl.BlockSpec((None, tq, A_Q_W), lambda bi, i: (bi, i, 0)),
        out_shape=jax.ShapeDtypeStruct((b, s, A_Q_W), BF16),
        scratch_shapes=[pltpu.VMEM((4, s + 2 * WINDOW, LANES), BF16),
                        pltpu.VMEM((4, s + 2 * WINDOW, LANES), BF16)],
        compiler_params=_params(2),
        name="window_attention",
    )(sink, proj, proj, proj, g)


def _na_tables(seq):
    rows = seq // GRID_W
    kh = min(NA_ROWS, rows)
    assert rows % NA_TILE_ROWS == 0 and rows >= NA_KEY_ROWS and NA_KEY_ROWS % 2 == 0
    nt = rows // NA_TILE_ROWS
    u = np.arange(NA_TILE_ROWS * GRID_W) // GRID_W
    c = np.arange(NA_TILE_ROWS * GRID_W) % GRID_W
    ki = np.arange(NA_KEY_ROWS * GRID_W) // GRID_W
    kc = np.arange(NA_KEY_ROWS * GRID_W) % GRID_W
    cs = np.clip(c - NA_COLS // 2, 0, GRID_W - NA_COLS)
    col_ok = (kc[None, :] >= cs[:, None]) & (kc[None, :] < cs[:, None] + NA_COLS)
    dc = np.clip(kc[None, :] - c[:, None] + NA_COLS - 1, 0, 2 * NA_COLS - 2)
    types, type_of, kstart = [], [], []
    for t in range(nt):
        r0 = t * NA_TILE_ROWS
        k0 = int(np.clip(r0 - kh // 2, 0, rows - NA_KEY_ROWS))
        k0 -= k0 % 2
        r = r0 + u
        rs = np.clip(r - kh // 2, 0, rows - kh)
        kr = k0 + ki
        row_ok = (kr[None, :] >= rs[:, None]) & (kr[None, :] < rs[:, None] + kh)
        assert (rs >= k0).all() and (rs + kh <= k0 + NA_KEY_ROWS).all()
        dr = np.clip(kr[None, :] - r[:, None] + NA_ROWS - 1, 0, 2 * NA_ROWS - 2)
        key = (dr.tobytes(), row_ok.tobytes())
        for idx, (k_, _, _) in enumerate(types):
            if k_ == key:
                type_of.append(idx)
                break
        else:
            type_of.append(len(types))
            types.append((key, dr, row_ok & col_ok))
        kstart.append(k0)
    dr_all = np.stack([t_[1] for t_ in types]).astype(np.int32)
    ok_all = np.stack([t_[2] for t_ in types])
    dc_all = np.broadcast_to(dc[None], dr_all.shape).astype(np.int32)
    return (np.asarray(type_of, np.int32), np.asarray(kstart, np.int32), dr_all, dc_all, ok_all)


def _na_bias(rpb, seq):
    _, _, dr, dc, ok = _na_tables(seq)
    bias = rpb.astype(F32)[:, dr, dc]
    bias = jnp.where(ok[None], bias, NEG)
    return jnp.transpose(bias, (1, 0, 2, 3))


def _na_kernel(type_ref, kstart_ref, q_ref, k_ref, v_ref, bias_ref, g_ref, o_ref, ks_ref, vs_ref, *, seq):
    del type_ref
    i = pl.program_id(1)
    lo = _lo_mask()
    chunk = 512

    @pl.when(i == 0)
    def _build():
        def body(c, carry):
            rows = pl.ds(pl.multiple_of(c * chunk, chunk), chunk)
            for src, dst in ((k_ref, ks_ref), (v_ref, vs_ref)):
                for p in range(2):
                    x = src[rows, p * LANES:(p + 1) * LANES]
                    dst[2 * p, rows, :] = jnp.where(lo, x, jnp.zeros_like(x))
                    dst[2 * p + 1, rows, :] = jnp.where(lo, jnp.zeros_like(x), x)
            return carry

        lax.fori_loop(0, seq // chunk, body, 0)

    nk = NA_KEY_ROWS * GRID_W
    win = pl.ds(pl.multiple_of(kstart_ref[i] * GRID_W, LANES), nk)
    outs = []
    for p in range(2):
        q2 = q_ref[:, p * LANES:(p + 1) * LANES]
        num = None
        invs = []
        for t in range(2):
            s = _dot_nt(q2, ks_ref[2 * p + t, win, :]) * SCALE + bias_ref[2 * p + t]
            m = jnp.max(s, axis=-1, keepdims=True)
            e = jnp.exp(s - m)
            den = jnp.sum(e, axis=-1, keepdims=True)
            pv = _dot(e.astype(BF16), vs_ref[2 * p + t, win, :])
            num = pv if num is None else num + pv
            invs.append(1.0 / den)
        outs.append(num * jnp.where(lo, invs[0], invs[1]))
    of = jnp.concatenate(outs, axis=1)
    o_ref[...] = _group_rms(of, g_ref[...]).astype(o_ref.dtype)


def _neighborhood_attention(proj, bias, g):
    b, s, _ = proj.shape
    type_of, kstart, _, _, _ = _na_tables(s)
    tq = NA_TILE_ROWS * GRID_W
    nk = NA_KEY_ROWS * GRID_W
    kern = functools.partial(_na_kernel, seq=s)
    grid_spec = pltpu.PrefetchScalarGridSpec(
        num_scalar_prefetch=2,
        grid=(b, s // tq),
        in_specs=[
            pl.BlockSpec((None, tq, B_W), lambda bi, i, ty, ks: (bi, i, 3)),
            pl.BlockSpec((None, s, B_W), lambda bi, i, ty, ks: (bi, 0, 4)),
            pl.BlockSpec((None, s, B_W), lambda bi, i, ty, ks: (bi, 0, 5)),
            pl.BlockSpec((None, B_HEADS, tq, nk), lambda bi, i, ty, ks: (ty[i], 0, 0, 0)),
            pl.BlockSpec((1, B_W), lambda bi, i, ty, ks: (0, 0)),
        ],
        out_specs=pl.BlockSpec((None, tq, B_W), lambda bi, i, ty, ks: (bi, i, 0)),
        scratch_shapes=[pltpu.VMEM((4, s, LANES), BF16), pltpu.VMEM((4, s, LANES), BF16)],
    )
    return pl.pallas_call(
        kern,
        grid_spec=grid_spec,
        out_shape=jax.ShapeDtypeStruct((b, s, B_W), BF16),
        compiler_params=_params(2),
        name="neighborhood_attention",
    )(jnp.asarray(type_of), jnp.asarray(kstart), proj, proj, proj, bias, g)


def _mem_kernel(q_ref, mkv_ref, g_ref, o_ref):
    lo = _lo_mask()
    outs = []
    for p in range(2):
        q2 = q_ref[:, p * LANES:(p + 1) * LANES]
        mk = mkv_ref[:, p * LANES:(p + 1) * LANES]
        mv = mkv_ref[:, M_W + p * LANES:M_W + (p + 1) * LANES]
        zero = jnp.zeros_like(mk)
        num = None
        invs = []
        for t in range(2):
            keep = lo if t == 0 else jnp.logical_not(lo)
            s = _dot_nt(q2, jnp.where(keep, mk, zero)) * SCALE
            m = jnp.max(s, axis=-1, keepdims=True)
            e = jnp.exp(s - m)
            den = jnp.sum(e, axis=-1, keepdims=True)
            pv = _dot(e.astype(BF16), jnp.where(keep, mv, zero))
            num = pv if num is None else num + pv
            invs.append(1.0 / den)
        outs.append(num * jnp.where(lo, invs[0], invs[1]))
    of = jnp.concatenate(outs, axis=1)
    o_ref[...] = _group_rms(of, g_ref[...]).astype(o_ref.dtype)


def _memory_attention(proj, mkv, g, tq=512):
    b, s, _ = proj.shape
    mlen = mkv.shape[1]
    return pl.pallas_call(
        _mem_kernel,
        grid=(b, s // tq),
        in_specs=[
            pl.BlockSpec((None, tq, M_W), lambda bi, i: (bi, i, 6)),
            pl.BlockSpec((None, mlen, 2 * M_W), lambda bi, i: (bi, 0, 0)),
            pl.BlockSpec((1, M_W), lambda bi, i: (0, 0)),
        ],
        out_specs=pl.BlockSpec((None, tq, M_W), lambda bi, i: (bi, i, 0)),
        out_shape=jax.ShapeDtypeStruct((b, s, M_W), BF16),
        compiler_params=_params(2),
        name="memory_attention",
    )(proj, mkv, g)


def _layer_norm(z, g, b):
    mu = jnp.mean(z, axis=-1, keepdims=True)
    zc = z - mu
    var = jnp.mean(zc * zc, axis=-1, keepdims=True)
    return zc * lax.rsqrt(var + LN_EPS) * g + b


def _route(lg_t, rbias):
    e, t = lg_t.shape
    eg = e // N_GROUPS
    scores = 1.0 / (1.0 + jnp.exp(-lg_t))
    sel = scores + rbias
    row = lax.broadcasted_iota(jnp.int32, (e, t), 0)
    best = None
    gidx = None
    for gi in range(N_GROUPS):
        r = [sel[gi * eg + k:gi * eg + k + 1, :] for k in range(eg)]
        top2 = None
        for a in range(eg):
            for b in range(a + 1, eg):
                pair = r[a] + r[b]
                top2 = pair if top2 is None else jnp.maximum(top2, pair)
        if best is None:
            best, gidx = top2, jnp.zeros((1, t), jnp.int32)
        else:
            better = top2 > best
            gidx = jnp.where(better, gi, gidx)
            best = jnp.maximum(best, top2)
    in_group = (row // eg) == gidx
    masked = jnp.where(in_group, sel, -jnp.inf)
    m1 = jnp.max(masked, axis=0, keepdims=True)
    i1 = jnp.min(jnp.where(masked == m1, row, e), axis=0, keepdims=True)
    rest = jnp.where(row == i1, -jnp.inf, masked)
    m2 = jnp.max(rest, axis=0, keepdims=True)
    i2 = jnp.min(jnp.where(rest == m2, row, e), axis=0, keepdims=True)
    pick1 = row == i1
    pick2 = row == i2
    w1 = jnp.sum(jnp.where(pick1, scores, 0.0), axis=0, keepdims=True)
    w2 = jnp.sum(jnp.where(pick2, scores, 0.0), axis=0, keepdims=True)
    tot = w1 + w2
    return jnp.where(pick1, w1 / tot, 0.0) + jnp.where(pick2, w2 / tot, 0.0)


def _out_kernel(ma_ref, mb_ref, mm_ref, w_ref, x_ref, g_ref, b_ref, wr_ref, rb_ref,
                x1_ref, x1b_ref, c_ref):
    y = _dot(ma_ref[...], w_ref[0:A_Q_W, :])
    y = y + _dot(mb_ref[...], w_ref[A_Q_W:A_Q_W + B_W, :])
    y = y + _dot(mm_ref[...], w_ref[A_Q_W + B_W:, :])
    x1 = _layer_norm(ALPHA * x_ref[...] + y, g_ref[...], b_ref[...])
    x1_ref[...] = x1
    x1b_ref[...] = x1.astype(BF16)
    lg = jnp.dot(x1, wr_ref[...], preferred_element_type=F32, precision=lax.Precision.HIGHEST)
    lg_t = lg.T[0:N_EXPERTS, :]
    c_t = _route(lg_t, rb_ref[...])
    tm = c_t.shape[1]
    c_pad = jnp.concatenate([c_t, jnp.zeros((LANES - N_EXPERTS, tm), F32)], axis=0)
    c_ref[...] = c_pad.T


def _out_proj(ma, mb, mm, w_out, x, g, b, wr_pad, rbias, tm=512):
    n = x.shape[0]
    row = lambda i: (i, 0)
    fix = lambda i: (0, 0)
    return pl.pallas_call(
        _out_kernel,
        grid=(n // tm,),
        in_specs=[
            pl.BlockSpec((tm, A_Q_W), row), pl.BlockSpec((tm, B_W), row), pl.BlockSpec((tm, M_W), row),
            pl.BlockSpec((D_MODEL, D_MODEL), fix), pl.BlockSpec((tm, D_MODEL), row),
            pl.BlockSpec((1, D_MODEL), fix), pl.BlockSpec((1, D_MODEL), fix),
            pl.BlockSpec((D_MODEL, LANES), fix), pl.BlockSpec((N_EXPERTS, 1), fix),
        ],
        out_specs=[pl.BlockSpec((tm, D_MODEL), row), pl.BlockSpec((tm, D_MODEL), row),
                   pl.BlockSpec((tm, LANES), row)],
        out_shape=[jax.ShapeDtypeStruct((n, D_MODEL), F32), jax.ShapeDtypeStruct((n, D_MODEL), BF16),
                   jax.ShapeDtypeStruct((n, LANES), F32)],
        compiler_params=_params(1),
        name="out_proj_ln_router",
    )(ma, mb, mm, w_out, x, g, b, wr_pad, rbias)


def _moe_kernel(xb_ref, wg_ref, wu_ref, wd_ref, c_ref, x1_ref, g_ref, b_ref, o_ref, acc_ref):
    e = pl.program_id(1)

    @pl.when(e == 0)
    def _init():
        acc_ref[...] = jnp.zeros_like(acc_ref)

    xb = xb_ref[...]
    gate = _dot(xb, wg_ref[...])
    up = _dot(xb, wu_ref[...])
    h = (gate * (1.0 / (1.0 + jnp.exp(-gate))) * up).astype(BF16)
    y = _dot(h, wd_ref[...])
    lane = lax.broadcasted_iota(jnp.int32, c_ref.shape, 1)
    ce = jnp.sum(jnp.where(lane == e, c_ref[...], 0.0), axis=-1, keepdims=True)
    acc_ref[...] += ce * y

    @pl.when(e == pl.num_programs(1) - 1)
    def _fin():
        o_ref[...] = _layer_norm(ALPHA * x1_ref[...] + acc_ref[...], g_ref[...], b_ref[...])


def _moe_dense(x1b, wg, wu, wd, c, x1, g, b, tm=1024):
    n = x1.shape[0]
    row = lambda i, e: (i, 0)
    fix = lambda i, e: (0, 0)
    return pl.pallas_call(
        _moe_kernel,
        grid=(n // tm, N_EXPERTS),
        in_specs=[
            pl.BlockSpec((tm, D_MODEL), row),
            pl.BlockSpec((None, D_MODEL, D_EXPERT), lambda i, e: (e, 0, 0)),
            pl.BlockSpec((None, D_MODEL, D_EXPERT), lambda i, e: (e, 0, 0)),
            pl.BlockSpec((None, D_EXPERT, D_MODEL), lambda i, e: (e, 0, 0)),
            pl.BlockSpec((tm, LANES), row), pl.BlockSpec((tm, D_MODEL), row),
            pl.BlockSpec((1, D_MODEL), fix), pl.BlockSpec((1, D_MODEL), fix),
        ],
        out_specs=pl.BlockSpec((tm, D_MODEL), row),
        out_shape=jax.ShapeDtypeStruct((n, D_MODEL), F32),
        scratch_shapes=[pltpu.VMEM((tm, D_MODEL), F32)],
        compiler_params=_params(2),
        name="moe_ln2",
    )(x1b, wg, wu, wd, c, x1, g, b)


def _layer(x, mem, p):
    b, s, d = x.shape
    n = b * s
    xf = x.reshape(n, d)
    proj = _matmul(xf, p["w_in"], 512, "in_proj").reshape(b, s, IN_W)
    mkv = _matmul(mem.reshape(-1, d), p["w_mem_kv"], 512, "mem_kv").reshape(b, -1, 2 * M_W)
    oa = _window_attention(proj, p["sink"], p["g_a"])
    ob = _neighborhood_attention(proj, p["na_bias"], p["g_b"])
    om = _memory_attention(proj, mkv, p["g_m"])
    x1, x1b, c = _out_proj(oa.reshape(n, A_Q_W), ob.reshape(n, B_W), om.reshape(n, M_W), p["w_out"], xf,
                           p["ln1_g"], p["ln1_b"], p["wr_pad"], p["rbias"])
    y = _moe_dense(x1b, p["w_gate"], p["w_up"], p["w_down"], c, x1, p["ln2_g"], p["ln2_b"])
    return y.reshape(b, s, d)


def kernel(x_prompt, x_sample, mem_prompt, mem_sample, w_in, w_mem_kv, sink_logits, rpb, grp_norm_g, w_out,
           ln1_g, ln1_b, w_router, router_bias, w_gate, w_up, w_down, ln2_g, ln2_b):
    seq = x_prompt.shape[1]
    assert x_sample.shape[1] == seq
    wr_pad = jnp.pad(w_router.astype(F32), ((0, 0), (0, LANES - N_EXPERTS)))
    rbias = router_bias.astype(F32).reshape(N_EXPERTS, 1)
    layers = []
    for l in range(DEPTH):
        g = grp_norm_g[l].astype(F32)
        layers.append(dict(
            w_in=w_in[l].astype(BF16), w_mem_kv=w_mem_kv[l].astype(BF16),
            sink=sink_logits[l].astype(F32), na_bias=_na_bias(rpb[l], seq),
            g_a=g[:A_Q_W].reshape(1, -1), g_b=g[A_Q_W:A_Q_W + B_W].reshape(1, -1),
            g_m=g[A_Q_W + B_W:].reshape(1, -1),
            w_out=w_out[l].astype(BF16),
            ln1_g=ln1_g[l].astype(F32).reshape(1, -1), ln1_b=ln1_b[l].astype(F32).reshape(1, -1),
            wr_pad=wr_pad, rbias=rbias,
            w_gate=w_gate[l].astype(BF16), w_up=w_up[l].astype(BF16), w_down=w_down[l].astype(BF16),
            ln2_g=ln2_g[l].astype(F32).reshape(1, -1), ln2_b=ln2_b[l].astype(F32).reshape(1, -1)))

    def trunk(x, mem):
        for p in layers:
            x = _layer(x, mem, p)
        return x

    return (trunk(x_prompt, mem_prompt), trunk(x_sample, mem_sample))
```

```python
import functools

import jax
import jax.numpy as jnp
import numpy as np
from jax import lax
from jax.experimental import pallas as pl
from jax.experimental.pallas import tpu as pltpu

F32 = jnp.float32
BF16 = jnp.bfloat16

D_MODEL = 1024
DEPTH = 2
HEAD_DIM = 64
A_HEADS = 8
WINDOW = 128
B_HEADS = 4
GRID_W = 64
NA_ROWS = 8
NA_COLS = 16
M_HEADS = 4
N_EXPERTS = 16
N_GROUPS = 4
D_EXPERT = 512
LN_EPS = 1e-5
A_Q_W = 512
B_W = 256
M_W = 256
IN_W = 1792
ALPHA = (2 * DEPTH) ** 0.25
SCALE = HEAD_DIM ** -0.5
NEG = -1e30

LANES = 128
NA_TILE_ROWS = 2
NA_KEY_ROWS = 10
VMEM_LIMIT = 56 * 1024 * 1024


def _params(n_axes, vmem=VMEM_LIMIT):
    return pltpu.CompilerParams(dimension_semantics=("arbitrary",) * n_axes, vmem_limit_bytes=vmem)


def _dot_nt(a, b):
    return lax.dot_general(a, b, (((1,), (1,)), ((), ())), preferred_element_type=F32)


def _dot(a, b):
    return jnp.dot(a, b, preferred_element_type=F32)


def _lo_mask():
    return lax.broadcasted_iota(jnp.int32, (1, LANES), 1) < HEAD_DIM


def _mm_kernel(x_ref, w_ref, o_ref):
    o_ref[...] = _dot(x_ref[...].astype(BF16), w_ref[...]).astype(o_ref.dtype)


def _matmul(x, w, tm, name):
    n, k = x.shape
    m = w.shape[1]
    tm = min(tm, n)
    assert n % tm == 0
    return pl.pallas_call(
        _mm_kernel,
        grid=(n // tm,),
        in_specs=[pl.BlockSpec((tm, k), lambda i: (i, 0)), pl.BlockSpec((k, m), lambda i: (0, 0))],
        out_specs=pl.BlockSpec((tm, m), lambda i: (i, 0)),
        out_shape=jax.ShapeDtypeStruct((n, m), BF16),
        compiler_params=_params(1),
        name=name,
    )(x, w)


def _group_rms(of, g):
    ms = jnp.mean(of * of, axis=-1, keepdims=True)
    return of * lax.rsqrt(ms + LN_EPS) * g


def _win_kernel(sink_ref, q_ref, k_ref, v_ref, g_ref, o_ref, ks_ref, vs_ref, *, seq, tq):
    i = pl.program_id(1)
    lo = _lo_mask()
    chunk = 512

    @pl.when(i == 0)
    def _build():
        zeros = jnp.zeros((WINDOW, LANES), BF16)
        for t in range(4):
            for dst in (ks_ref, vs_ref):
                dst[t, 0:WINDOW, :] = zeros
                dst[t, seq + WINDOW:seq + 2 * WINDOW, :] = zeros

        def body(c, carry):
            r0 = pl.multiple_of(c * chunk, chunk)
            for src, dst in ((k_ref, ks_ref), (v_ref, vs_ref)):
                x = src[pl.ds(r0, chunk), :].astype(F32)
                xr = pltpu.roll(x, HEAD_DIM, 1)
                rows = pl.ds(r0 + WINDOW, chunk)
                dst[0, rows, :] = jnp.where(lo, x, 0.0).astype(BF16)
                dst[1, rows, :] = jnp.where(lo, 0.0, xr).astype(BF16)
                dst[2, rows, :] = jnp.where(lo, xr, 0.0).astype(BF16)
                dst[3, rows, :] = jnp.where(lo, 0.0, x).astype(BF16)
            return carry

        lax.fori_loop(0, seq // chunk, body, 0)

    kw = 3 * WINDOW
    a_i = lax.broadcasted_iota(jnp.int32, (WINDOW, kw), 0)
    j_i = lax.broadcasted_iota(jnp.int32, (WINDOW, kw), 1)
    dist = a_i + WINDOW - j_i
    absd = jnp.abs(dist).astype(F32)
    in_window = jnp.abs(dist) <= WINDOW
    g = g_ref[...]

    def sub(j, carry):
        row0 = pl.multiple_of(j * WINDOW, WINDOW)
        qs = pl.multiple_of(i * tq + j * WINDOW, WINDOW)
        kpos = qs - WINDOW + j_i
        valid = in_window & (kpos >= 0) & (kpos < seq)
        win = pl.ds(qs, kw)
        outs = []
        for c in range(4):
            h = c // 2
            q2 = q_ref[pl.ds(row0, WINDOW), c * LANES:(c + 1) * LANES]
            num = None
            invs = []
            for t in range(2):
                head = 2 * c + t
                slope = 2.0 ** (-8.0 * (head + 1) / A_HEADS)
                s = _dot_nt(q2, ks_ref[2 * h + t, win, :]) * SCALE - slope * absd
                s = jnp.where(valid, s, NEG)
                sk = sink_ref[head]
                m = jnp.maximum(jnp.max(s, axis=-1, keepdims=True), sk)
                p = jnp.exp(s - m)
                den = jnp.sum(p, axis=-1, keepdims=True) + jnp.exp(sk - m)
                pv = _dot(p.astype(BF16), vs_ref[2 * h + t, win, :])
                num = pv if num is None else num + pv
                invs.append(1.0 / den)
            outs.append(num * jnp.where(lo, invs[0], invs[1]))
        of = jnp.concatenate(outs, axis=1)
        o_ref[pl.ds(row0, WINDOW), :] = _group_rms(of, g).astype(o_ref.dtype)
        return carry

    lax.fori_loop(0, tq // WINDOW, sub, 0)


def _window_attention(proj, sink, g, tq=512):
    b, s, _ = proj.shape
    kern = functools.partial(_win_kernel, seq=s, tq=tq)
    return pl.pallas_call(
        kern,
        grid=(b, s // tq),
        in_specs=[
            pl.BlockSpec(memory_space=pltpu.SMEM),
            pl.BlockSpec((None, tq, A_Q_W), lambda bi, i: (bi, i, 0)),
            pl.BlockSpec((None, s, LANES), lambda bi, i: (bi, 0, 4)),
            pl.BlockSpec((None, s, LANES), lambda bi, i: (bi, 0, 5)),
            pl.BlockSpec((1, A_Q_W), lambda bi, i: (0, 0)),
        ],
        out_specs=pl.BlockSpec((None, tq, A_Q_W), lambda bi, i: (bi, i, 0)),
        out_shape=jax.ShapeDtypeStruct((b, s, A_Q_W), BF16),
        scratch_shapes=[pltpu.VMEM((4, s + 2 * WINDOW, LANES), BF16),
                        pltpu.VMEM((4, s + 2 * WINDOW, LANES), BF16)],
        compiler_params=_params(2),
        name="window_attention",
    )(sink, proj, proj, proj, g)


def _na_tables(seq):
    rows = seq // GRID_W
    kh = min(NA_ROWS, rows)
    assert rows % NA_TILE_ROWS == 0 and rows >= NA_KEY_ROWS and NA_KEY_ROWS % 2 == 0
    nt = rows // NA_TILE_ROWS
    u = np.arange(NA_TILE_ROWS * GRID_W) // GRID_W
    c = np.arange(NA_TILE_ROWS * GRID_W) % GRID_W
    ki = np.arange(NA_KEY_ROWS * GRID_W) // GRID_W
    kc = np.arange(NA_KEY_ROWS * GRID_W) % GRID_W
    cs = np.clip(c - NA_COLS // 2, 0, GRID_W - NA_COLS)
    col_ok = (kc[None, :] >= cs[:, None]) & (kc[None, :] < cs[:, None] + NA_COLS)
    dc = np.clip(kc[None, :] - c[:, None] + NA_COLS - 1, 0, 2 * NA_COLS - 2)
    types, type_of, kstart = [], [], []
    for t in range(nt):
        r0 = t * NA_TILE_ROWS
        k0 = int(np.clip(r0 - kh // 2, 0, rows - NA_KEY_ROWS))
        k0 -= k0 % 2
        r = r0 + u
        rs = np.clip(r - kh // 2, 0, rows - kh)
        kr = k0 + ki
        row_ok = (kr[None, :] >= rs[:, None]) & (kr[None, :] < rs[:, None] + kh)
        assert (rs >= k0).all() and (rs + kh <= k0 + NA_KEY_ROWS).all()
        dr = np.clip(kr[None, :] - r[:, None] + NA_ROWS - 1, 0, 2 * NA_ROWS - 2)
        key = (dr.tobytes(), row_ok.tobytes())
        for idx, (k_, _, _) in enumerate(types):
            if k_ == key:
                type_of.append(idx)
                break
        else:
            type_of.append(len(types))
            types.append((key, dr, row_ok & col_ok))
        kstart.append(k0)
    dr_rows = np.stack([t_[1][::GRID_W, ::GRID_W] for t_ in types])
    ok_all = np.stack([t_[2] for t_ in types])
    dc_cols = dc[:GRID_W, :GRID_W]
    return (np.asarray(type_of, np.int32), np.asarray(kstart, np.int32), dr_rows, dc_cols, ok_all)


def _na_bias(rpb, seq):
    _, _, dr_rows, dc_cols, ok = _na_tables(seq)
    nt = dr_rows.shape[0]
    hi = lax.Precision.HIGHEST
    col_sel = jnp.asarray(np.eye(2 * NA_COLS - 1, dtype=np.float32)[dc_cols])
    row_sel = jnp.asarray(np.eye(2 * NA_ROWS - 1, dtype=np.float32)[dr_rows])
    by_col = jnp.einsum("hdj,ckj->hdck", rpb.astype(F32), col_sel, precision=hi)
    bias = jnp.einsum("tuid,hdck->thucik", row_sel, by_col, precision=hi)
    bias = bias.reshape(nt, B_HEADS, NA_TILE_ROWS * GRID_W, NA_KEY_ROWS * GRID_W)
    return jnp.where(ok[:, None], bias, NEG)


def _na_kernel(type_ref, kstart_ref, q_ref, k_ref, v_ref, bias_ref, g_ref, o_ref, ks_ref, vs_ref, *, seq):
    del type_ref
    i = pl.program_id(1)
    lo = _lo_mask()
    chunk = 512

    @pl.when(i == 0)
    def _build():
        def body(c, carry):
            rows = pl.ds(pl.multiple_of(c * chunk, chunk), chunk)
            for src, dst in ((k_ref, ks_ref), (v_ref, vs_ref)):
                for p in range(2):
                    x = src[rows, p * LANES:(p + 1) * LANES]
                    dst[2 * p, rows, :] = jnp.where(lo, x, jnp.zeros_like(x))
                    dst[2 * p + 1, rows, :] = jnp.where(lo, jnp.zeros_like(x), x)
            return carry

        lax.fori_loop(0, seq // chunk, body, 0)

    nk = NA_KEY_ROWS * GRID_W
    win = pl.ds(pl.multiple_of(kstart_ref[i] * GRID_W, LANES), nk)
    outs = []
    for p in range(2):
        q2 = q_ref[:, p * LANES:(p + 1) * LANES]
        num = None
        invs = []
        for t in range(2):
            s = _dot_nt(q2, ks_ref[2 * p + t, win, :]) * SCALE + bias_ref[2 * p + t]
            m = jnp.max(s, axis=-1, keepdims=True)
            e = jnp.exp(s - m)
            den = jnp.sum(e, axis=-1, keepdims=True)
            pv = _dot(e.astype(BF16), vs_ref[2 * p + t, win, :])
            num = pv if num is None else num + pv
            invs.append(1.0 / den)
        outs.append(num * jnp.where(lo, invs[0], invs[1]))
    of = jnp.concatenate(outs, axis=1)
    o_ref[...] = _group_rms(of, g_ref[...]).astype(o_ref.dtype)


def _neighborhood_attention(proj, bias, g):
    b, s, _ = proj.shape
    type_of, kstart, _, _, _ = _na_tables(s)
    tq = NA_TILE_ROWS * GRID_W
    nk = NA_KEY_ROWS * GRID_W
    kern = functools.partial(_na_kernel, seq=s)
    grid_spec = pltpu.PrefetchScalarGridSpec(
        num_scalar_prefetch=2,
        grid=(b, s // tq),
        in_specs=[
            pl.BlockSpec((None, tq, B_W), lambda bi, i, ty, ks: (bi, i, 3)),
            pl.BlockSpec((None, s, B_W), lambda bi, i, ty, ks: (bi, 0, 4)),
            pl.BlockSpec((None, s, B_W), lambda bi, i, ty, ks: (bi, 0, 5)),
            pl.BlockSpec((None, B_HEADS, tq, nk), lambda bi, i, ty, ks: (ty[i], 0, 0, 0)),
            pl.BlockSpec((1, B_W), lambda bi, i, ty, ks: (0, 0)),
        ],
        out_specs=pl.BlockSpec((None, tq, B_W), lambda bi, i, ty, ks: (bi, i, 0)),
        scratch_shapes=[pltpu.VMEM((4, s, LANES), BF16), pltpu.VMEM((4, s, LANES), BF16)],
    )
    return pl.pallas_call(
        kern,
        grid_spec=grid_spec,
        out_shape=jax.ShapeDtypeStruct((b, s, B_W), BF16),
        compiler_params=_params(2),
        name="neighborhood_attention",
    )(jnp.asarray(type_of), jnp.asarray(kstart), proj, proj, proj, bias, g)


def _mem_kernel(q_ref, mkv_ref, g_ref, o_ref):
    lo = _lo_mask()
    outs = []
    for p in range(2):
        q2 = q_ref[:, p * LANES:(p + 1) * LANES]
        mk = mkv_ref[:, p * LANES:(p + 1) * LANES]
        mv = mkv_ref[:, M_W + p * LANES:M_W + (p + 1) * LANES]
        zero = jnp.zeros_like(mk)
        num = None
        invs = []
        for t in range(2):
            keep = lo if t == 0 else jnp.logical_not(lo)
            s = _dot_nt(q2, jnp.where(keep, mk, zero)) * SCALE
            m = jnp.max(s, axis=-1, keepdims=True)
            e = jnp.exp(s - m)
            den = jnp.sum(e, axis=-1, keepdims=True)
            pv = _dot(e.astype(BF16), jnp.where(keep, mv, zero))
            num = pv if num is None else num + pv
            invs.append(1.0 / den)
        outs.append(num * jnp.where(lo, invs[0], invs[1]))
    of = jnp.concatenate(outs, axis=1)
    o_ref[...] = _group_rms(of, g_ref[...]).astype(o_ref.dtype)


def _memory_attention(proj, mkv, g, tq=512):
    b, s, _ = proj.shape
    mlen = mkv.shape[1]
    return pl.pallas_call(
        _mem_kernel,
        grid=(b, s // tq),
        in_specs=[
            pl.BlockSpec((None, tq, M_W), lambda bi, i: (bi, i, 6)),
            pl.BlockSpec((None, mlen, 2 * M_W), lambda bi, i: (bi, 0, 0)),
            pl.BlockSpec((1, M_W), lambda bi, i: (0, 0)),
        ],
        out_specs=pl.BlockSpec((None, tq, M_W), lambda bi, i: (bi, i, 0)),
        out_shape=jax.ShapeDtypeStruct((b, s, M_W), BF16),
        compiler_params=_params(2),
        name="memory_attention",
    )(proj, mkv, g)


def _layer_norm(z, g, b):
    mu = jnp.mean(z, axis=-1, keepdims=True)
    zc = z - mu
    var = jnp.mean(zc * zc, axis=-1, keepdims=True)
    return zc * lax.rsqrt(var + LN_EPS) * g + b


def _route(lg_t, rbias):
    e, t = lg_t.shape
    eg = e // N_GROUPS
    scores = 1.0 / (1.0 + jnp.exp(-lg_t))
    sel = scores + rbias
    row = lax.broadcasted_iota(jnp.int32, (e, t), 0)
    best = None
    gidx = None
    for gi in range(N_GROUPS):
        r = [sel[gi * eg + k:gi * eg + k + 1, :] for k in range(eg)]
        top2 = None
        for a in range(eg):
            for b in range(a + 1, eg):
                pair = r[a] + r[b]
                top2 = pair if top2 is None else jnp.maximum(top2, pair)
        if best is None:
            best, gidx = top2, jnp.zeros((1, t), jnp.int32)
        else:
            better = top2 > best
            gidx = jnp.where(better, gi, gidx)
            best = jnp.maximum(best, top2)
    in_group = (row // eg) == gidx
    masked = jnp.where(in_group, sel, -jnp.inf)
    m1 = jnp.max(masked, axis=0, keepdims=True)
    i1 = jnp.min(jnp.where(masked == m1, row, e), axis=0, keepdims=True)
    rest = jnp.where(row == i1, -jnp.inf, masked)
    m2 = jnp.max(rest, axis=0, keepdims=True)
    i2 = jnp.min(jnp.where(rest == m2, row, e), axis=0, keepdims=True)
    pick1 = row == i1
    pick2 = row == i2
    w1 = jnp.sum(jnp.where(pick1, scores, 0.0), axis=0, keepdims=True)
    w2 = jnp.sum(jnp.where(pick2, scores, 0.0), axis=0, keepdims=True)
    tot = w1 + w2
    return jnp.where(pick1, w1 / tot, 0.0) + jnp.where(pick2, w2 / tot, 0.0)


def _out_kernel(ma_ref, mb_ref, mm_ref, w_ref, x_ref, g_ref, b_ref, wr_ref, rb_ref,
                x1_ref, x1b_ref, c_ref):
    y = _dot(ma_ref[...], w_ref[0:A_Q_W, :])
    y = y + _dot(mb_ref[...], w_ref[A_Q_W:A_Q_W + B_W, :])
    y = y + _dot(mm_ref[...], w_ref[A_Q_W + B_W:, :])
    x1 = _layer_norm(ALPHA * x_ref[...] + y, g_ref[...], b_ref[...])
    x1_ref[...] = x1
    x1b_ref[...] = x1.astype(BF16)
    lg = jnp.dot(x1, wr_ref[...], preferred_element_type=F32, precision=lax.Precision.HIGHEST)
    lg_t = lg.T[0:N_EXPERTS, :]
    c_t = _route(lg_t, rb_ref[...])
    tm = c_t.shape[1]
    c_pad = jnp.concatenate([c_t, jnp.zeros((LANES - N_EXPERTS, tm), F32)], axis=0)
    c_ref[...] = c_pad.T


def _out_proj(ma, mb, mm, w_out, x, g, b, wr_pad, rbias, tm=512):
    n = x.shape[0]
    row = lambda i: (i, 0)
    fix = lambda i: (0, 0)
    return pl.pallas_call(
        _out_kernel,
        grid=(n // tm,),
        in_specs=[
            pl.BlockSpec((tm, A_Q_W), row), pl.BlockSpec((tm, B_W), row), pl.BlockSpec((tm, M_W), row),
            pl.BlockSpec((D_MODEL, D_MODEL), fix), pl.BlockSpec((tm, D_MODEL), row),
            pl.BlockSpec((1, D_MODEL), fix), pl.BlockSpec((1, D_MODEL), fix),
            pl.BlockSpec((D_MODEL, LANES), fix), pl.BlockSpec((N_EXPERTS, 1), fix),
        ],
        out_specs=[pl.BlockSpec((tm, D_MODEL), row), pl.BlockSpec((tm, D_MODEL), row),
                   pl.BlockSpec((tm, LANES), row)],
        out_shape=[jax.ShapeDtypeStruct((n, D_MODEL), F32), jax.ShapeDtypeStruct((n, D_MODEL), BF16),
                   jax.ShapeDtypeStruct((n, LANES), F32)],
        compiler_params=_params(1),
        name="out_proj_ln_router",
    )(ma, mb, mm, w_out, x, g, b, wr_pad, rbias)


def _moe_kernel(xb_ref, wg_ref, wu_ref, wd_ref, c_ref, x1_ref, g_ref, b_ref, o_ref, acc_ref):
    e = pl.program_id(1)

    @pl.when(e == 0)
    def _init():
        acc_ref[...] = jnp.zeros_like(acc_ref)

    xb = xb_ref[...]
    gate = _dot(xb, wg_ref[...])
    up = _dot(xb, wu_ref[...])
    h = (gate * (1.0 / (1.0 + jnp.exp(-gate))) * up).astype(BF16)
    y = _dot(h, wd_ref[...])
    lane = lax.broadcasted_iota(jnp.int32, c_ref.shape, 1)
    ce = jnp.sum(jnp.where(lane == e, c_ref[...], 0.0), axis=-1, keepdims=True)
    acc_ref[...] += ce * y

    @pl.when(e == pl.num_programs(1) - 1)
    def _fin():
        o_ref[...] = _layer_norm(ALPHA * x1_ref[...] + acc_ref[...], g_ref[...], b_ref[...])


def _moe_dense(x1b, wg, wu, wd, c, x1, g, b, tm=1024):
    n = x1.shape[0]
    row = lambda i, e: (i, 0)
    fix = lambda i, e: (0, 0)
    return pl.pallas_call(
        _moe_kernel,
        grid=(n // tm, N_EXPERTS),
        in_specs=[
            pl.BlockSpec((tm, D_MODEL), row),
            pl.BlockSpec((None, D_MODEL, D_EXPERT), lambda i, e: (e, 0, 0)),
            pl.BlockSpec((None, D_MODEL, D_EXPERT), lambda i, e: (e, 0, 0)),
            pl.BlockSpec((None, D_EXPERT, D_MODEL), lambda i, e: (e, 0, 0)),
            pl.BlockSpec((tm, LANES), row), pl.BlockSpec((tm, D_MODEL), row),
            pl.BlockSpec((1, D_MODEL), fix), pl.BlockSpec((1, D_MODEL), fix),
        ],
        out_specs=pl.BlockSpec((tm, D_MODEL), row),
        out_shape=jax.ShapeDtypeStruct((n, D_MODEL), F32),
        scratch_shapes=[pltpu.VMEM((tm, D_MODEL), F32)],
        compiler_params=_params(2),
        name="moe_ln2",
    )(x1b, wg, wu, wd, c, x1, g, b)


def _layer(x, mem, p):
    b, s, d = x.shape
    n = b * s
    xf = x.reshape(n, d)
    proj = _matmul(xf, p["w_in"], 512, "in_proj").reshape(b, s, IN_W)
    mkv = _matmul(mem.reshape(-1, d), p["w_mem_kv"], 512, "mem_kv").reshape(b, -1, 2 * M_W)
    oa = _window_attention(proj, p["sink"], p["g_a"])
    ob = _neighborhood_attention(proj, p["na_bias"], p["g_b"])
    om = _memory_attention(proj, mkv, p["g_m"])
    x1, x1b, c = _out_proj(oa.reshape(n, A_Q_W), ob.reshape(n, B_W), om.reshape(n, M_W), p["w_out"], xf,
                           p["ln1_g"], p["ln1_b"], p["wr_pad"], p["rbias"])
    y = _moe_dense(x1b, p["w_gate"], p["w_up"], p["w_down"], c, x1, p["ln2_g"], p["ln2_b"])
    return y.reshape(b, s, d)


def kernel(x_prompt, x_sample, mem_prompt, mem_sample, w_in, w_mem_kv, sink_logits, rpb, grp_norm_g, w_out,
           ln1_g, ln1_b, w_router, router_bias, w_gate, w_up, w_down, ln2_g, ln2_b):
    seq = x_prompt.shape[1]
    assert x_sample.shape[1] == seq
    wr_pad = jnp.pad(w_router.astype(F32), ((0, 0), (0, LANES - N_EXPERTS)))
    rbias = router_bias.astype(F32).reshape(N_EXPERTS, 1)
    layers = []
    for l in range(DEPTH):
        g = grp_norm_g[l].astype(F32)
        layers.append(dict(
            w_in=w_in[l].astype(BF16), w_mem_kv=w_mem_kv[l].astype(BF16),
            sink=sink_logits[l].astype(F32), na_bias=_na_bias(rpb[l], seq),
            g_a=g[:A_Q_W].reshape(1, -1), g_b=g[A_Q_W:A_Q_W + B_W].reshape(1, -1),
            g_m=g[A_Q_W + B_W:].reshape(1, -1),
            w_out=w_out[l].astype(BF16),
            ln1_g=ln1_g[l].astype(F32).reshape(1, -1), ln1_b=ln1_b[l].astype(F32).reshape(1, -1),
            wr_pad=wr_pad, rbias=rbias,
            w_gate=w_gate[l].astype(BF16), w_up=w_up[l].astype(BF16), w_down=w_down[l].astype(BF16),
            ln2_g=ln2_g[l].astype(F32).reshape(1, -1), ln2_b=ln2_b[l].astype(F32).reshape(1, -1)))

    def trunk(x, mem):
        for p in layers:
            x = _layer(x, mem, p)
        return x

    return (trunk(x_prompt, mem_prompt), trunk(x_sample, mem_sample))
```

```python
import functools

import jax
import jax.numpy as jnp
import numpy as np
from jax import lax
from jax.experimental import pallas as pl
from jax.experimental.pallas import tpu as pltpu

F32 = jnp.float32
BF16 = jnp.bfloat16

D_MODEL = 1024
DEPTH = 2
HEAD_DIM = 64
A_HEADS = 8
WINDOW = 128
B_HEADS = 4
GRID_W = 64
NA_ROWS = 8
NA_COLS = 16
M_HEADS = 4
N_EXPERTS = 16
N_GROUPS = 4
D_EXPERT = 512
LN_EPS = 1e-5
A_Q_W = 512
B_W = 256
M_W = 256
IN_W = 1792
ALPHA = (2 * DEPTH) ** 0.25
SCALE = HEAD_DIM ** -0.5
NEG = -1e30

LANES = 128
NA_TILE_ROWS = 2
NA_KEY_ROWS = 10
VMEM_LIMIT = 56 * 1024 * 1024


def _params(n_axes, vmem=VMEM_LIMIT):
    return pltpu.CompilerParams(dimension_semantics=("arbitrary",) * n_axes, vmem_limit_bytes=vmem)


def _dot_nt(a, b):
    return lax.dot_general(a, b, (((1,), (1,)), ((), ())), preferred_element_type=F32)


def _dot(a, b):
    return jnp.dot(a, b, preferred_element_type=F32)


def _lo_mask():
    return lax.broadcasted_iota(jnp.int32, (1, LANES), 1) < HEAD_DIM


def _mm_kernel(x_ref, w_ref, o_ref):
    o_ref[...] = _dot(x_ref[...].astype(BF16), w_ref[...]).astype(o_ref.dtype)


def _matmul(x, w, tm, name):
    n, k = x.shape
    m = w.shape[1]
    tm = min(tm, n)
    assert n % tm == 0
    return pl.pallas_call(
        _mm_kernel,
        grid=(n // tm,),
        in_specs=[pl.BlockSpec((tm, k), lambda i: (i, 0)), pl.BlockSpec((k, m), lambda i: (0, 0))],
        out_specs=pl.BlockSpec((tm, m), lambda i: (i, 0)),
        out_shape=jax.ShapeDtypeStruct((n, m), BF16),
        compiler_params=_params(1),
        name=name,
    )(x, w)


def _group_rms(of, g):
    ms = jnp.mean(of * of, axis=-1, keepdims=True)
    return of * lax.rsqrt(ms + LN_EPS) * g


def _win_kernel(sink_ref, q_ref, k_ref, v_ref, g_ref, o_ref, ks_ref, vs_ref, *, seq, tq):
    i = pl.program_id(1)
    lo = _lo_mask()
    chunk = 512

    @pl.when(i == 0)
    def _build():
        zeros = jnp.zeros((WINDOW, LANES), BF16)
        for t in range(4):
            for dst in (ks_ref, vs_ref):
                dst[t, 0:WINDOW, :] = zeros
                dst[t, seq + WINDOW:seq + 2 * WINDOW, :] = zeros

        def body(c, carry):
            r0 = pl.multiple_of(c * chunk, chunk)
            for src, dst in ((k_ref, ks_ref), (v_ref, vs_ref)):
                x = src[pl.ds(r0, chunk), :].astype(F32)
                xr = pltpu.roll(x, HEAD_DIM, 1)
                rows = pl.ds(r0 + WINDOW, chunk)
                dst[0, rows, :] = jnp.where(lo, x, 0.0).astype(BF16)
                dst[1, rows, :] = jnp.where(lo, 0.0, xr).astype(BF16)
                dst[2, rows, :] = jnp.where(lo, xr, 0.0).astype(BF16)
                dst[3, rows, :] = jnp.where(lo, 0.0, x).astype(BF16)
            return carry

        lax.fori_loop(0, seq // chunk, body, 0)

    kw = 3 * WINDOW
    a_i = lax.broadcasted_iota(jnp.int32, (WINDOW, kw), 0)
    j_i = lax.broadcasted_iota(jnp.int32, (WINDOW, kw), 1)
    dist = a_i + WINDOW - j_i
    absd = jnp.abs(dist).astype(F32)
    in_window = jnp.abs(dist) <= WINDOW
    g = g_ref[...]

    def sub(j, carry):
        row0 = pl.multiple_of(j * WINDOW, WINDOW)
        qs = pl.multiple_of(i * tq + j * WINDOW, WINDOW)
        kpos = qs - WINDOW + j_i
        valid = in_window & (kpos >= 0) & (kpos < seq)
        win = pl.ds(qs, kw)
        outs = []
        for c in range(4):
            h = c // 2
            q2 = q_ref[pl.ds(row0, WINDOW), c * LANES:(c + 1) * LANES]
            num = None
            invs = []
            for t in range(2):
                head = 2 * c + t
                slope = 2.0 ** (-8.0 * (head + 1) / A_HEADS)
                s = _dot_nt(q2, ks_ref[2 * h + t, win, :]) * SCALE - slope * absd
                s = jnp.where(valid, s, NEG)
                sk = sink_ref[head]
                m = jnp.maximum(jnp.max(s, axis=-1, keepdims=True), sk)
                p = jnp.exp(s - m)
                den = jnp.sum(p, axis=-1, keepdims=True) + jnp.exp(sk - m)
                pv = _dot(p.astype(BF16), vs_ref[2 * h + t, win, :])
                num = pv if num is None else num + pv
                invs.append(1.0 / den)
            outs.append(num * jnp.where(lo, invs[0], invs[1]))
        of = jnp.concatenate(outs, axis=1)
        o_ref[pl.ds(row0, WINDOW), :] = _group_rms(of, g).astype(o_ref.dtype)
        return carry

    lax.fori_loop(0, tq // WINDOW, sub, 0)


def _window_attention(proj, sink, g, tq=512):
    b, s, _ = proj.shape
    kern = functools.partial(_win_kernel, seq=s, tq=tq)
    return pl.pallas_call(
        kern,
        grid=(b, s // tq),
        in_specs=[
            pl.BlockSpec(memory_space=pltpu.SMEM),
            pl.BlockSpec((None, tq, A_Q_W), lambda bi, i: (bi, i, 0)),
            pl.BlockSpec((None, s, LANES), lambda bi, i: (bi, 0, 4)),
            pl.BlockSpec((None, s, LANES), lambda bi, i: (bi, 0, 5)),
            pl.BlockSpec((1, A_Q_W), lambda bi, i: (0, 0)),
        ],
        out_specs=pl.BlockSpec((None, tq, A_Q_W), lambda bi, i: (bi, i, 0)),
        out_shape=jax.ShapeDtypeStruct((b, s, A_Q_W), BF16),
        scratch_shapes=[pltpu.VMEM((4, s + 2 * WINDOW, LANES), BF16),
                        pltpu.VMEM((4, s + 2 * WINDOW, LANES), BF16)],
        compiler_params=_params(2),
        name="window_attention",
    )(sink, proj, proj, proj, g)


def _na_tables(seq):
    rows = seq // GRID_W
    kh = min(NA_ROWS, rows)
    assert rows % NA_TILE_ROWS == 0 and rows >= NA_KEY_ROWS and NA_KEY_ROWS % 2 == 0
    nt = rows // NA_TILE_ROWS
    u = np.arange(NA_TILE_ROWS * GRID_W) // GRID_W
    c = np.arange(NA_TILE_ROWS * GRID_W) % GRID_W
    ki = np.arange(NA_KEY_ROWS * GRID_W) // GRID_W
    kc = np.arange(NA_KEY_ROWS * GRID_W) % GRID_W
    cs = np.clip(c - NA_COLS // 2, 0, GRID_W - NA_COLS)
    col_ok = (kc[None, :] >= cs[:, None]) & (kc[None, :] < cs[:, None] + NA_COLS)
    dc = np.clip(kc[None, :] - c[:, None] + NA_COLS - 1, 0, 2 * NA_COLS - 2)
    types, type_of, kstart = [], [], []
    for t in range(nt):
        r0 = t * NA_TILE_ROWS
        k0 = int(np.clip(r0 - kh // 2, 0, rows - NA_KEY_ROWS))
        k0 -= k0 % 2
        r = r0 + u
        rs = np.clip(r - kh // 2, 0, rows - kh)
        kr = k0 + ki
        row_ok = (kr[None, :] >= rs[:, None]) & (kr[None, :] < rs[:, None] + kh)
        assert (rs >= k0).all() and (rs + kh <= k0 + NA_KEY_ROWS).all()
        dr = np.clip(kr[None, :] - r[:, None] + NA_ROWS - 1, 0, 2 * NA_ROWS - 2)
        key = (dr.tobytes(), row_ok.tobytes())
        for idx, (k_, _, _) in enumerate(types):
            if k_ == key:
                type_of.append(idx)
                break
        else:
            type_of.append(len(types))
            types.append((key, dr, row_ok & col_ok))
        kstart.append(k0)
    dr_rows = np.stack([t_[1][::GRID_W, ::GRID_W] for t_ in types])
    ok_all = np.stack([t_[2] for t_ in types])
    dc_cols = dc[:GRID_W, :GRID_W]
    return (np.asarray(type_of, np.int32), np.asarray(kstart, np.int32), dr_rows, dc_cols, ok_all)


def _na_bias(rpb, seq):
    _, _, dr_rows, dc_cols, ok = _na_tables(seq)
    nt = dr_rows.shape[0]
    hi = lax.Precision.HIGHEST
    col_sel = jnp.asarray(np.eye(2 * NA_COLS - 1, dtype=np.float32)[dc_cols])
    row_sel = jnp.asarray(np.eye(2 * NA_ROWS - 1, dtype=np.float32)[dr_rows])
    by_col = jnp.einsum("hdj,ckj->hdck", rpb.astype(F32), col_sel, precision=hi)
    bias = jnp.einsum("tuid,hdck->thucik", row_sel, by_col, precision=hi)
    bias = bias.reshape(nt, B_HEADS, NA_TILE_ROWS * GRID_W, NA_KEY_ROWS * GRID_W)
    return jnp.where(ok[:, None], bias, NEG)


def _na_kernel(type_ref, kstart_ref, q_ref, k_ref, v_ref, bias_ref, g_ref, o_ref, ks_ref, vs_ref, *, seq):
    del type_ref
    i = pl.program_id(1)
    lo = _lo_mask()
    chunk = 512

    @pl.when(i == 0)
    def _build():
        def body(c, carry):
            rows = pl.ds(pl.multiple_of(c * chunk, chunk), chunk)
            for src, dst in ((k_ref, ks_ref), (v_ref, vs_ref)):
                for p in range(2):
                    x = src[rows, p * LANES:(p + 1) * LANES]
                    dst[2 * p, rows, :] = jnp.where(lo, x, jnp.zeros_like(x))
                    dst[2 * p + 1, rows, :] = jnp.where(lo, jnp.zeros_like(x), x)
            return carry

        lax.fori_loop(0, seq // chunk, body, 0)

    nk = NA_KEY_ROWS * GRID_W
    win = pl.ds(pl.multiple_of(kstart_ref[i] * GRID_W, LANES), nk)
    outs = []
    for p in range(2):
        q2 = q_ref[:, p * LANES:(p + 1) * LANES]
        num = None
        invs = []
        for t in range(2):
            s = _dot_nt(q2, ks_ref[2 * p + t, win, :]) * SCALE + bias_ref[2 * p + t]
            m = jnp.max(s, axis=-1, keepdims=True)
            e = jnp.exp(s - m)
            den = jnp.sum(e, axis=-1, keepdims=True)
            pv = _dot(e.astype(BF16), vs_ref[2 * p + t, win, :])
            num = pv if num is None else num + pv
            invs.append(1.0 / den)
        outs.append(num * jnp.where(lo, invs[0], invs[1]))
    of = jnp.concatenate(outs, axis=1)
    o_ref[...] = _group_rms(of, g_ref[...]).astype(o_ref.dtype)


def _neighborhood_attention(proj, bias, g):
    b, s, _ = proj.shape
    type_of, kstart, _, _, _ = _na_tables(s)
    tq = NA_TILE_ROWS * GRID_W
    nk = NA_KEY_ROWS * GRID_W
    kern = functools.partial(_na_kernel, seq=s)
    grid_spec = pltpu.PrefetchScalarGridSpec(
        num_scalar_prefetch=2,
        grid=(b, s // tq),
        in_specs=[
            pl.BlockSpec((None, tq, B_W), lambda bi, i, ty, ks: (bi, i, 3)),
            pl.BlockSpec((None, s, B_W), lambda bi, i, ty, ks: (bi, 0, 4)),
            pl.BlockSpec((None, s, B_W), lambda bi, i, ty, ks: (bi, 0, 5)),
            pl.BlockSpec((None, B_HEADS, tq, nk), lambda bi, i, ty, ks: (ty[i], 0, 0, 0)),
            pl.BlockSpec((1, B_W), lambda bi, i, ty, ks: (0, 0)),
        ],
        out_specs=pl.BlockSpec((None, tq, B_W), lambda bi, i, ty, ks: (bi, i, 0)),
        scratch_shapes=[pltpu.VMEM((4, s, LANES), BF16), pltpu.VMEM((4, s, LANES), BF16)],
    )
    return pl.pallas_call(
        kern,
        grid_spec=grid_spec,
        out_shape=jax.ShapeDtypeStruct((b, s, B_W), BF16),
        compiler_params=_params(2),
        name="neighborhood_attention",
    )(jnp.asarray(type_of), jnp.asarray(kstart), proj, proj, proj, bias, g)


def _mem_kernel(q_ref, mkv_ref, g_ref, o_ref):
    lo = _lo_mask()
    outs = []
    for p in range(2):
        q2 = q_ref[:, p * LANES:(p + 1) * LANES]
        mk = mkv_ref[:, p * LANES:(p + 1) * LANES]
        mv = mkv_ref[:, M_W + p * LANES:M_W + (p + 1) * LANES]
        zero = jnp.zeros_like(mk)
        num = None
        invs = []
        for t in range(2):
            keep = lo if t == 0 else jnp.logical_not(lo)
            s = _dot_nt(q2, jnp.where(keep, mk, zero)) * SCALE
            m = jnp.max(s, axis=-1, keepdims=True)
            e = jnp.exp(s - m)
            den = jnp.sum(e, axis=-1, keepdims=True)
            pv = _dot(e.astype(BF16), jnp.where(keep, mv, zero))
            num = pv if num is None else num + pv
            invs.append(1.0 / den)
        outs.append(num * jnp.where(lo, invs[0], invs[1]))
    of = jnp.concatenate(outs, axis=1)
    o_ref[...] = _group_rms(of, g_ref[...]).astype(o_ref.dtype)


def _memory_attention(proj, mkv, g, tq=512):
    b, s, _ = proj.shape
    mlen = mkv.shape[1]
    return pl.pallas_call(
        _mem_kernel,
        grid=(b, s // tq),
        in_specs=[
            pl.BlockSpec((None, tq, M_W), lambda bi, i: (bi, i, 6)),
            pl.BlockSpec((None, mlen, 2 * M_W), lambda bi, i: (bi, 0, 0)),
            pl.BlockSpec((1, M_W), lambda bi, i: (0, 0)),
        ],
        out_specs=pl.BlockSpec((None, tq, M_W), lambda bi, i: (bi, i, 0)),
        out_shape=jax.ShapeDtypeStruct((b, s, M_W), BF16),
        compiler_params=_params(2),
        name="memory_attention",
    )(proj, mkv, g)


def _layer_norm(z, g, b):
    mu = jnp.mean(z, axis=-1, keepdims=True)
    zc = z - mu
    var = jnp.mean(zc * zc, axis=-1, keepdims=True)
    return zc * lax.rsqrt(var + LN_EPS) * g + b


def _route(lg_t, rbias):
    e, t = lg_t.shape
    eg = e // N_GROUPS
    scores = 1.0 / (1.0 + jnp.exp(-lg_t))
    sel = scores + rbias
    row = lax.broadcasted_iota(jnp.int32, (e, t), 0)
    best = None
    gidx = None
    for gi in range(N_GROUPS):
        r = [sel[gi * eg + k:gi * eg + k + 1, :] for k in range(eg)]
        top2 = None
        for a in range(eg):
            for b in range(a + 1, eg):
                pair = r[a] + r[b]
                top2 = pair if top2 is None else jnp.maximum(top2, pair)
        if best is None:
            best, gidx = top2, jnp.zeros((1, t), jnp.int32)
        else:
            better = top2 > best
            gidx = jnp.where(better, gi, gidx)
            best = jnp.maximum(best, top2)
    in_group = (row // eg) == gidx
    masked = jnp.where(in_group, sel, -jnp.inf)
    m1 = jnp.max(masked, axis=0, keepdims=True)
    i1 = jnp.min(jnp.where(masked == m1, row, e), axis=0, keepdims=True)
    rest = jnp.where(row == i1, -jnp.inf, masked)
    m2 = jnp.max(rest, axis=0, keepdims=True)
    i2 = jnp.min(jnp.where(rest == m2, row, e), axis=0, keepdims=True)
    pick1 = row == i1
    pick2 = row == i2
    w1 = jnp.sum(jnp.where(pick1, scores, 0.0), axis=0, keepdims=True)
    w2 = jnp.sum(jnp.where(pick2, scores, 0.0), axis=0, keepdims=True)
    tot = w1 + w2
    return i1, i2, pick1, pick2, w1 / tot, w2 / tot


def _out_kernel(ma_ref, mb_ref, mm_ref, w_ref, x_ref, g_ref, b_ref, wr_ref, rb_ref, tri_ref,
                x1_ref, info_ref, wcol_ref, cnt_ref, carry_ref):
    @pl.when(pl.program_id(0) == 0)
    def _init():
        carry_ref[...] = jnp.zeros_like(carry_ref)

    y = _dot(ma_ref[...], w_ref[0:A_Q_W, :])
    y = y + _dot(mb_ref[...], w_ref[A_Q_W:A_Q_W + B_W, :])
    y = y + _dot(mm_ref[...], w_ref[A_Q_W + B_W:, :])
    x1 = _layer_norm(ALPHA * x_ref[...] + y, g_ref[...], b_ref[...])
    x1_ref[...] = x1
    lg = jnp.dot(x1, wr_ref[...], preferred_element_type=F32, precision=lax.Precision.HIGHEST)
    lg_t = lg.T[0:N_EXPERTS, :]
    i1, i2, pick1, pick2, w1, w2 = _route(lg_t, rb_ref[...])
    tm = lg_t.shape[1]
    member = jnp.where(jnp.logical_or(pick1, pick2), 1.0, 0.0)
    before = carry_ref[:, 0:1] + _dot(member.astype(BF16), tri_ref[...])
    r1 = jnp.sum(jnp.where(pick1, before, 0.0), axis=0, keepdims=True)
    r2 = jnp.sum(jnp.where(pick2, before, 0.0), axis=0, keepdims=True)
    total = carry_ref[...] + jnp.sum(member, axis=1, keepdims=True)
    carry_ref[...] = total
    cnt_ref[...] = total
    info_ref[...] = jnp.concatenate(
        [i1, i2, r1.astype(jnp.int32), r2.astype(jnp.int32), jnp.zeros((4, tm), jnp.int32)], axis=0)
    w_pad = jnp.concatenate([w1, w2, jnp.zeros((LANES - 2, tm), F32)], axis=0)
    wcol_ref[...] = w_pad.T


def _out_proj(ma, mb, mm, w_out, x, g, b, wr_pad, rbias, tm=512):
    n = x.shape[0]
    row = lambda i: (i, 0)
    fix = lambda i: (0, 0)
    tri = jnp.asarray(np.triu(np.ones((tm, tm), np.float32), k=1), BF16)
    return pl.pallas_call(
        _out_kernel,
        grid=(n // tm,),
        in_specs=[
            pl.BlockSpec((tm, A_Q_W), row), pl.BlockSpec((tm, B_W), row), pl.BlockSpec((tm, M_W), row),
            pl.BlockSpec((D_MODEL, D_MODEL), fix), pl.BlockSpec((tm, D_MODEL), row),
            pl.BlockSpec((1, D_MODEL), fix), pl.BlockSpec((1, D_MODEL), fix),
            pl.BlockSpec((D_MODEL, LANES), fix), pl.BlockSpec((N_EXPERTS, 1), fix),
            pl.BlockSpec((tm, tm), fix),
        ],
        out_specs=[pl.BlockSpec((tm, D_MODEL), row), pl.BlockSpec((8, tm), lambda i: (0, i)),
                   pl.BlockSpec((tm, LANES), row), pl.BlockSpec((N_EXPERTS, LANES), fix)],
        out_shape=[jax.ShapeDtypeStruct((n, D_MODEL), F32), jax.ShapeDtypeStruct((8, n), jnp.int32),
                   jax.ShapeDtypeStruct((n, LANES), F32), jax.ShapeDtypeStruct((N_EXPERTS, LANES), F32)],
        scratch_shapes=[pltpu.VMEM((N_EXPERTS, LANES), F32)],
        compiler_params=_params(1),
        name="out_proj_ln_router",
    )(ma, mb, mm, w_out, x, g, b, wr_pad, rbias, tri)


MOE_ROW_TILE = 512
MOE_TOKEN_TILE = 512


def _index_copy(pos_hbm, idx_ref, isem, step, slot):
    return pltpu.make_async_copy(pos_hbm.at[step], idx_ref.at[slot], isem.at[slot])


def _dispatch_kernel(pos_hbm, x_ref, zeros_hbm, xs_hbm, idx_ref, isem, rsem, *, td):
    del zeros_hbm
    i = pl.program_id(0)
    n = pl.num_programs(0)
    slot = lax.rem(i, 2)

    @pl.when(i == 0)
    def _first():
        _index_copy(pos_hbm, idx_ref, isem, 0, 0).start()

    _index_copy(pos_hbm, idx_ref, isem, i, slot).wait()

    @pl.when(i + 1 < n)
    def _next():
        _index_copy(pos_hbm, idx_ref, isem, i + 1, 1 - slot).start()

    def body(t, carry):
        for k in range(2):
            p = idx_ref[slot, k * td + t]
            pltpu.make_async_copy(x_ref.at[pl.ds(t, 1)], xs_hbm.at[pl.ds(p, 1)], rsem).start()
        return carry

    lax.fori_loop(0, td, body, 0, unroll=8)
    for _ in range(2):
        pltpu.make_async_copy(x_ref, xs_hbm.at[pl.ds(0, td)], rsem).wait()


def _dispatch(x1, pos_t, n_rows, td):
    n = x1.shape[0]
    kern = functools.partial(_dispatch_kernel, td=td)
    return pl.pallas_call(
        kern,
        grid=(n // td,),
        in_specs=[pl.BlockSpec(memory_space=pl.ANY), pl.BlockSpec((td, D_MODEL), lambda i: (i, 0)),
                  pl.BlockSpec(memory_space=pl.ANY)],
        out_specs=pl.BlockSpec(memory_space=pl.ANY),
        out_shape=jax.ShapeDtypeStruct((n_rows, D_MODEL), F32),
        input_output_aliases={2: 0},
        scratch_shapes=[pltpu.SMEM((2, 2 * td), jnp.int32), pltpu.SemaphoreType.DMA((2,)),
                        pltpu.SemaphoreType.DMA(())],
        compiler_params=_params(1),
        name="moe_dispatch",
    )(pos_t, x1, jnp.zeros((n_rows, D_MODEL), F32))


def _gmm_kernel(te_ref, rows_ref, xs_ref, wg_ref, wu_ref, wd_ref, ys_ref):
    del te_ref
    nrows = rows_ref[pl.program_id(0)]

    @pl.when(nrows > 0)
    def _compute():
        x = xs_ref[...].astype(BF16)
        gate = _dot(x, wg_ref[...])
        up = _dot(x, wu_ref[...])
        h = (gate * (1.0 / (1.0 + jnp.exp(-gate))) * up).astype(BF16)
        ys_ref[...] = _dot(h, wd_ref[...])

    @pl.when(nrows == 0)
    def _empty():
        ys_ref[...] = jnp.zeros_like(ys_ref)


def _grouped_mlp(xs, tile_expert, tile_rows, wg, wu, wd, tm):
    n_rows = xs.shape[0]
    wspec = lambda shape: pl.BlockSpec((None,) + shape, lambda g, te, tr: (te[g], 0, 0))
    grid_spec = pltpu.PrefetchScalarGridSpec(
        num_scalar_prefetch=2,
        grid=(n_rows // tm,),
        in_specs=[pl.BlockSpec((tm, D_MODEL), lambda g, te, tr: (g, 0)),
                  wspec((D_MODEL, D_EXPERT)), wspec((D_MODEL, D_EXPERT)), wspec((D_EXPERT, D_MODEL))],
        out_specs=pl.BlockSpec((tm, D_MODEL), lambda g, te, tr: (g, 0)),
    )
    return pl.pallas_call(
        _gmm_kernel,
        grid_spec=grid_spec,
        out_shape=jax.ShapeDtypeStruct((n_rows, D_MODEL), F32),
        compiler_params=_params(1),
        name="moe_grouped_mlp",
    )(tile_expert, tile_rows, xs, wg, wu, wd)


def _combine_kernel(pos_hbm, ys_hbm, x1_ref, w_ref, g_ref, b_ref, o_ref, idx_ref, buf_ref, isem, rsem, *, td):
    i = pl.program_id(0)
    n = pl.num_programs(0)
    slot = lax.rem(i, 2)

    def issue_rows(sl):
        def body(t, carry):
            for k in range(2):
                p = idx_ref[sl, k * td + t]
                pltpu.make_async_copy(ys_hbm.at[pl.ds(p, 1)], buf_ref.at[sl, k, pl.ds(t, 1)], rsem.at[sl]).start()
            return carry

        lax.fori_loop(0, td, body, 0, unroll=8)

    @pl.when(i == 0)
    def _first():
        _index_copy(pos_hbm, idx_ref, isem, 0, 0).start()
        _index_copy(pos_hbm, idx_ref, isem, 0, 0).wait()
        issue_rows(0)

        @pl.when(n > 1)
        def _():
            _index_copy(pos_hbm, idx_ref, isem, 1, 1).start()

    @pl.when(i + 1 < n)
    def _next():
        _index_copy(pos_hbm, idx_ref, isem, i + 1, 1 - slot).wait()
        issue_rows(1 - slot)

        @pl.when(i + 2 < n)
        def _():
            _index_copy(pos_hbm, idx_ref, isem, i + 2, slot).start()

    for k in range(2):
        pltpu.make_async_copy(ys_hbm.at[pl.ds(0, td)], buf_ref.at[slot, k], rsem.at[slot]).wait()
    w = w_ref[...]
    y = w[:, 0:1] * buf_ref[slot, 0] + w[:, 1:2] * buf_ref[slot, 1]
    o_ref[...] = _layer_norm(ALPHA * x1_ref[...] + y, g_ref[...], b_ref[...])


def _combine(ys, pos_t, x1, wcol, g, b, td):
    n = x1.shape[0]
    row = lambda i: (i, 0)
    fix = lambda i: (0, 0)
    kern = functools.partial(_combine_kernel, td=td)
    return pl.pallas_call(
        kern,
        grid=(n // td,),
        in_specs=[pl.BlockSpec(memory_space=pl.ANY), pl.BlockSpec(memory_space=pl.ANY),
                  pl.BlockSpec((td, D_MODEL), row), pl.BlockSpec((td, LANES), row),
                  pl.BlockSpec((1, D_MODEL), fix), pl.BlockSpec((1, D_MODEL), fix)],
        out_specs=pl.BlockSpec((td, D_MODEL), row),
        out_shape=jax.ShapeDtypeStruct((n, D_MODEL), F32),
        scratch_shapes=[pltpu.SMEM((2, 2 * td), jnp.int32), pltpu.VMEM((2, 2, td, D_MODEL), F32),
                        pltpu.SemaphoreType.DMA((2,)), pltpu.SemaphoreType.DMA((2,))],
        compiler_params=_params(1),
        name="moe_combine_ln2",
    )(pos_t, ys, x1, wcol, g, b)


def _moe_routed(x1, info, wcol, cnt, p):
    n = x1.shape[0]
    tm, td = MOE_ROW_TILE, MOE_TOKEN_TILE
    n_tiles = 2 * n // tm + N_EXPERTS
    counts = cnt[:, 0].astype(jnp.int32)
    padded = (counts + (tm - 1)) // tm * tm
    ends = jnp.cumsum(padded)
    offs = ends - padded
    starts = jnp.arange(n_tiles, dtype=jnp.int32) * tm
    te = jnp.sum((ends[None, :] <= starts[:, None]).astype(jnp.int32), axis=1)
    tile_expert = jnp.minimum(te, N_EXPERTS - 1)
    experts = jnp.arange(N_EXPERTS, dtype=jnp.int32)
    pick = tile_expert[:, None] == experts[None, :]
    seg_end = jnp.sum(jnp.where(pick, (offs + counts)[None, :], 0), axis=1)
    tile_rows = jnp.where(te < N_EXPERTS, jnp.clip(seg_end - starts, 0, tm), 0).astype(jnp.int32)

    def position(e, r):
        return r + jnp.sum(jnp.where(e[None, :] == experts[:, None], offs[:, None], 0), axis=0)

    pos1 = position(info[0], info[2]).reshape(n // td, td)
    pos2 = position(info[1], info[3]).reshape(n // td, td)
    pos_t = jnp.concatenate([pos1, pos2], axis=1)
    xs = _dispatch(x1, pos_t, n_tiles * tm, td)
    ys = _grouped_mlp(xs, tile_expert, tile_rows, p["w_gate"], p["w_up"], p["w_down"], tm)
    return _combine(ys, pos_t, x1, wcol, p["ln2_g"], p["ln2_b"], td)


def _layer(x, mem, p):
    b, s, d = x.shape
    n = b * s
    xf = x.reshape(n, d)
    proj = _matmul(xf, p["w_in"], 512, "in_proj").reshape(b, s, IN_W)
    mkv = _matmul(mem.reshape(-1, d), p["w_mem_kv"], 512, "mem_kv").reshape(b, -1, 2 * M_W)
    oa = _window_attention(proj, p["sink"], p["g_a"])
    ob = _neighborhood_attention(proj, p["na_bias"], p["g_b"])
    om = _memory_attention(proj, mkv, p["g_m"])
    x1, info, wcol, cnt = _out_proj(oa.reshape(n, A_Q_W), ob.reshape(n, B_W), om.reshape(n, M_W), p["w_out"], xf,
                                    p["ln1_g"], p["ln1_b"], p["wr_pad"], p["rbias"])
    y = _moe_routed(x1, info, wcol, cnt, p)
    return y.reshape(b, s, d)


def kernel(x_prompt, x_sample, mem_prompt, mem_sample, w_in, w_mem_kv, sink_logits, rpb, grp_norm_g, w_out,
           ln1_g, ln1_b, w_router, router_bias, w_gate, w_up, w_down, ln2_g, ln2_b):
    seq = x_prompt.shape[1]
    assert x_sample.shape[1] == seq
    wr_pad = jnp.pad(w_router.astype(F32), ((0, 0), (0, LANES - N_EXPERTS)))
    rbias = router_bias.astype(F32).reshape(N_EXPERTS, 1)
    layers = []
    for l in range(DEPTH):
        g = grp_norm_g[l].astype(F32)
        layers.append(dict(
            w_in=w_in[l].astype(BF16), w_mem_kv=w_mem_kv[l].astype(BF16),
            sink=sink_logits[l].astype(F32), na_bias=_na_bias(rpb[l], seq),
            g_a=g[:A_Q_W].reshape(1, -1), g_b=g[A_Q_W:A_Q_W + B_W].reshape(1, -1),
            g_m=g[A_Q_W + B_W:].reshape(1, -1),
            w_out=w_out[l].astype(BF16),
            ln1_g=ln1_g[l].astype(F32).reshape(1, -1), ln1_b=ln1_b[l].astype(F32).reshape(1, -1),
            wr_pad=wr_pad, rbias=rbias,
            w_gate=w_gate[l].astype(BF16), w_up=w_up[l].astype(BF16), w_down=w_down[l].astype(BF16),
            ln2_g=ln2_g[l].astype(F32).reshape(1, -1), ln2_b=ln2_b[l].astype(F32).reshape(1, -1)))

    def trunk(x, mem):
        for p in layers:
            x = _layer(x, mem, p)
        return x

    return (trunk(x_prompt, mem_prompt), trunk(x_sample, mem_sample))
```

```python
import functools

import jax
import jax.numpy as jnp
import numpy as np
from jax import lax
from jax.experimental import pallas as pl
from jax.experimental.pallas import tpu as pltpu

F32 = jnp.float32
BF16 = jnp.bfloat16

D_MODEL = 1024
DEPTH = 2
HEAD_DIM = 64
A_HEADS = 8
WINDOW = 128
B_HEADS = 4
GRID_W = 64
NA_ROWS = 8
NA_COLS = 16
M_HEADS = 4
N_EXPERTS = 16
N_GROUPS = 4
D_EXPERT = 512
LN_EPS = 1e-5
A_Q_W = 512
B_W = 256
M_W = 256
IN_W = 1792
ALPHA = (2 * DEPTH) ** 0.25
SCALE = HEAD_DIM ** -0.5
NEG = -1e30

LANES = 128
NA_TILE_ROWS = 2
NA_KEY_ROWS = 10
VMEM_LIMIT = 56 * 1024 * 1024


def _params(n_axes, vmem=VMEM_LIMIT):
    return pltpu.CompilerParams(dimension_semantics=("arbitrary",) * n_axes, vmem_limit_bytes=vmem)


def _dot_nt(a, b):
    return lax.dot_general(a, b, (((1,), (1,)), ((), ())), preferred_element_type=F32)


def _dot(a, b):
    return jnp.dot(a, b, preferred_element_type=F32)


def _lo_mask():
    return lax.broadcasted_iota(jnp.int32, (1, LANES), 1) < HEAD_DIM


def _mm_kernel(x_ref, w_ref, o_ref):
    o_ref[...] = _dot(x_ref[...].astype(BF16), w_ref[...]).astype(o_ref.dtype)


def _matmul(x, w, tm, name):
    n, k = x.shape
    m = w.shape[1]
    tm = min(tm, n)
    assert n % tm == 0
    return pl.pallas_call(
        _mm_kernel,
        grid=(n // tm,),
        in_specs=[pl.BlockSpec((tm, k), lambda i: (i, 0)), pl.BlockSpec((k, m), lambda i: (0, 0))],
        out_specs=pl.BlockSpec((tm, m), lambda i: (i, 0)),
        out_shape=jax.ShapeDtypeStruct((n, m), BF16),
        compiler_params=_params(1),
        name=name,
    )(x, w)


def _group_rms(of, g):
    ms = jnp.mean(of * of, axis=-1, keepdims=True)
    return of * lax.rsqrt(ms + LN_EPS) * g


def _win_kernel(sink_ref, q_ref, k_ref, v_ref, g_ref, o_ref, ks_ref, vs_ref, bias_ref, s_ref, p_ref, *,
                seq, tq):
    i = pl.program_id(1)
    lo = _lo_mask()
    chunk = 512

    @pl.when(i == 0)
    def _build():
        zeros = jnp.zeros((WINDOW, LANES), BF16)
        for t in range(4):
            for dst in (ks_ref, vs_ref):
                dst[t, 0:WINDOW, :] = zeros
                dst[t, seq + WINDOW:seq + 2 * WINDOW, :] = zeros

        def body(c, carry):
            r0 = pl.multiple_of(c * chunk, chunk)
            for src, dst in ((k_ref, ks_ref), (v_ref, vs_ref)):
                x = src[pl.ds(r0, chunk), :].astype(F32)
                xr = pltpu.roll(x, HEAD_DIM, 1)
                rows = pl.ds(r0 + WINDOW, chunk)
                dst[0, rows, :] = jnp.where(lo, x, 0.0).astype(BF16)
                dst[1, rows, :] = jnp.where(lo, 0.0, xr).astype(BF16)
                dst[2, rows, :] = jnp.where(lo, xr, 0.0).astype(BF16)
                dst[3, rows, :] = jnp.where(lo, 0.0, x).astype(BF16)
            return carry

        lax.fori_loop(0, seq // chunk, body, 0)

    kw = 3 * WINDOW

    @pl.when(jnp.logical_and(pl.program_id(0) == 0, i == 0))
    def _build_bias():
        a_i = lax.broadcasted_iota(jnp.int32, (WINDOW, kw), 0)
        j_i = lax.broadcasted_iota(jnp.int32, (WINDOW, kw), 1)
        dist = jnp.abs(a_i + WINDOW - j_i)
        absd = dist.astype(F32)
        for variant in range(3):
            valid = dist <= WINDOW
            if variant == 0:
                valid = valid & (j_i >= WINDOW)
            if variant == 2:
                valid = valid & (j_i < 2 * WINDOW)
            for head in range(A_HEADS):
                slope = 2.0 ** (-8.0 * (head + 1) / A_HEADS)
                bias_ref[variant * A_HEADS + head] = jnp.where(valid, -slope * absd, NEG)

    g = g_ref[...]

    def sub(j, carry):
        row0 = pl.multiple_of(j * WINDOW, WINDOW)
        qs = pl.multiple_of(i * tq + j * WINDOW, WINDOW)
        variant = jnp.where(qs == 0, 0, jnp.where(qs == seq - WINDOW, 2, 1))
        win = pl.ds(qs, kw)
        for head in range(A_HEADS):
            c, t = divmod(head, 2)
            q2 = q_ref[pl.ds(row0, WINDOW), c * LANES:(c + 1) * LANES]
            s_ref[head] = _dot_nt(q2, ks_ref[2 * (c // 2) + t, win, :]) + bias_ref[variant * A_HEADS + head]
        invs = []
        for head in range(A_HEADS):
            s = s_ref[head]
            sk = sink_ref[head]
            m = jnp.maximum(jnp.max(s, axis=-1, keepdims=True), sk)
            p = jnp.exp(s - m)
            den = jnp.sum(p, axis=-1, keepdims=True) + jnp.exp(sk - m)
            p_ref[head] = p.astype(BF16)
            invs.append(1.0 / den)
        outs = []
        for c in range(4):
            h = c // 2
            num = (_dot(p_ref[2 * c], vs_ref[2 * h, win, :])
                   + _dot(p_ref[2 * c + 1], vs_ref[2 * h + 1, win, :]))
            outs.append(num * jnp.where(lo, invs[2 * c], invs[2 * c + 1]))
        of = jnp.concatenate(outs, axis=1)
        o_ref[pl.ds(row0, WINDOW), :] = _group_rms(of, g).astype(o_ref.dtype)
        return carry

    lax.fori_loop(0, tq // WINDOW, sub, 0)


def _window_attention(proj, sink, g, tq=512):
    b, s, _ = proj.shape
    assert s % tq == 0 and s >= 2 * WINDOW
    kern = functools.partial(_win_kernel, seq=s, tq=tq)
    return pl.pallas_call(
        kern,
        grid=(b, s // tq),
        in_specs=[
            pl.BlockSpec(memory_space=pltpu.SMEM),
            pl.BlockSpec((None, tq, A_Q_W), lambda bi, i: (bi, i, 0)),
            pl.BlockSpec((None, s, LANES), lambda bi, i: (bi, 0, 4)),
            pl.BlockSpec((None, s, LANES), lambda bi, i: (bi, 0, 5)),
            pl.BlockSpec((1, A_Q_W), lambda bi, i: (0, 0)),
        ],
        out_specs=pl.BlockSpec((None, tq, A_Q_W), lambda bi, i: (bi, i, 0)),
        out_shape=jax.ShapeDtypeStruct((b, s, A_Q_W), BF16),
        scratch_shapes=[pltpu.VMEM((4, s + 2 * WINDOW, LANES), BF16),
                        pltpu.VMEM((4, s + 2 * WINDOW, LANES), BF16),
                        pltpu.VMEM((3 * A_HEADS, WINDOW, 3 * WINDOW), F32),
                        pltpu.VMEM((A_HEADS, WINDOW, 3 * WINDOW), F32),
                        pltpu.VMEM((A_HEADS, WINDOW, 3 * WINDOW), BF16)],
        compiler_params=_params(2),
        name="window_attention",
    )(sink, proj, proj, proj, g)


def _na_tables(seq):
    rows = seq // GRID_W
    kh = min(NA_ROWS, rows)
    assert rows % NA_TILE_ROWS == 0 and rows >= NA_KEY_ROWS and NA_KEY_ROWS % 2 == 0
    nt = rows // NA_TILE_ROWS
    u = np.arange(NA_TILE_ROWS * GRID_W) // GRID_W
    c = np.arange(NA_TILE_ROWS * GRID_W) % GRID_W
    ki = np.arange(NA_KEY_ROWS * GRID_W) // GRID_W
    kc = np.arange(NA_KEY_ROWS * GRID_W) % GRID_W
    cs = np.clip(c - NA_COLS // 2, 0, GRID_W - NA_COLS)
    col_ok = (kc[None, :] >= cs[:, None]) & (kc[None, :] < cs[:, None] + NA_COLS)
    dc = np.clip(kc[None, :] - c[:, None] + NA_COLS - 1, 0, 2 * NA_COLS - 2)
    types, type_of, kstart = [], [], []
    for t in range(nt):
        r0 = t * NA_TILE_ROWS
        k0 = int(np.clip(r0 - kh // 2, 0, rows - NA_KEY_ROWS))
        k0 -= k0 % 2
        r = r0 + u
        rs = np.clip(r - kh // 2, 0, rows - kh)
        kr = k0 + ki
        row_ok = (kr[None, :] >= rs[:, None]) & (kr[None, :] < rs[:, None] + kh)
        assert (rs >= k0).all() and (rs + kh <= k0 + NA_KEY_ROWS).all()
        dr = np.clip(kr[None, :] - r[:, None] + NA_ROWS - 1, 0, 2 * NA_ROWS - 2)
        key = (dr.tobytes(), row_ok.tobytes())
        for idx, (k_, _, _) in enumerate(types):
            if k_ == key:
                type_of.append(idx)
                break
        else:
            type_of.append(len(types))
            types.append((key, dr, row_ok & col_ok))
        kstart.append(k0)
    dr_rows = np.stack([t_[1][::GRID_W, ::GRID_W] for t_ in types])
    ok_all = np.stack([t_[2] for t_ in types])
    dc_cols = dc[:GRID_W, :GRID_W]
    return (np.asarray(type_of, np.int32), np.asarray(kstart, np.int32), dr_rows, dc_cols, ok_all)


def _na_bias(rpb, seq):
    _, _, dr_rows, dc_cols, ok = _na_tables(seq)
    nt = dr_rows.shape[0]
    hi = lax.Precision.HIGHEST
    col_sel = jnp.asarray(np.eye(2 * NA_COLS - 1, dtype=np.float32)[dc_cols])
    row_sel = jnp.asarray(np.eye(2 * NA_ROWS - 1, dtype=np.float32)[dr_rows])
    by_col = jnp.einsum("hdj,ckj->hdck", rpb.astype(F32), col_sel, precision=hi)
    bias = jnp.einsum("tuid,hdck->thucik", row_sel, by_col, precision=hi)
    bias = bias.reshape(nt, B_HEADS, NA_TILE_ROWS * GRID_W, NA_KEY_ROWS * GRID_W)
    return jnp.where(ok[:, None], bias, NEG)


def _na_kernel(type_ref, kstart_ref, q_ref, k_ref, v_ref, bias_ref, g_ref, o_ref, ks_ref, vs_ref, s_ref, p_ref,
               *, seq):
    del type_ref
    i = pl.program_id(1)
    lo = _lo_mask()
    chunk = 512

    @pl.when(i == 0)
    def _build():
        def body(c, carry):
            rows = pl.ds(pl.multiple_of(c * chunk, chunk), chunk)
            for src, dst in ((k_ref, ks_ref), (v_ref, vs_ref)):
                for p in range(2):
                    x = src[rows, p * LANES:(p + 1) * LANES]
                    dst[2 * p, rows, :] = jnp.where(lo, x, jnp.zeros_like(x))
                    dst[2 * p + 1, rows, :] = jnp.where(lo, jnp.zeros_like(x), x)
            return carry

        lax.fori_loop(0, seq // chunk, body, 0)

    nk = NA_KEY_ROWS * GRID_W
    win = pl.ds(pl.multiple_of(kstart_ref[i] * GRID_W, LANES), nk)
    for head in range(B_HEADS):
        q2 = q_ref[:, (head // 2) * LANES:(head // 2 + 1) * LANES]
        s_ref[head] = _dot_nt(q2, ks_ref[head, win, :]) + bias_ref[head]
    invs = []
    for head in range(B_HEADS):
        s = s_ref[head]
        m = jnp.max(s, axis=-1, keepdims=True)
        e = jnp.exp(s - m)
        invs.append(1.0 / jnp.sum(e, axis=-1, keepdims=True))
        p_ref[head] = e.astype(BF16)
    outs = []
    for p in range(2):
        num = _dot(p_ref[2 * p], vs_ref[2 * p, win, :]) + _dot(p_ref[2 * p + 1], vs_ref[2 * p + 1, win, :])
        outs.append(num * jnp.where(lo, invs[2 * p], invs[2 * p + 1]))
    of = jnp.concatenate(outs, axis=1)
    o_ref[...] = _group_rms(of, g_ref[...]).astype(o_ref.dtype)


def _neighborhood_attention(proj, bias, g):
    b, s, _ = proj.shape
    type_of, kstart, _, _, _ = _na_tables(s)
    tq = NA_TILE_ROWS * GRID_W
    nk = NA_KEY_ROWS * GRID_W
    kern = functools.partial(_na_kernel, seq=s)
    grid_spec = pltpu.PrefetchScalarGridSpec(
        num_scalar_prefetch=2,
        grid=(b, s // tq),
        in_specs=[
            pl.BlockSpec((None, tq, B_W), lambda bi, i, ty, ks: (bi, i, 3)),
            pl.BlockSpec((None, s, B_W), lambda bi, i, ty, ks: (bi, 0, 4)),
            pl.BlockSpec((None, s, B_W), lambda bi, i, ty, ks: (bi, 0, 5)),
            pl.BlockSpec((None, B_HEADS, tq, nk), lambda bi, i, ty, ks: (ty[i], 0, 0, 0)),
            pl.BlockSpec((1, B_W), lambda bi, i, ty, ks: (0, 0)),
        ],
        out_specs=pl.BlockSpec((None, tq, B_W), lambda bi, i, ty, ks: (bi, i, 0)),
        scratch_shapes=[pltpu.VMEM((4, s, LANES), BF16), pltpu.VMEM((4, s, LANES), BF16),
                        pltpu.VMEM((B_HEADS, tq, nk), F32), pltpu.VMEM((B_HEADS, tq, nk), BF16)],
    )
    return pl.pallas_call(
        kern,
        grid_spec=grid_spec,
        out_shape=jax.ShapeDtypeStruct((b, s, B_W), BF16),
        compiler_params=_params(2),
        name="neighborhood_attention",
    )(jnp.asarray(type_of), jnp.asarray(kstart), proj, proj, proj, bias, g)


def _mem_kernel(q_ref, mkv_ref, g_ref, o_ref):
    lo = _lo_mask()
    outs = []
    for p in range(2):
        q2 = q_ref[:, p * LANES:(p + 1) * LANES]
        mk = mkv_ref[:, p * LANES:(p + 1) * LANES]
        mv = mkv_ref[:, M_W + p * LANES:M_W + (p + 1) * LANES]
        zero = jnp.zeros_like(mk)
        num = None
        invs = []
        for t in range(2):
            keep = lo if t == 0 else jnp.logical_not(lo)
            s = _dot_nt(q2, jnp.where(keep, mk, zero))
            m = jnp.max(s, axis=-1, keepdims=True)
            e = jnp.exp(s - m)
            den = jnp.sum(e, axis=-1, keepdims=True)
            pv = _dot(e.astype(BF16), jnp.where(keep, mv, zero))
            num = pv if num is None else num + pv
            invs.append(1.0 / den)
        outs.append(num * jnp.where(lo, invs[0], invs[1]))
    of = jnp.concatenate(outs, axis=1)
    o_ref[...] = _group_rms(of, g_ref[...]).astype(o_ref.dtype)


def _memory_attention(proj, mkv, g, tq=512):
    b, s, _ = proj.shape
    mlen = mkv.shape[1]
    return pl.pallas_call(
        _mem_kernel,
        grid=(b, s // tq),
        in_specs=[
            pl.BlockSpec((None, tq, M_W), lambda bi, i: (bi, i, 6)),
            pl.BlockSpec((None, mlen, 2 * M_W), lambda bi, i: (bi, 0, 0)),
            pl.BlockSpec((1, M_W), lambda bi, i: (0, 0)),
        ],
        out_specs=pl.BlockSpec((None, tq, M_W), lambda bi, i: (bi, i, 0)),
        out_shape=jax.ShapeDtypeStruct((b, s, M_W), BF16),
        compiler_params=_params(2),
        name="memory_attention",
    )(proj, mkv, g)


def _layer_norm(z, g, b):
    mu = jnp.mean(z, axis=-1, keepdims=True)
    zc = z - mu
    var = jnp.mean(zc * zc, axis=-1, keepdims=True)
    return zc * lax.rsqrt(var + LN_EPS) * g + b


def _route(lg_t, rbias):
    e, t = lg_t.shape
    eg = e // N_GROUPS
    scores = 1.0 / (1.0 + jnp.exp(-lg_t))
    sel = scores + rbias
    row = lax.broadcasted_iota(jnp.int32, (e, t), 0)
    best = None
    gidx = None
    for gi in range(N_GROUPS):
        r = [sel[gi * eg + k:gi * eg + k + 1, :] for k in range(eg)]
        top2 = None
        for a in range(eg):
            for b in range(a + 1, eg):
                pair = r[a] + r[b]
                top2 = pair if top2 is None else jnp.maximum(top2, pair)
        if best is None:
            best, gidx = top2, jnp.zeros((1, t), jnp.int32)
        else:
            better = top2 > best
            gidx = jnp.where(better, gi, gidx)
            best = jnp.maximum(best, top2)
    in_group = (row // eg) == gidx
    masked = jnp.where(in_group, sel, -jnp.inf)
    m1 = jnp.max(masked, axis=0, keepdims=True)
    i1 = jnp.min(jnp.where(masked == m1, row, e), axis=0, keepdims=True)
    rest = jnp.where(row == i1, -jnp.inf, masked)
    m2 = jnp.max(rest, axis=0, keepdims=True)
    i2 = jnp.min(jnp.where(rest == m2, row, e), axis=0, keepdims=True)
    pick1 = row == i1
    pick2 = row == i2
    w1 = jnp.sum(jnp.where(pick1, scores, 0.0), axis=0, keepdims=True)
    w2 = jnp.sum(jnp.where(pick2, scores, 0.0), axis=0, keepdims=True)
    tot = w1 + w2
    return i1, i2, pick1, pick2, w1 / tot, w2 / tot


def _out_kernel(ma_ref, mb_ref, mm_ref, w_ref, x_ref, g_ref, b_ref, wr_ref, rb_ref, tri_ref,
                x1_ref, info_ref, wcol_ref, cnt_ref, carry_ref):
    @pl.when(pl.program_id(0) == 0)
    def _init():
        carry_ref[...] = jnp.zeros_like(carry_ref)

    y = _dot(ma_ref[...], w_ref[0:A_Q_W, :])
    y = y + _dot(mb_ref[...], w_ref[A_Q_W:A_Q_W + B_W, :])
    y = y + _dot(mm_ref[...], w_ref[A_Q_W + B_W:, :])
    x1 = _layer_norm(ALPHA * x_ref[...] + y, g_ref[...], b_ref[...])
    x1_ref[...] = x1
    x_hi = x1.astype(BF16)
    x_lo = (x1 - x_hi.astype(F32)).astype(BF16)
    hi2 = _dot(x_hi, wr_ref[...])
    lg = hi2[:, 0:LANES] + hi2[:, LANES:] + _dot(x_lo, wr_ref[:, 0:LANES])
    lg_t = lg.T[0:N_EXPERTS, :]
    i1, i2, pick1, pick2, w1, w2 = _route(lg_t, rb_ref[...])
    tm = lg_t.shape[1]
    member = jnp.where(jnp.logical_or(pick1, pick2), 1.0, 0.0)
    before = carry_ref[:, 0:1] + _dot(member.astype(BF16), tri_ref[...])
    r1 = jnp.sum(jnp.where(pick1, before, 0.0), axis=0, keepdims=True)
    r2 = jnp.sum(jnp.where(pick2, before, 0.0), axis=0, keepdims=True)
    total = carry_ref[...] + jnp.sum(member, axis=1, keepdims=True)
    carry_ref[...] = total
    cnt_ref[...] = total
    info_ref[...] = jnp.concatenate(
        [i1, i2, r1.astype(jnp.int32), r2.astype(jnp.int32), jnp.zeros((4, tm), jnp.int32)], axis=0)
    w_pad = jnp.concatenate([w1, w2, jnp.zeros((LANES - 2, tm), F32)], axis=0)
    wcol_ref[...] = w_pad.T


def _out_proj(ma, mb, mm, w_out, x, g, b, wr_cat, rbias, tm=512):
    n = x.shape[0]
    row = lambda i: (i, 0)
    fix = lambda i: (0, 0)
    tri = jnp.asarray(np.triu(np.ones((tm, tm), np.float32), k=1), BF16)
    return pl.pallas_call(
        _out_kernel,
        grid=(n // tm,),
        in_specs=[
            pl.BlockSpec((tm, A_Q_W), row), pl.BlockSpec((tm, B_W), row), pl.BlockSpec((tm, M_W), row),
            pl.BlockSpec((D_MODEL, D_MODEL), fix), pl.BlockSpec((tm, D_MODEL), row),
            pl.BlockSpec((1, D_MODEL), fix), pl.BlockSpec((1, D_MODEL), fix),
            pl.BlockSpec((D_MODEL, 2 * LANES), fix), pl.BlockSpec((N_EXPERTS, 1), fix),
            pl.BlockSpec((tm, tm), fix),
        ],
        out_specs=[pl.BlockSpec((tm, D_MODEL), row), pl.BlockSpec((8, tm), lambda i: (0, i)),
                   pl.BlockSpec((tm, LANES), row), pl.BlockSpec((N_EXPERTS, LANES), fix)],
        out_shape=[jax.ShapeDtypeStruct((n, D_MODEL), F32), jax.ShapeDtypeStruct((8, n), jnp.int32),
                   jax.ShapeDtypeStruct((n, LANES), F32), jax.ShapeDtypeStruct((N_EXPERTS, LANES), F32)],
        scratch_shapes=[pltpu.VMEM((N_EXPERTS, LANES), F32)],
        compiler_params=_params(1),
        name="out_proj_ln_router",
    )(ma, mb, mm, w_out, x, g, b, wr_cat, rbias, tri)


MOE_ROW_TILE = 512
MOE_TOKEN_TILE = 512


SUBLANES = 8


def _index_copy(pos_hbm, idx_ref, isem, step, slot):
    width = pos_hbm.shape[1]
    dst = idx_ref.at[pl.ds(pl.multiple_of(slot * width, width), width)]
    return pltpu.make_async_copy(pos_hbm.at[step], dst, isem.at[slot])


def _row_of(ref, p):
    return ref.at[lax.shift_right_logical(p, 3), pl.ds(jnp.bitwise_and(p, SUBLANES - 1), 1)]


def _dispatch_kernel(pos_hbm, x_ref, zeros_hbm, xs_hbm, idx_ref, isem, rsem, *, td):
    del zeros_hbm
    i = pl.program_id(0)
    n = pl.num_programs(0)
    slot = lax.rem(i, 2)

    @pl.when(i == 0)
    def _first():
        _index_copy(pos_hbm, idx_ref, isem, 0, 0).start()

    _index_copy(pos_hbm, idx_ref, isem, i, slot).wait()

    @pl.when(i + 1 < n)
    def _next():
        _index_copy(pos_hbm, idx_ref, isem, i + 1, 1 - slot).start()

    base = slot * (2 * td)

    def body(j, carry):
        for u in range(SUBLANES):
            for k in range(2):
                p = idx_ref[base + k * td + j * SUBLANES + u]
                pltpu.make_async_copy(x_ref.at[j, pl.ds(u, 1)], _row_of(xs_hbm, p), rsem).start(priority=k)
        return carry

    lax.fori_loop(0, td // SUBLANES, body, 0)
    for _ in range(2):
        pltpu.make_async_copy(x_ref, xs_hbm.at[pl.ds(0, td // SUBLANES)], rsem).wait()


def _dispatch(x1, pos_t, n_rows, td):
    n = x1.shape[0]
    kern = functools.partial(_dispatch_kernel, td=td)
    return pl.pallas_call(
        kern,
        grid=(n // td,),
        in_specs=[pl.BlockSpec(memory_space=pl.ANY),
                  pl.BlockSpec((td // SUBLANES, SUBLANES, D_MODEL), lambda i: (i, 0, 0)),
                  pl.BlockSpec(memory_space=pl.ANY)],
        out_specs=pl.BlockSpec(memory_space=pl.ANY),
        out_shape=jax.ShapeDtypeStruct((n_rows // SUBLANES, SUBLANES, D_MODEL), F32),
        input_output_aliases={2: 0},
        scratch_shapes=[pltpu.SMEM((4 * td,), jnp.int32), pltpu.SemaphoreType.DMA((2,)),
                        pltpu.SemaphoreType.DMA(())],
        compiler_params=_params(1),
        name="moe_dispatch",
    )(pos_t, x1.reshape(n // SUBLANES, SUBLANES, D_MODEL),
      jnp.zeros((n_rows // SUBLANES, SUBLANES, D_MODEL), F32)).reshape(n_rows, D_MODEL)


def _gmm_kernel(te_ref, rows_ref, xs_ref, wg_ref, wu_ref, wd_ref, ys_ref):
    del te_ref
    nrows = rows_ref[pl.program_id(0)]

    @pl.when(nrows > 0)
    def _compute():
        x = xs_ref[...].astype(BF16)
        gate = _dot(x, wg_ref[...])
        up = _dot(x, wu_ref[...])
        h = (gate * (1.0 / (1.0 + jnp.exp(-gate))) * up).astype(BF16)
        ys_ref[...] = _dot(h, wd_ref[...])

    @pl.when(nrows == 0)
    def _empty():
        ys_ref[...] = jnp.zeros_like(ys_ref)


def _grouped_mlp(xs, tile_expert, tile_rows, wg, wu, wd, tm):
    n_rows = xs.shape[0]
    wspec = lambda shape: pl.BlockSpec((None,) + shape, lambda g, te, tr: (te[g], 0, 0))
    grid_spec = pltpu.PrefetchScalarGridSpec(
        num_scalar_prefetch=2,
        grid=(n_rows // tm,),
        in_specs=[pl.BlockSpec((tm, D_MODEL), lambda g, te, tr: (g, 0)),
                  wspec((D_MODEL, D_EXPERT)), wspec((D_MODEL, D_EXPERT)), wspec((D_EXPERT, D_MODEL))],
        out_specs=pl.BlockSpec((tm, D_MODEL), lambda g, te, tr: (g, 0)),
    )
    return pl.pallas_call(
        _gmm_kernel,
        grid_spec=grid_spec,
        out_shape=jax.ShapeDtypeStruct((n_rows, D_MODEL), F32),
        compiler_params=_params(1),
        name="moe_grouped_mlp",
    )(tile_expert, tile_rows, xs, wg, wu, wd)


def _combine_kernel(pos_hbm, ys_hbm, x1_ref, w_ref, g_ref, b_ref, o_ref, idx_ref, buf_ref, isem, rsem, *, td):
    i = pl.program_id(0)
    n = pl.num_programs(0)
    slot = lax.rem(i, 2)

    def issue_rows(sl):
        base = sl * (2 * td)

        def body(j, carry):
            for u in range(SUBLANES):
                for k in range(2):
                    p = idx_ref[base + k * td + j * SUBLANES + u]
                    pltpu.make_async_copy(_row_of(ys_hbm, p), buf_ref.at[sl, k, j, pl.ds(u, 1)],
                                          rsem.at[sl]).start(priority=k)
            return carry

        lax.fori_loop(0, td // SUBLANES, body, 0)

    @pl.when(i == 0)
    def _first():
        _index_copy(pos_hbm, idx_ref, isem, 0, 0).start()
        _index_copy(pos_hbm, idx_ref, isem, 0, 0).wait()
        issue_rows(0)

        @pl.when(n > 1)
        def _():
            _index_copy(pos_hbm, idx_ref, isem, 1, 1).start()

    @pl.when(i + 1 < n)
    def _next():
        _index_copy(pos_hbm, idx_ref, isem, i + 1, 1 - slot).wait()
        issue_rows(1 - slot)

        @pl.when(i + 2 < n)
        def _():
            _index_copy(pos_hbm, idx_ref, isem, i + 2, slot).start()

    for k in range(2):
        pltpu.make_async_copy(ys_hbm.at[pl.ds(0, td // SUBLANES)], buf_ref.at[slot, k], rsem.at[slot]).wait()
    w = w_ref[...]
    y = (w[:, 0:1] * buf_ref[slot, 0].reshape(td, D_MODEL)
         + w[:, 1:2] * buf_ref[slot, 1].reshape(td, D_MODEL))
    o_ref[...] = _layer_norm(ALPHA * x1_ref[...] + y, g_ref[...], b_ref[...])


def _combine(ys, pos_t, x1, wcol, g, b, td):
    n = x1.shape[0]
    row = lambda i: (i, 0)
    fix = lambda i: (0, 0)
    kern = functools.partial(_combine_kernel, td=td)
    return pl.pallas_call(
        kern,
        grid=(n // td,),
        in_specs=[pl.BlockSpec(memory_space=pl.ANY), pl.BlockSpec(memory_space=pl.ANY),
                  pl.BlockSpec((td, D_MODEL), row), pl.BlockSpec((td, LANES), row),
                  pl.BlockSpec((1, D_MODEL), fix), pl.BlockSpec((1, D_MODEL), fix)],
        out_specs=pl.BlockSpec((td, D_MODEL), row),
        out_shape=jax.ShapeDtypeStruct((n, D_MODEL), F32),
        scratch_shapes=[pltpu.SMEM((4 * td,), jnp.int32),
                        pltpu.VMEM((2, 2, td // SUBLANES, SUBLANES, D_MODEL), F32),
                        pltpu.SemaphoreType.DMA((2,)), pltpu.SemaphoreType.DMA((2,))],
        compiler_params=_params(1),
        name="moe_combine_ln2",
    )(pos_t, ys.reshape(-1, SUBLANES, D_MODEL), x1, wcol, g, b)


def _moe_routed(x1, info, wcol, cnt, p):
    n = x1.shape[0]
    tm, td = MOE_ROW_TILE, MOE_TOKEN_TILE
    n_tiles = 2 * n // tm + N_EXPERTS
    counts = cnt[:, 0].astype(jnp.int32)
    padded = (counts + (tm - 1)) // tm * tm
    ends = jnp.cumsum(padded)
    offs = ends - padded
    starts = jnp.arange(n_tiles, dtype=jnp.int32) * tm
    te = jnp.sum((ends[None, :] <= starts[:, None]).astype(jnp.int32), axis=1)
    tile_expert = jnp.minimum(te, N_EXPERTS - 1)
    experts = jnp.arange(N_EXPERTS, dtype=jnp.int32)
    pick = tile_expert[:, None] == experts[None, :]
    seg_end = jnp.sum(jnp.where(pick, (offs + counts)[None, :], 0), axis=1)
    tile_rows = jnp.where(te < N_EXPERTS, jnp.clip(seg_end - starts, 0, tm), 0).astype(jnp.int32)

    def position(e, r):
        return r + jnp.sum(jnp.where(e[None, :] == experts[:, None], offs[:, None], 0), axis=0)

    pos1 = position(info[0], info[2]).reshape(n // td, td)
    pos2 = position(info[1], info[3]).reshape(n // td, td)
    pos_t = jnp.concatenate([pos1, pos2], axis=1)
    xs = _dispatch(x1, pos_t, n_tiles * tm, td)
    ys = _grouped_mlp(xs, tile_expert, tile_rows, p["w_gate"], p["w_up"], p["w_down"], tm)
    return _combine(ys, pos_t, x1, wcol, p["ln2_g"], p["ln2_b"], td)


def _layer(x, mem, p):
    b, s, d = x.shape
    n = b * s
    xf = x.reshape(n, d)
    proj = _matmul(xf, p["w_in"], 512, "in_proj").reshape(b, s, IN_W)
    mkv = _matmul(mem.reshape(-1, d), p["w_mem_kv"], 512, "mem_kv").reshape(b, -1, 2 * M_W)
    oa = _window_attention(proj, p["sink"], p["g_a"])
    ob = _neighborhood_attention(proj, p["na_bias"], p["g_b"])
    om = _memory_attention(proj, mkv, p["g_m"])
    x1, info, wcol, cnt = _out_proj(oa.reshape(n, A_Q_W), ob.reshape(n, B_W), om.reshape(n, M_W), p["w_out"], xf,
                                    p["ln1_g"], p["ln1_b"], p["wr_cat"], p["rbias"])
    y = _moe_routed(x1, info, wcol, cnt, p)
    return y.reshape(b, s, d)


def kernel(x_prompt, x_sample, mem_prompt, mem_sample, w_in, w_mem_kv, sink_logits, rpb, grp_norm_g, w_out,
           ln1_g, ln1_b, w_router, router_bias, w_gate, w_up, w_down, ln2_g, ln2_b):
    seq = x_prompt.shape[1]
    assert x_sample.shape[1] == seq
    wr = jnp.pad(w_router.astype(F32), ((0, 0), (0, LANES - N_EXPERTS)))
    wr_hi = wr.astype(BF16)
    wr_cat = jnp.concatenate([wr_hi, (wr - wr_hi.astype(F32)).astype(BF16)], axis=1)
    q_scale = np.ones((IN_W,), np.float32)
    for c0, width in ((0, A_Q_W), (A_Q_W + 2 * LANES, B_W), (IN_W - M_W, M_W)):
        q_scale[c0:c0 + width] = SCALE
    rbias = router_bias.astype(F32).reshape(N_EXPERTS, 1)
    layers = []
    for l in range(DEPTH):
        g = grp_norm_g[l].astype(F32)
        layers.append(dict(
            w_in=(w_in[l] * q_scale).astype(BF16), w_mem_kv=w_mem_kv[l].astype(BF16),
            sink=sink_logits[l].astype(F32), na_bias=_na_bias(rpb[l], seq),
            g_a=g[:A_Q_W].reshape(1, -1), g_b=g[A_Q_W:A_Q_W + B_W].reshape(1, -1),
            g_m=g[A_Q_W + B_W:].reshape(1, -1),
            w_out=w_out[l].astype(BF16),
            ln1_g=ln1_g[l].astype(F32).reshape(1, -1), ln1_b=ln1_b[l].astype(F32).reshape(1, -1),
            wr_cat=wr_cat, rbias=rbias,
            w_gate=w_gate[l].astype(BF16), w_up=w_up[l].astype(BF16), w_down=w_down[l].astype(BF16),
            ln2_g=ln2_g[l].astype(F32).reshape(1, -1), ln2_b=ln2_b[l].astype(F32).reshape(1, -1)))

    def trunk(x, mem):
        for p in layers:
            x = _layer(x, mem, p)
        return x

    return (trunk(x_prompt, mem_prompt), trunk(x_sample, mem_sample))
```

```python
import functools

import jax
import jax.numpy as jnp
import numpy as np
from jax import lax
from jax.experimental import pallas as pl
from jax.experimental.pallas import tpu as pltpu

F32 = jnp.float32
BF16 = jnp.bfloat16

D_MODEL = 1024
DEPTH = 2
HEAD_DIM = 64
A_HEADS = 8
WINDOW = 128
B_HEADS = 4
GRID_W = 64
NA_ROWS = 8
NA_COLS = 16
M_HEADS = 4
N_EXPERTS = 16
N_GROUPS = 4
D_EXPERT = 512
LN_EPS = 1e-5
A_Q_W = 512
B_W = 256
M_W = 256
IN_W = 1792
ALPHA = (2 * DEPTH) ** 0.25
SCALE = HEAD_DIM ** -0.5
NEG = -1e30

LANES = 128
NA_TILE_ROWS = 2
NA_KEY_ROWS = 10
VMEM_LIMIT = 56 * 1024 * 1024


def _params(n_axes, vmem=VMEM_LIMIT):
    return pltpu.CompilerParams(dimension_semantics=("arbitrary",) * n_axes, vmem_limit_bytes=vmem)


def _dot_nt(a, b):
    return lax.dot_general(a, b, (((1,), (1,)), ((), ())), preferred_element_type=F32)


def _dot(a, b):
    return jnp.dot(a, b, preferred_element_type=F32)


def _lo_mask():
    return lax.broadcasted_iota(jnp.int32, (1, LANES), 1) < HEAD_DIM


def _mm_kernel(x_ref, w_ref, o_ref):
    o_ref[...] = _dot(x_ref[...].astype(BF16), w_ref[...]).astype(o_ref.dtype)


def _matmul(x, w, tm, name):
    n, k = x.shape
    m = w.shape[1]
    tm = min(tm, n)
    assert n % tm == 0
    return pl.pallas_call(
        _mm_kernel,
        grid=(n // tm,),
        in_specs=[pl.BlockSpec((tm, k), lambda i: (i, 0)), pl.BlockSpec((k, m), lambda i: (0, 0))],
        out_specs=pl.BlockSpec((tm, m), lambda i: (i, 0)),
        out_shape=jax.ShapeDtypeStruct((n, m), BF16),
        compiler_params=_params(1),
        name=name,
    )(x, w)


def _group_rms(of, g):
    ms = jnp.mean(of * of, axis=-1, keepdims=True)
    return of * lax.rsqrt(ms + LN_EPS) * g


def _win_kernel(sink_ref, q_ref, k_ref, v_ref, g_ref, o_ref, ks_ref, vs_ref, bias_ref, s_ref, p_ref, *,
                seq, tq):
    i = pl.program_id(1)
    lo = _lo_mask()
    chunk = 512

    @pl.when(i == 0)
    def _build():
        zeros = jnp.zeros((WINDOW, LANES), BF16)
        for t in range(4):
            for dst in (ks_ref, vs_ref):
                dst[t, 0:WINDOW, :] = zeros
                dst[t, seq + WINDOW:seq + 2 * WINDOW, :] = zeros

        def body(c, carry):
            r0 = pl.multiple_of(c * chunk, chunk)
            for src, dst in ((k_ref, ks_ref), (v_ref, vs_ref)):
                x = src[pl.ds(r0, chunk), :].astype(F32)
                xr = pltpu.roll(x, HEAD_DIM, 1)
                rows = pl.ds(r0 + WINDOW, chunk)
                dst[0, rows, :] = jnp.where(lo, x, 0.0).astype(BF16)
                dst[1, rows, :] = jnp.where(lo, 0.0, xr).astype(BF16)
                dst[2, rows, :] = jnp.where(lo, xr, 0.0).astype(BF16)
                dst[3, rows, :] = jnp.where(lo, 0.0, x).astype(BF16)
            return carry

        lax.fori_loop(0, seq // chunk, body, 0)

    kw = 3 * WINDOW

    @pl.when(jnp.logical_and(pl.program_id(0) == 0, i == 0))
    def _build_bias():
        a_i = lax.broadcasted_iota(jnp.int32, (WINDOW, kw), 0)
        j_i = lax.broadcasted_iota(jnp.int32, (WINDOW, kw), 1)
        dist = jnp.abs(a_i + WINDOW - j_i)
        absd = dist.astype(F32)
        for variant in range(3):
            valid = dist <= WINDOW
            if variant == 0:
                valid = valid & (j_i >= WINDOW)
            if variant == 2:
                valid = valid & (j_i < 2 * WINDOW)
            for head in range(A_HEADS):
                slope = 2.0 ** (-8.0 * (head + 1) / A_HEADS)
                bias_ref[variant * A_HEADS + head] = jnp.where(valid, -slope * absd, NEG)

    g = g_ref[...]

    def sub(j, carry):
        row0 = pl.multiple_of(j * WINDOW, WINDOW)
        qs = pl.multiple_of(i * tq + j * WINDOW, WINDOW)
        variant = jnp.where(qs == 0, 0, jnp.where(qs == seq - WINDOW, 2, 1))
        win = pl.ds(qs, kw)
        for head in range(A_HEADS):
            c, t = divmod(head, 2)
            q2 = q_ref[pl.ds(row0, WINDOW), c * LANES:(c + 1) * LANES]
            s_ref[head] = _dot_nt(q2, ks_ref[2 * (c // 2) + t, win, :]) + bias_ref[variant * A_HEADS + head]
        invs = []
        for head in range(A_HEADS):
            s = s_ref[head]
            sk = sink_ref[head]
            m = jnp.maximum(jnp.max(s, axis=-1, keepdims=True), sk)
            p = jnp.exp(s - m)
            den = jnp.sum(p, axis=-1, keepdims=True) + jnp.exp(sk - m)
            p_ref[head] = p.astype(BF16)
            invs.append(1.0 / den)
        outs = []
        for c in range(4):
            h = c // 2
            num = (_dot(p_ref[2 * c], vs_ref[2 * h, win, :])
                   + _dot(p_ref[2 * c + 1], vs_ref[2 * h + 1, win, :]))
            outs.append(num * jnp.where(lo, invs[2 * c], invs[2 * c + 1]))
        of = jnp.concatenate(outs, axis=1)
        o_ref[pl.ds(row0, WINDOW), :] = _group_rms(of, g).astype(o_ref.dtype)
        return carry

    lax.fori_loop(0, tq // WINDOW, sub, 0)


def _window_attention(proj, sink, g, tq=512):
    b, s, _ = proj.shape
    assert s % tq == 0 and s >= 2 * WINDOW
    kern = functools.partial(_win_kernel, seq=s, tq=tq)
    return pl.pallas_call(
        kern,
        grid=(b, s // tq),
        in_specs=[
            pl.BlockSpec(memory_space=pltpu.SMEM),
            pl.BlockSpec((None, tq, A_Q_W), lambda bi, i: (bi, i, 0)),
            pl.BlockSpec((None, s, LANES), lambda bi, i: (bi, 0, 4)),
            pl.BlockSpec((None, s, LANES), lambda bi, i: (bi, 0, 5)),
            pl.BlockSpec((1, A_Q_W), lambda bi, i: (0, 0)),
        ],
        out_specs=pl.BlockSpec((None, tq, A_Q_W), lambda bi, i: (bi, i, 0)),
        out_shape=jax.ShapeDtypeStruct((b, s, A_Q_W), BF16),
        scratch_shapes=[pltpu.VMEM((4, s + 2 * WINDOW, LANES), BF16),
                        pltpu.VMEM((4, s + 2 * WINDOW, LANES), BF16),
                        pltpu.VMEM((3 * A_HEADS, WINDOW, 3 * WINDOW), F32),
                        pltpu.VMEM((A_HEADS, WINDOW, 3 * WINDOW), F32),
                        pltpu.VMEM((A_HEADS, WINDOW, 3 * WINDOW), BF16)],
        compiler_params=_params(2),
        name="window_attention",
    )(sink, proj, proj, proj, g)


def _na_tables(seq):
    rows = seq // GRID_W
    kh = min(NA_ROWS, rows)
    assert rows % NA_TILE_ROWS == 0 and rows >= NA_KEY_ROWS and NA_KEY_ROWS % 2 == 0
    nt = rows // NA_TILE_ROWS
    u = np.arange(NA_TILE_ROWS * GRID_W) // GRID_W
    c = np.arange(NA_TILE_ROWS * GRID_W) % GRID_W
    ki = np.arange(NA_KEY_ROWS * GRID_W) // GRID_W
    kc = np.arange(NA_KEY_ROWS * GRID_W) % GRID_W
    cs = np.clip(c - NA_COLS // 2, 0, GRID_W - NA_COLS)
    col_ok = (kc[None, :] >= cs[:, None]) & (kc[None, :] < cs[:, None] + NA_COLS)
    dc = np.clip(kc[None, :] - c[:, None] + NA_COLS - 1, 0, 2 * NA_COLS - 2)
    types, type_of, kstart = [], [], []
    for t in range(nt):
        r0 = t * NA_TILE_ROWS
        k0 = int(np.clip(r0 - kh // 2, 0, rows - NA_KEY_ROWS))
        k0 -= k0 % 2
        r = r0 + u
        rs = np.clip(r - kh // 2, 0, rows - kh)
        kr = k0 + ki
        row_ok = (kr[None, :] >= rs[:, None]) & (kr[None, :] < rs[:, None] + kh)
        assert (rs >= k0).all() and (rs + kh <= k0 + NA_KEY_ROWS).all()
        dr = np.clip(kr[None, :] - r[:, None] + NA_ROWS - 1, 0, 2 * NA_ROWS - 2)
        key = (dr.tobytes(), row_ok.tobytes())
        for idx, (k_, _, _) in enumerate(types):
            if k_ == key:
                type_of.append(idx)
                break
        else:
            type_of.append(len(types))
            types.append((key, dr, row_ok & col_ok))
        kstart.append(k0)
    dr_rows = np.stack([t_[1][::GRID_W, ::GRID_W] for t_ in types])
    ok_all = np.stack([t_[2] for t_ in types])
    dc_cols = dc[:GRID_W, :GRID_W]
    return (np.asarray(type_of, np.int32), np.asarray(kstart, np.int32), dr_rows, dc_cols, ok_all)


def _na_bias(rpb, seq):
    _, _, dr_rows, dc_cols, ok = _na_tables(seq)
    nt = dr_rows.shape[0]
    hi = lax.Precision.HIGHEST
    col_sel = jnp.asarray(np.eye(2 * NA_COLS - 1, dtype=np.float32)[dc_cols])
    row_sel = jnp.asarray(np.eye(2 * NA_ROWS - 1, dtype=np.float32)[dr_rows])
    by_col = jnp.einsum("hdj,ckj->hdck", rpb.astype(F32), col_sel, precision=hi)
    bias = jnp.einsum("tuid,hdck->thucik", row_sel, by_col, precision=hi)
    bias = bias.reshape(nt, B_HEADS, NA_TILE_ROWS * GRID_W, NA_KEY_ROWS * GRID_W)
    return jnp.where(ok[:, None], bias, NEG)


def _na_kernel(type_ref, kstart_ref, q_ref, k_ref, v_ref, bias_ref, g_ref, o_ref, ks_ref, vs_ref, s_ref, p_ref,
               *, seq):
    del type_ref
    i = pl.program_id(1)
    lo = _lo_mask()
    chunk = 512

    @pl.when(i == 0)
    def _build():
        def body(c, carry):
            rows = pl.ds(pl.multiple_of(c * chunk, chunk), chunk)
            for src, dst in ((k_ref, ks_ref), (v_ref, vs_ref)):
                for p in range(2):
                    x = src[rows, p * LANES:(p + 1) * LANES]
                    dst[2 * p, rows, :] = jnp.where(lo, x, jnp.zeros_like(x))
                    dst[2 * p + 1, rows, :] = jnp.where(lo, jnp.zeros_like(x), x)
            return carry

        lax.fori_loop(0, seq // chunk, body, 0)

    nk = NA_KEY_ROWS * GRID_W
    win = pl.ds(pl.multiple_of(kstart_ref[i] * GRID_W, LANES), nk)
    for head in range(B_HEADS):
        q2 = q_ref[:, (head // 2) * LANES:(head // 2 + 1) * LANES]
        s_ref[head] = _dot_nt(q2, ks_ref[head, win, :]) + bias_ref[head]
    invs = []
    for head in range(B_HEADS):
        s = s_ref[head]
        m = jnp.max(s, axis=-1, keepdims=True)
        e = jnp.exp(s - m)
        invs.append(1.0 / jnp.sum(e, axis=-1, keepdims=True))
        p_ref[head] = e.astype(BF16)
    outs = []
    for p in range(2):
        num = _dot(p_ref[2 * p], vs_ref[2 * p, win, :]) + _dot(p_ref[2 * p + 1], vs_ref[2 * p + 1, win, :])
        outs.append(num * jnp.where(lo, invs[2 * p], invs[2 * p + 1]))
    of = jnp.concatenate(outs, axis=1)
    o_ref[...] = _group_rms(of, g_ref[...]).astype(o_ref.dtype)


def _neighborhood_attention(proj, bias, g):
    b, s, _ = proj.shape
    type_of, kstart, _, _, _ = _na_tables(s)
    tq = NA_TILE_ROWS * GRID_W
    nk = NA_KEY_ROWS * GRID_W
    kern = functools.partial(_na_kernel, seq=s)
    grid_spec = pltpu.PrefetchScalarGridSpec(
        num_scalar_prefetch=2,
        grid=(b, s // tq),
        in_specs=[
            pl.BlockSpec((None, tq, B_W), lambda bi, i, ty, ks: (bi, i, 3)),
            pl.BlockSpec((None, s, B_W), lambda bi, i, ty, ks: (bi, 0, 4)),
            pl.BlockSpec((None, s, B_W), lambda bi, i, ty, ks: (bi, 0, 5)),
            pl.BlockSpec((None, B_HEADS, tq, nk), lambda bi, i, ty, ks: (ty[i], 0, 0, 0)),
            pl.BlockSpec((1, B_W), lambda bi, i, ty, ks: (0, 0)),
        ],
        out_specs=pl.BlockSpec((None, tq, B_W), lambda bi, i, ty, ks: (bi, i, 0)),
        scratch_shapes=[pltpu.VMEM((4, s, LANES), BF16), pltpu.VMEM((4, s, LANES), BF16),
                        pltpu.VMEM((B_HEADS, tq, nk), F32), pltpu.VMEM((B_HEADS, tq, nk), BF16)],
    )
    return pl.pallas_call(
        kern,
        grid_spec=grid_spec,
        out_shape=jax.ShapeDtypeStruct((b, s, B_W), BF16),
        compiler_params=_params(2),
        name="neighborhood_attention",
    )(jnp.asarray(type_of), jnp.asarray(kstart), proj, proj, proj, bias, g)


def _mem_kernel(q_ref, mkv_ref, g_ref, o_ref):
    lo = _lo_mask()
    outs = []
    for p in range(2):
        q2 = q_ref[:, p * LANES:(p + 1) * LANES]
        mk = mkv_ref[:, p * LANES:(p + 1) * LANES]
        mv = mkv_ref[:, M_W + p * LANES:M_W + (p + 1) * LANES]
        zero = jnp.zeros_like(mk)
        num = None
        invs = []
        for t in range(2):
            keep = lo if t == 0 else jnp.logical_not(lo)
            s = _dot_nt(q2, jnp.where(keep, mk, zero))
            m = jnp.max(s, axis=-1, keepdims=True)
            e = jnp.exp(s - m)
            den = jnp.sum(e, axis=-1, keepdims=True)
            pv = _dot(e.astype(BF16), jnp.where(keep, mv, zero))
            num = pv if num is None else num + pv
            invs.append(1.0 / den)
        outs.append(num * jnp.where(lo, invs[0], invs[1]))
    of = jnp.concatenate(outs, axis=1)
    o_ref[...] = _group_rms(of, g_ref[...]).astype(o_ref.dtype)


def _memory_attention(proj, mkv, g, tq=512):
    b, s, _ = proj.shape
    mlen = mkv.shape[1]
    return pl.pallas_call(
        _mem_kernel,
        grid=(b, s // tq),
        in_specs=[
            pl.BlockSpec((None, tq, M_W), lambda bi, i: (bi, i, 6)),
            pl.BlockSpec((None, mlen, 2 * M_W), lambda bi, i: (bi, 0, 0)),
            pl.BlockSpec((1, M_W), lambda bi, i: (0, 0)),
        ],
        out_specs=pl.BlockSpec((None, tq, M_W), lambda bi, i: (bi, i, 0)),
        out_shape=jax.ShapeDtypeStruct((b, s, M_W), BF16),
        compiler_params=_params(2),
        name="memory_attention",
    )(proj, mkv, g)


def _layer_norm(z, g, b):
    mu = jnp.mean(z, axis=-1, keepdims=True)
    zc = z - mu
    var = jnp.mean(zc * zc, axis=-1, keepdims=True)
    return zc * lax.rsqrt(var + LN_EPS) * g + b


def _route(lg_t, rbias):
    e, t = lg_t.shape
    eg = e // N_GROUPS
    scores = 1.0 / (1.0 + jnp.exp(-lg_t))
    sel = scores + rbias
    row = lax.broadcasted_iota(jnp.int32, (e, t), 0)
    best = None
    gidx = None
    for gi in range(N_GROUPS):
        r = [sel[gi * eg + k:gi * eg + k + 1, :] for k in range(eg)]
        top2 = None
        for a in range(eg):
            for b in range(a + 1, eg):
                pair = r[a] + r[b]
                top2 = pair if top2 is None else jnp.maximum(top2, pair)
        if best is None:
            best, gidx = top2, jnp.zeros((1, t), jnp.int32)
        else:
            better = top2 > best
            gidx = jnp.where(better, gi, gidx)
            best = jnp.maximum(best, top2)
    in_group = (row // eg) == gidx
    masked = jnp.where(in_group, sel, -jnp.inf)
    m1 = jnp.max(masked, axis=0, keepdims=True)
    i1 = jnp.min(jnp.where(masked == m1, row, e), axis=0, keepdims=True)
    rest = jnp.where(row == i1, -jnp.inf, masked)
    m2 = jnp.max(rest, axis=0, keepdims=True)
    i2 = jnp.min(jnp.where(rest == m2, row, e), axis=0, keepdims=True)
    pick1 = row == i1
    pick2 = row == i2
    w1 = jnp.sum(jnp.where(pick1, scores, 0.0), axis=0, keepdims=True)
    w2 = jnp.sum(jnp.where(pick2, scores, 0.0), axis=0, keepdims=True)
    tot = w1 + w2
    return i1, i2, pick1, pick2, w1 / tot, w2 / tot


def _out_kernel(ma_ref, mb_ref, mm_ref, w_ref, x_ref, g_ref, b_ref, wr_ref, rb_ref, tri_ref,
                x1_ref, info_ref, wcol_ref, cnt_ref, carry_ref):
    @pl.when(pl.program_id(0) == 0)
    def _init():
        carry_ref[...] = jnp.zeros_like(carry_ref)

    y = _dot(ma_ref[...], w_ref[0:A_Q_W, :])
    y = y + _dot(mb_ref[...], w_ref[A_Q_W:A_Q_W + B_W, :])
    y = y + _dot(mm_ref[...], w_ref[A_Q_W + B_W:, :])
    x1 = _layer_norm(ALPHA * x_ref[...] + y, g_ref[...], b_ref[...])
    x1_ref[...] = x1
    x_hi = x1.astype(BF16)
    x_lo = (x1 - x_hi.astype(F32)).astype(BF16)
    hi2 = _dot(x_hi, wr_ref[...])
    lg = hi2[:, 0:LANES] + hi2[:, LANES:] + _dot(x_lo, wr_ref[:, 0:LANES])
    lg_t = lg.T[0:N_EXPERTS, :]
    i1, i2, pick1, pick2, w1, w2 = _route(lg_t, rb_ref[...])
    tm = lg_t.shape[1]
    member = jnp.where(jnp.logical_or(pick1, pick2), 1.0, 0.0)
    before = carry_ref[:, 0:1] + _dot(member.astype(BF16), tri_ref[...])
    r1 = jnp.sum(jnp.where(pick1, before, 0.0), axis=0, keepdims=True)
    r2 = jnp.sum(jnp.where(pick2, before, 0.0), axis=0, keepdims=True)
    total = carry_ref[...] + jnp.sum(member, axis=1, keepdims=True)
    carry_ref[...] = total
    cnt_ref[...] = total
    info_ref[...] = jnp.concatenate(
        [i1, i2, r1.astype(jnp.int32), r2.astype(jnp.int32), jnp.zeros((4, tm), jnp.int32)], axis=0)
    w_pad = jnp.concatenate([w1, w2, jnp.zeros((LANES - 2, tm), F32)], axis=0)
    wcol_ref[...] = w_pad.T


def _out_proj(ma, mb, mm, w_out, x, g, b, wr_cat, rbias, tm=512):
    n = x.shape[0]
    row = lambda i: (i, 0)
    fix = lambda i: (0, 0)
    tri = jnp.asarray(np.triu(np.ones((tm, tm), np.float32), k=1), BF16)
    return pl.pallas_call(
        _out_kernel,
        grid=(n // tm,),
        in_specs=[
            pl.BlockSpec((tm, A_Q_W), row), pl.BlockSpec((tm, B_W), row), pl.BlockSpec((tm, M_W), row),
            pl.BlockSpec((D_MODEL, D_MODEL), fix), pl.BlockSpec((tm, D_MODEL), row),
            pl.BlockSpec((1, D_MODEL), fix), pl.BlockSpec((1, D_MODEL), fix),
            pl.BlockSpec((D_MODEL, 2 * LANES), fix), pl.BlockSpec((N_EXPERTS, 1), fix),
            pl.BlockSpec((tm, tm), fix),
        ],
        out_specs=[pl.BlockSpec((tm, D_MODEL), row), pl.BlockSpec((8, tm), lambda i: (0, i)),
                   pl.BlockSpec((tm, LANES), row), pl.BlockSpec((N_EXPERTS, LANES), fix)],
        out_shape=[jax.ShapeDtypeStruct((n, D_MODEL), F32), jax.ShapeDtypeStruct((8, n), jnp.int32),
                   jax.ShapeDtypeStruct((n, LANES), F32), jax.ShapeDtypeStruct((N_EXPERTS, LANES), F32)],
        scratch_shapes=[pltpu.VMEM((N_EXPERTS, LANES), F32)],
        compiler_params=_params(1),
        name="out_proj_ln_router",
    )(ma, mb, mm, w_out, x, g, b, wr_cat, rbias, tri)


MOE_ROW_TILE = 512
MOE_TOKEN_TILE = 512


SUBLANES = 8


def _index_copy(pos_hbm, idx_ref, isem, step, slot):
    width = pos_hbm.shape[1]
    dst = idx_ref.at[pl.ds(pl.multiple_of(slot * width, width), width)]
    return pltpu.make_async_copy(pos_hbm.at[step], dst, isem.at[slot])


assert D_MODEL == SUBLANES * LANES
ZERO_CHUNK = 64


def _token_tile(ref, p):
    return ref.at[pl.ds(pl.multiple_of(p * SUBLANES, SUBLANES), SUBLANES)]


def _store_token_tiles(dst_ref, x):
    t = x.shape[0]
    for c in range(SUBLANES):
        dst_ref[pl.ds(c, t, stride=SUBLANES), :] = x[:, c * LANES:(c + 1) * LANES]


def _load_token_tiles(src_ref, t):
    return jnp.concatenate([src_ref[pl.ds(c, t, stride=SUBLANES), :] for c in range(SUBLANES)], axis=1)


def _dispatch_kernel(lo_ref, hi_ref, pos_hbm, x_ref, xs_hbm, idx_ref, xt_ref, z_ref, isem, rsem, zsem, *, td):
    i = pl.program_id(0)
    n = pl.num_programs(0)
    slot = lax.rem(i, 2)

    @pl.when(i == 0)
    def _first():
        _index_copy(pos_hbm, idx_ref, isem, 0, 0).start()
        z_ref[...] = jnp.zeros_like(z_ref)
        one = z_ref.at[pl.ds(0, SUBLANES)]

        def fill(e, carry):
            a = lo_ref[e]
            b = hi_ref[e]
            n_single = jnp.minimum(b - a, jnp.bitwise_and(-a, ZERO_CHUNK - 1))
            a2 = a + n_single
            n_chunk = lax.shift_right_logical(b - a2, 6)

            def single(r, c):
                pltpu.make_async_copy(one, _token_tile(xs_hbm, a + r), zsem).start()
                return c

            def chunk(r, c):
                dst = xs_hbm.at[pl.ds(pl.multiple_of((a2 + r * ZERO_CHUNK) * SUBLANES, SUBLANES),
                                      ZERO_CHUNK * SUBLANES)]
                pltpu.make_async_copy(z_ref, dst, zsem).start()
                return c

            def single_wait(r, c):
                pltpu.make_async_copy(one, _token_tile(xs_hbm, 0), zsem).wait()
                return c

            def chunk_wait(r, c):
                pltpu.make_async_copy(z_ref, xs_hbm.at[pl.ds(0, ZERO_CHUNK * SUBLANES)], zsem).wait()
                return c

            lax.fori_loop(0, n_single, single, 0)
            lax.fori_loop(0, n_chunk, chunk, 0)
            lax.fori_loop(0, n_single, single_wait, 0)
            lax.fori_loop(0, n_chunk, chunk_wait, 0)
            return carry

        lax.fori_loop(0, N_EXPERTS, fill, 0)

    _index_copy(pos_hbm, idx_ref, isem, i, slot).wait()

    @pl.when(i + 1 < n)
    def _next():
        _index_copy(pos_hbm, idx_ref, isem, i + 1, 1 - slot).start()

    _store_token_tiles(xt_ref, x_ref[...])
    base = slot * (2 * td)

    def body(j, carry):
        for u in range(SUBLANES):
            t = j * SUBLANES + u
            src = _token_tile(xt_ref, t)
            for k in range(2):
                p = idx_ref[base + k * td + t]
                pltpu.make_async_copy(src, _token_tile(xs_hbm, p), rsem).start(priority=k)
        return carry

    lax.fori_loop(0, td // SUBLANES, body, 0)
    for _ in range(2):
        pltpu.make_async_copy(xt_ref, xs_hbm.at[pl.ds(0, td * SUBLANES)], rsem).wait()


def _dispatch(x1, pos_t, pad_lo, pad_hi, n_rows, td):
    n = x1.shape[0]
    kern = functools.partial(_dispatch_kernel, td=td)
    grid_spec = pltpu.PrefetchScalarGridSpec(
        num_scalar_prefetch=2,
        grid=(n // td,),
        in_specs=[pl.BlockSpec(memory_space=pl.ANY),
                  pl.BlockSpec((td, D_MODEL), lambda i, lo, hi: (i, 0))],
        out_specs=pl.BlockSpec(memory_space=pl.ANY),
        scratch_shapes=[pltpu.SMEM((4 * td,), jnp.int32),
                        pltpu.VMEM((td * SUBLANES, LANES), F32),
                        pltpu.VMEM((ZERO_CHUNK * SUBLANES, LANES), F32),
                        pltpu.SemaphoreType.DMA((2,)), pltpu.SemaphoreType.DMA(()),
                        pltpu.SemaphoreType.DMA(())],
    )
    return pl.pallas_call(
        kern,
        grid_spec=grid_spec,
        out_shape=jax.ShapeDtypeStruct((n_rows * SUBLANES, LANES), F32),
        compiler_params=_params(1),
        name="moe_dispatch",
    )(pad_lo, pad_hi, pos_t, x1)


def _gmm_kernel(te_ref, rows_ref, xs_ref, wg_ref, wu_ref, wd_ref, ys_ref):
    del te_ref
    nrows = rows_ref[pl.program_id(0)]

    @pl.when(nrows > 0)
    def _compute():
        x = _load_token_tiles(xs_ref, xs_ref.shape[0] // SUBLANES).astype(BF16)
        gate = _dot(x, wg_ref[...])
        up = _dot(x, wu_ref[...])
        h = (gate * (1.0 / (1.0 + jnp.exp(-gate))) * up).astype(BF16)
        _store_token_tiles(ys_ref, _dot(h, wd_ref[...]))

    @pl.when(nrows == 0)
    def _empty():
        ys_ref[...] = jnp.zeros_like(ys_ref)


def _grouped_mlp(xs, tile_expert, tile_rows, wg, wu, wd, tm):
    n_rows = xs.shape[0] // SUBLANES
    wspec = lambda shape: pl.BlockSpec((None,) + shape, lambda g, te, tr: (te[g], 0, 0))
    grid_spec = pltpu.PrefetchScalarGridSpec(
        num_scalar_prefetch=2,
        grid=(n_rows // tm,),
        in_specs=[pl.BlockSpec((tm * SUBLANES, LANES), lambda g, te, tr: (g, 0)),
                  wspec((D_MODEL, D_EXPERT)), wspec((D_MODEL, D_EXPERT)), wspec((D_EXPERT, D_MODEL))],
        out_specs=pl.BlockSpec((tm * SUBLANES, LANES), lambda g, te, tr: (g, 0)),
    )
    return pl.pallas_call(
        _gmm_kernel,
        grid_spec=grid_spec,
        out_shape=jax.ShapeDtypeStruct(xs.shape, F32),
        compiler_params=_params(1),
        name="moe_grouped_mlp",
    )(tile_expert, tile_rows, xs, wg, wu, wd)


def _combine_kernel(pos_hbm, ys_hbm, x1_ref, w_ref, g_ref, b_ref, o_ref, idx_ref, buf_ref, isem, rsem, *, td):
    i = pl.program_id(0)
    n = pl.num_programs(0)
    slot = lax.rem(i, 2)

    def issue_rows(sl):
        base = sl * (2 * td)

        def body(j, carry):
            for u in range(SUBLANES):
                t = j * SUBLANES + u
                for k in range(2):
                    p = idx_ref[base + k * td + t]
                    pltpu.make_async_copy(_token_tile(ys_hbm, p), _token_tile(buf_ref.at[sl, k], t),
                                          rsem.at[sl]).start(priority=k)
            return carry

        lax.fori_loop(0, td // SUBLANES, body, 0)

    @pl.when(i == 0)
    def _first():
        _index_copy(pos_hbm, idx_ref, isem, 0, 0).start()
        _index_copy(pos_hbm, idx_ref, isem, 0, 0).wait()
        issue_rows(0)

        @pl.when(n > 1)
        def _():
            _index_copy(pos_hbm, idx_ref, isem, 1, 1).start()

    @pl.when(i + 1 < n)
    def _next():
        _index_copy(pos_hbm, idx_ref, isem, i + 1, 1 - slot).wait()
        issue_rows(1 - slot)

        @pl.when(i + 2 < n)
        def _():
            _index_copy(pos_hbm, idx_ref, isem, i + 2, slot).start()

    for k in range(2):
        pltpu.make_async_copy(ys_hbm.at[pl.ds(0, td * SUBLANES)], buf_ref.at[slot, k], rsem.at[slot]).wait()
    w = w_ref[...]
    y = (w[:, 0:1] * _load_token_tiles(buf_ref.at[slot, 0], td)
         + w[:, 1:2] * _load_token_tiles(buf_ref.at[slot, 1], td))
    o_ref[...] = _layer_norm(ALPHA * x1_ref[...] + y, g_ref[...], b_ref[...])


def _combine(ys, pos_t, x1, wcol, g, b, td):
    n = x1.shape[0]
    row = lambda i: (i, 0)
    fix = lambda i: (0, 0)
    kern = functools.partial(_combine_kernel, td=td)
    return pl.pallas_call(
        kern,
        grid=(n // td,),
        in_specs=[pl.BlockSpec(memory_space=pl.ANY), pl.BlockSpec(memory_space=pl.ANY),
                  pl.BlockSpec((td, D_MODEL), row), pl.BlockSpec((td, LANES), row),
                  pl.BlockSpec((1, D_MODEL), fix), pl.BlockSpec((1, D_MODEL), fix)],
        out_specs=pl.BlockSpec((td, D_MODEL), row),
        out_shape=jax.ShapeDtypeStruct((n, D_MODEL), F32),
        scratch_shapes=[pltpu.SMEM((4 * td,), jnp.int32),
                        pltpu.VMEM((2, 2, td * SUBLANES, LANES), F32),
                        pltpu.SemaphoreType.DMA((2,)), pltpu.SemaphoreType.DMA((2,))],
        compiler_params=_params(1),
        name="moe_combine_ln2",
    )(pos_t, ys, x1, wcol, g, b)


def _moe_routed(x1, info, wcol, cnt, p):
    n = x1.shape[0]
    tm, td = MOE_ROW_TILE, MOE_TOKEN_TILE
    n_tiles = 2 * n // tm + N_EXPERTS
    counts = cnt[:, 0].astype(jnp.int32)
    padded = (counts + (tm - 1)) // tm * tm
    ends = jnp.cumsum(padded)
    offs = ends - padded
    starts = jnp.arange(n_tiles, dtype=jnp.int32) * tm
    te = jnp.sum((ends[None, :] <= starts[:, None]).astype(jnp.int32), axis=1)
    tile_expert = jnp.minimum(te, N_EXPERTS - 1)
    experts = jnp.arange(N_EXPERTS, dtype=jnp.int32)
    pick = tile_expert[:, None] == experts[None, :]
    seg_end = jnp.sum(jnp.where(pick, (offs + counts)[None, :], 0), axis=1)
    tile_rows = jnp.where(te < N_EXPERTS, jnp.clip(seg_end - starts, 0, tm), 0).astype(jnp.int32)

    def position(e, r):
        return r + jnp.sum(jnp.where(e[None, :] == experts[:, None], offs[:, None], 0), axis=0)

    pos1 = position(info[0], info[2]).reshape(n // td, td)
    pos2 = position(info[1], info[3]).reshape(n // td, td)
    pos_t = jnp.concatenate([pos1, pos2], axis=1)
    pad_lo = offs + counts
    pad_hi = ends.at[N_EXPERTS - 1].set(n_tiles * tm)
    xs = _dispatch(x1, pos_t, pad_lo, pad_hi, n_tiles * tm, td)
    ys = _grouped_mlp(xs, tile_expert, tile_rows, p["w_gate"], p["w_up"], p["w_down"], tm)
    return _combine(ys, pos_t, x1, wcol, p["ln2_g"], p["ln2_b"], td)


def _layer(x, mem, p):
    b, s, d = x.shape
    n = b * s
    xf = x.reshape(n, d)
    proj = _matmul(xf, p["w_in"], 512, "in_proj").reshape(b, s, IN_W)
    mkv = _matmul(mem.reshape(-1, d), p["w_mem_kv"], 512, "mem_kv").reshape(b, -1, 2 * M_W)
    oa = _window_attention(proj, p["sink"], p["g_a"])
    ob = _neighborhood_attention(proj, p["na_bias"], p["g_b"])
    om = _memory_attention(proj, mkv, p["g_m"])
    x1, info, wcol, cnt = _out_proj(oa.reshape(n, A_Q_W), ob.reshape(n, B_W), om.reshape(n, M_W), p["w_out"], xf,
                                    p["ln1_g"], p["ln1_b"], p["wr_cat"], p["rbias"])
    y = _moe_routed(x1, info, wcol, cnt, p)
    return y.reshape(b, s, d)


def kernel(x_prompt, x_sample, mem_prompt, mem_sample, w_in, w_mem_kv, sink_logits, rpb, grp_norm_g, w_out,
           ln1_g, ln1_b, w_router, router_bias, w_gate, w_up, w_down, ln2_g, ln2_b):
    seq = x_prompt.shape[1]
    assert x_sample.shape[1] == seq
    wr = jnp.pad(w_router.astype(F32), ((0, 0), (0, LANES - N_EXPERTS)))
    wr_hi = wr.astype(BF16)
    wr_cat = jnp.concatenate([wr_hi, (wr - wr_hi.astype(F32)).astype(BF16)], axis=1)
    q_scale = np.ones((IN_W,), np.float32)
    for c0, width in ((0, A_Q_W), (A_Q_W + 2 * LANES, B_W), (IN_W - M_W, M_W)):
        q_scale[c0:c0 + width] = SCALE
    rbias = router_bias.astype(F32).reshape(N_EXPERTS, 1)
    layers = []
    for l in range(DEPTH):
        g = grp_norm_g[l].astype(F32)
        layers.append(dict(
            w_in=(w_in[l] * q_scale).astype(BF16), w_mem_kv=w_mem_kv[l].astype(BF16),
            sink=sink_logits[l].astype(F32), na_bias=_na_bias(rpb[l], seq),
            g_a=g[:A_Q_W].reshape(1, -1), g_b=g[A_Q_W:A_Q_W + B_W].reshape(1, -1),
            g_m=g[A_Q_W + B_W:].reshape(1, -1),
            w_out=w_out[l].astype(BF16),
            ln1_g=ln1_g[l].astype(F32).reshape(1, -1), ln1_b=ln1_b[l].astype(F32).reshape(1, -1),
            wr_cat=wr_cat, rbias=rbias,
            w_gate=w_gate[l].astype(BF16), w_up=w_up[l].astype(BF16), w_down=w_down[l].astype(BF16),
            ln2_g=ln2_g[l].astype(F32).reshape(1, -1), ln2_b=ln2_b[l].astype(F32).reshape(1, -1)))

    def trunk(x, mem):
        for p in layers:
            x = _layer(x, mem, p)
        return x

    return (trunk(x_prompt, mem_prompt), trunk(x_sample, mem_sample))
```

```python
import functools

import jax
import jax.numpy as jnp
import numpy as np
from jax import lax
from jax.experimental import pallas as pl
from jax.experimental.pallas import tpu as pltpu

F32 = jnp.float32
BF16 = jnp.bfloat16

D_MODEL = 1024
DEPTH = 2
HEAD_DIM = 64
A_HEADS = 8
WINDOW = 128
B_HEADS = 4
GRID_W = 64
NA_ROWS = 8
NA_COLS = 16
M_HEADS = 4
N_EXPERTS = 16
N_GROUPS = 4
D_EXPERT = 512
LN_EPS = 1e-5
A_Q_W = 512
B_W = 256
M_W = 256
IN_W = 1792
ALPHA = (2 * DEPTH) ** 0.25
SCALE = HEAD_DIM ** -0.5
NEG = -1e30

LANES = 128
SOFTMAX_HEADS = 2
NA_TILE_ROWS = 2
NA_KEY_ROWS = 10
VMEM_LIMIT = 56 * 1024 * 1024


def _params(n_axes, vmem=VMEM_LIMIT):
    return pltpu.CompilerParams(dimension_semantics=("arbitrary",) * n_axes, vmem_limit_bytes=vmem)


def _dot_nt(a, b):
    return lax.dot_general(a, b, (((1,), (1,)), ((), ())), preferred_element_type=F32)


def _dot(a, b):
    return jnp.dot(a, b, preferred_element_type=F32)


def _lo_mask():
    return lax.broadcasted_iota(jnp.int32, (1, LANES), 1) < HEAD_DIM


def _mm_kernel(x_ref, w_ref, o_ref):
    o_ref[...] = _dot(x_ref[...].astype(BF16), w_ref[...]).astype(o_ref.dtype)


def _matmul(x, w, tm, name):
    n, k = x.shape
    m = w.shape[1]
    tm = min(tm, n)
    assert n % tm == 0
    return pl.pallas_call(
        _mm_kernel,
        grid=(n // tm,),
        in_specs=[pl.BlockSpec((tm, k), lambda i: (i, 0)), pl.BlockSpec((k, m), lambda i: (0, 0))],
        out_specs=pl.BlockSpec((tm, m), lambda i: (i, 0)),
        out_shape=jax.ShapeDtypeStruct((n, m), BF16),
        compiler_params=_params(1),
        name=name,
    )(x, w)


def _group_rms(of, g):
    ms = jnp.mean(of * of, axis=-1, keepdims=True)
    return of * lax.rsqrt(ms + LN_EPS) * g


def _win_kernel(sink_ref, q_ref, k_ref, v_ref, g_ref, o_ref, ks_ref, vs_ref, bias_ref, s_ref, p_ref, *,
                seq, tq):
    i = pl.program_id(1)
    lo = _lo_mask()
    chunk = 512

    @pl.when(i == 0)
    def _build():
        zeros = jnp.zeros((WINDOW, LANES), BF16)
        for t in range(4):
            for dst in (ks_ref, vs_ref):
                dst[t, 0:WINDOW, :] = zeros
                dst[t, seq + WINDOW:seq + 2 * WINDOW, :] = zeros

        def body(c, carry):
            r0 = pl.multiple_of(c * chunk, chunk)
            for src, dst in ((k_ref, ks_ref), (v_ref, vs_ref)):
                x = src[pl.ds(r0, chunk), :].astype(F32)
                xr = pltpu.roll(x, HEAD_DIM, 1)
                rows = pl.ds(r0 + WINDOW, chunk)
                dst[0, rows, :] = jnp.where(lo, x, 0.0).astype(BF16)
                dst[1, rows, :] = jnp.where(lo, 0.0, xr).astype(BF16)
                dst[2, rows, :] = jnp.where(lo, xr, 0.0).astype(BF16)
                dst[3, rows, :] = jnp.where(lo, 0.0, x).astype(BF16)
            return carry

        lax.fori_loop(0, seq // chunk, body, 0)

    kw = 3 * WINDOW

    @pl.when(jnp.logical_and(pl.program_id(0) == 0, i == 0))
    def _build_bias():
        a_i = lax.broadcasted_iota(jnp.int32, (WINDOW, kw), 0)
        j_i = lax.broadcasted_iota(jnp.int32, (WINDOW, kw), 1)
        dist = jnp.abs(a_i + WINDOW - j_i)
        absd = dist.astype(F32)
        for variant in range(3):
            valid = dist <= WINDOW
            if variant == 0:
                valid = valid & (j_i >= WINDOW)
            if variant == 2:
                valid = valid & (j_i < 2 * WINDOW)
            for head in range(A_HEADS):
                slope = 2.0 ** (-8.0 * (head + 1) / A_HEADS)
                bias_ref[variant * A_HEADS + head] = jnp.where(valid, -slope * absd, NEG)

    g = g_ref[...]

    def sub(j, carry):
        row0 = pl.multiple_of(j * WINDOW, WINDOW)
        qs = pl.multiple_of(i * tq + j * WINDOW, WINDOW)
        variant = jnp.where(qs == 0, 0, jnp.where(qs == seq - WINDOW, 2, 1))
        win = pl.ds(qs, kw)
        for head in range(A_HEADS):
            c, t = divmod(head, 2)
            q2 = q_ref[pl.ds(row0, WINDOW), c * LANES:(c + 1) * LANES]
            s_ref[head] = _dot_nt(q2, ks_ref[2 * (c // 2) + t, win, :]) + bias_ref[variant * A_HEADS + head]
        hs = SOFTMAX_HEADS
        invs = []
        for h0 in range(0, A_HEADS, hs):
            s = s_ref[h0:h0 + hs]
            hid = lax.broadcasted_iota(jnp.int32, (hs, 1, 1), 0)
            sk = jnp.zeros((hs, 1, 1), F32)
            for u in range(hs):
                sk = jnp.where(hid == u, sink_ref[h0 + u], sk)
            m = jnp.maximum(jnp.max(s, axis=-1, keepdims=True), sk)
            p = jnp.exp(s - m)
            den = jnp.sum(p, axis=-1, keepdims=True) + jnp.exp(sk - m)
            p_ref[h0:h0 + hs] = p.astype(BF16)
            inv = 1.0 / den
            invs.extend(inv[u] for u in range(hs))
        outs = []
        for c in range(4):
            h = c // 2
            num = (_dot(p_ref[2 * c], vs_ref[2 * h, win, :])
                   + _dot(p_ref[2 * c + 1], vs_ref[2 * h + 1, win, :]))
            outs.append(num * jnp.where(lo, invs[2 * c], invs[2 * c + 1]))
        of = jnp.concatenate(outs, axis=1)
        o_ref[pl.ds(row0, WINDOW), :] = _group_rms(of, g).astype(o_ref.dtype)
        return carry

    lax.fori_loop(0, tq // WINDOW, sub, 0, unroll=2)


def _window_attention(proj, sink, g, tq=512):
    b, s, _ = proj.shape
    assert s % tq == 0 and s >= 2 * WINDOW
    kern = functools.partial(_win_kernel, seq=s, tq=tq)
    return pl.pallas_call(
        kern,
        grid=(b, s // tq),
        in_specs=[
            pl.BlockSpec(memory_space=pltpu.SMEM),
            pl.BlockSpec((None, tq, A_Q_W), lambda bi, i: (bi, i, 0)),
            pl.BlockSpec((None, s, LANES), lambda bi, i: (bi, 0, 4)),
            pl.BlockSpec((None, s, LANES), lambda bi, i: (bi, 0, 5)),
            pl.BlockSpec((1, A_Q_W), lambda bi, i: (0, 0)),
        ],
        out_specs=pl.BlockSpec((None, tq, A_Q_W), lambda bi, i: (bi, i, 0)),
        out_shape=jax.ShapeDtypeStruct((b, s, A_Q_W), BF16),
        scratch_shapes=[pltpu.VMEM((4, s + 2 * WINDOW, LANES), BF16),
                        pltpu.VMEM((4, s + 2 * WINDOW, LANES), BF16),
                        pltpu.VMEM((3 * A_HEADS, WINDOW, 3 * WINDOW), F32),
                        pltpu.VMEM((A_HEADS, WINDOW, 3 * WINDOW), F32),
                        pltpu.VMEM((A_HEADS, WINDOW, 3 * WINDOW), BF16)],
        compiler_params=_params(2),
        name="window_attention",
    )(sink, proj, proj, proj, g)


def _na_tables(seq):
    rows = seq // GRID_W
    kh = min(NA_ROWS, rows)
    assert rows % NA_TILE_ROWS == 0 and rows >= NA_KEY_ROWS and NA_KEY_ROWS % 2 == 0
    nt = rows // NA_TILE_ROWS
    u = np.arange(NA_TILE_ROWS * GRID_W) // GRID_W
    c = np.arange(NA_TILE_ROWS * GRID_W) % GRID_W
    ki = np.arange(NA_KEY_ROWS * GRID_W) // GRID_W
    kc = np.arange(NA_KEY_ROWS * GRID_W) % GRID_W
    cs = np.clip(c - NA_COLS // 2, 0, GRID_W - NA_COLS)
    col_ok = (kc[None, :] >= cs[:, None]) & (kc[None, :] < cs[:, None] + NA_COLS)
    dc = np.clip(kc[None, :] - c[:, None] + NA_COLS - 1, 0, 2 * NA_COLS - 2)
    types, type_of, kstart = [], [], []
    for t in range(nt):
        r0 = t * NA_TILE_ROWS
        k0 = int(np.clip(r0 - kh // 2, 0, rows - NA_KEY_ROWS))
        k0 -= k0 % 2
        r = r0 + u
        rs = np.clip(r - kh // 2, 0, rows - kh)
        kr = k0 + ki
        row_ok = (kr[None, :] >= rs[:, None]) & (kr[None, :] < rs[:, None] + kh)
        assert (rs >= k0).all() and (rs + kh <= k0 + NA_KEY_ROWS).all()
        dr = np.clip(kr[None, :] - r[:, None] + NA_ROWS - 1, 0, 2 * NA_ROWS - 2)
        key = (dr.tobytes(), row_ok.tobytes())
        for idx, (k_, _, _) in enumerate(types):
            if k_ == key:
                type_of.append(idx)
                break
        else:
            type_of.append(len(types))
            types.append((key, dr, row_ok & col_ok))
        kstart.append(k0)
    dr_rows = np.stack([t_[1][::GRID_W, ::GRID_W] for t_ in types])
    ok_all = np.stack([t_[2] for t_ in types])
    dc_cols = dc[:GRID_W, :GRID_W]
    return (np.asarray(type_of, np.int32), np.asarray(kstart, np.int32), dr_rows, dc_cols, ok_all)


def _na_bias(rpb, seq):
    _, _, dr_rows, dc_cols, ok = _na_tables(seq)
    nt = dr_rows.shape[0]
    hi = lax.Precision.HIGHEST
    col_sel = jnp.asarray(np.eye(2 * NA_COLS - 1, dtype=np.float32)[dc_cols])
    row_sel = jnp.asarray(np.eye(2 * NA_ROWS - 1, dtype=np.float32)[dr_rows])
    by_col = jnp.einsum("hdj,ckj->hdck", rpb.astype(F32), col_sel, precision=hi)
    bias = jnp.einsum("tuid,hdck->thucik", row_sel, by_col, precision=hi)
    bias = bias.reshape(nt, B_HEADS, NA_TILE_ROWS * GRID_W, NA_KEY_ROWS * GRID_W)
    return jnp.where(ok[:, None], bias, NEG)


def _na_kernel(type_ref, kstart_ref, q_ref, k_ref, v_ref, bias_ref, g_ref, o_ref, ks_ref, vs_ref, s_ref, p_ref,
               *, seq):
    del type_ref
    i = pl.program_id(1)
    lo = _lo_mask()
    chunk = 512

    @pl.when(i == 0)
    def _build():
        def body(c, carry):
            rows = pl.ds(pl.multiple_of(c * chunk, chunk), chunk)
            for src, dst in ((k_ref, ks_ref), (v_ref, vs_ref)):
                for p in range(2):
                    x = src[rows, p * LANES:(p + 1) * LANES]
                    dst[2 * p, rows, :] = jnp.where(lo, x, jnp.zeros_like(x))
                    dst[2 * p + 1, rows, :] = jnp.where(lo, jnp.zeros_like(x), x)
            return carry

        lax.fori_loop(0, seq // chunk, body, 0)

    nk = NA_KEY_ROWS * GRID_W
    win = pl.ds(pl.multiple_of(kstart_ref[i] * GRID_W, LANES), nk)
    for head in range(B_HEADS):
        q2 = q_ref[:, (head // 2) * LANES:(head // 2 + 1) * LANES]
        s_ref[head] = _dot_nt(q2, ks_ref[head, win, :]) + bias_ref[head]
    invs = []
    for head in range(B_HEADS):
        s = s_ref[head]
        m = jnp.max(s, axis=-1, keepdims=True)
        e = jnp.exp(s - m)
        invs.append(1.0 / jnp.sum(e, axis=-1, keepdims=True))
        p_ref[head] = e.astype(BF16)
    outs = []
    for p in range(2):
        num = _dot(p_ref[2 * p], vs_ref[2 * p, win, :]) + _dot(p_ref[2 * p + 1], vs_ref[2 * p + 1, win, :])
        outs.append(num * jnp.where(lo, invs[2 * p], invs[2 * p + 1]))
    of = jnp.concatenate(outs, axis=1)
    o_ref[...] = _group_rms(of, g_ref[...]).astype(o_ref.dtype)


def _neighborhood_attention(proj, bias, g):
    b, s, _ = proj.shape
    type_of, kstart, _, _, _ = _na_tables(s)
    tq = NA_TILE_ROWS * GRID_W
    nk = NA_KEY_ROWS * GRID_W
    kern = functools.partial(_na_kernel, seq=s)
    grid_spec = pltpu.PrefetchScalarGridSpec(
        num_scalar_prefetch=2,
        grid=(b, s // tq),
        in_specs=[
            pl.BlockSpec((None, tq, B_W), lambda bi, i, ty, ks: (bi, i, 3)),
            pl.BlockSpec((None, s, B_W), lambda bi, i, ty, ks: (bi, 0, 4)),
            pl.BlockSpec((None, s, B_W), lambda bi, i, ty, ks: (bi, 0, 5)),
            pl.BlockSpec((None, B_HEADS, tq, nk), lambda bi, i, ty, ks: (ty[i], 0, 0, 0)),
            pl.BlockSpec((1, B_W), lambda bi, i, ty, ks: (0, 0)),
        ],
        out_specs=pl.BlockSpec((None, tq, B_W), lambda bi, i, ty, ks: (bi, i, 0)),
        scratch_shapes=[pltpu.VMEM((4, s, LANES), BF16), pltpu.VMEM((4, s, LANES), BF16),
                        pltpu.VMEM((B_HEADS, tq, nk), F32), pltpu.VMEM((B_HEADS, tq, nk), BF16)],
    )
    return pl.pallas_call(
        kern,
        grid_spec=grid_spec,
        out_shape=jax.ShapeDtypeStruct((b, s, B_W), BF16),
        compiler_params=_params(2),
        name="neighborhood_attention",
    )(jnp.asarray(type_of), jnp.asarray(kstart), proj, proj, proj, bias, g)


def _mem_kernel(q_ref, mkv_ref, g_ref, o_ref):
    lo = _lo_mask()
    outs = []
    for p in range(2):
        q2 = q_ref[:, p * LANES:(p + 1) * LANES]
        mk = mkv_ref[:, p * LANES:(p + 1) * LANES]
        mv = mkv_ref[:, M_W + p * LANES:M_W + (p + 1) * LANES]
        zero = jnp.zeros_like(mk)
        num = None
        invs = []
        for t in range(2):
            keep = lo if t == 0 else jnp.logical_not(lo)
            s = _dot_nt(q2, jnp.where(keep, mk, zero))
            m = jnp.max(s, axis=-1, keepdims=True)
            e = jnp.exp(s - m)
            den = jnp.sum(e, axis=-1, keepdims=True)
            pv = _dot(e.astype(BF16), jnp.where(keep, mv, zero))
            num = pv if num is None else num + pv
            invs.append(1.0 / den)
        outs.append(num * jnp.where(lo, invs[0], invs[1]))
    of = jnp.concatenate(outs, axis=1)
    o_ref[...] = _group_rms(of, g_ref[...]).astype(o_ref.dtype)


def _memory_attention(proj, mkv, g, tq=512):
    b, s, _ = proj.shape
    mlen = mkv.shape[1]
    return pl.pallas_call(
        _mem_kernel,
        grid=(b, s // tq),
        in_specs=[
            pl.BlockSpec((None, tq, M_W), lambda bi, i: (bi, i, 6)),
            pl.BlockSpec((None, mlen, 2 * M_W), lambda bi, i: (bi, 0, 0)),
            pl.BlockSpec((1, M_W), lambda bi, i: (0, 0)),
        ],
        out_specs=pl.BlockSpec((None, tq, M_W), lambda bi, i: (bi, i, 0)),
        out_shape=jax.ShapeDtypeStruct((b, s, M_W), BF16),
        compiler_params=_params(2),
        name="memory_attention",
    )(proj, mkv, g)


def _layer_norm(z, g, b):
    mu = jnp.mean(z, axis=-1, keepdims=True)
    zc = z - mu
    var = jnp.mean(zc * zc, axis=-1, keepdims=True)
    return zc * lax.rsqrt(var + LN_EPS) * g + b


def _route(lg_t, rbias):
    e, t = lg_t.shape
    eg = e // N_GROUPS
    scores = 1.0 / (1.0 + jnp.exp(-lg_t))
    sel = scores + rbias
    row = lax.broadcasted_iota(jnp.int32, (e, t), 0)
    best = None
    gidx = None
    for gi in range(N_GROUPS):
        r = [sel[gi * eg + k:gi * eg + k + 1, :] for k in range(eg)]
        top2 = None
        for a in range(eg):
            for b in range(a + 1, eg):
                pair = r[a] + r[b]
                top2 = pair if top2 is None else jnp.maximum(top2, pair)
        if best is None:
            best, gidx = top2, jnp.zeros((1, t), jnp.int32)
        else:
            better = top2 > best
            gidx = jnp.where(better, gi, gidx)
            best = jnp.maximum(best, top2)
    in_group = (row // eg) == gidx
    masked = jnp.where(in_group, sel, -jnp.inf)
    m1 = jnp.max(masked, axis=0, keepdims=True)
    i1 = jnp.min(jnp.where(masked == m1, row, e), axis=0, keepdims=True)
    rest = jnp.where(row == i1, -jnp.inf, masked)
    m2 = jnp.max(rest, axis=0, keepdims=True)
    i2 = jnp.min(jnp.where(rest == m2, row, e), axis=0, keepdims=True)
    pick1 = row == i1
    pick2 = row == i2
    w1 = jnp.sum(jnp.where(pick1, scores, 0.0), axis=0, keepdims=True)
    w2 = jnp.sum(jnp.where(pick2, scores, 0.0), axis=0, keepdims=True)
    tot = w1 + w2
    return i1, i2, pick1, pick2, w1 / tot, w2 / tot


def _out_kernel(ma_ref, mb_ref, mm_ref, w_ref, x_ref, g_ref, b_ref, wr_ref, rb_ref, tri_ref,
                x1_ref, info_ref, wcol_ref, cnt_ref, carry_ref):
    @pl.when(pl.program_id(0) == 0)
    def _init():
        carry_ref[...] = jnp.zeros_like(carry_ref)

    y = _dot(ma_ref[...], w_ref[0:A_Q_W, :])
    y = y + _dot(mb_ref[...], w_ref[A_Q_W:A_Q_W + B_W, :])
    y = y + _dot(mm_ref[...], w_ref[A_Q_W + B_W:, :])
    x1 = _layer_norm(ALPHA * x_ref[...] + y, g_ref[...], b_ref[...])
    x1_ref[...] = x1
    x_hi = x1.astype(BF16)
    x_lo = (x1 - x_hi.astype(F32)).astype(BF16)
    hi2 = _dot(x_hi, wr_ref[...])
    lg = hi2[:, 0:LANES] + hi2[:, LANES:] + _dot(x_lo, wr_ref[:, 0:LANES])
    lg_t = lg.T[0:N_EXPERTS, :]
    i1, i2, pick1, pick2, w1, w2 = _route(lg_t, rb_ref[...])
    tm = lg_t.shape[1]
    member = jnp.where(jnp.logical_or(pick1, pick2), 1.0, 0.0)
    before = carry_ref[:, 0:1] + _dot(member.astype(BF16), tri_ref[...])
    r1 = jnp.sum(jnp.where(pick1, before, 0.0), axis=0, keepdims=True)
    r2 = jnp.sum(jnp.where(pick2, before, 0.0), axis=0, keepdims=True)
    total = carry_ref[...] + jnp.sum(member, axis=1, keepdims=True)
    carry_ref[...] = total
    cnt_ref[...] = total
    info_ref[...] = jnp.concatenate(
        [i1, i2, r1.astype(jnp.int32), r2.astype(jnp.int32), jnp.zeros((4, tm), jnp.int32)], axis=0)
    w_pad = jnp.concatenate([w1, w2, jnp.zeros((LANES - 2, tm), F32)], axis=0)
    wcol_ref[...] = w_pad.T


def _out_proj(ma, mb, mm, w_out, x, g, b, wr_cat, rbias, tm=1024):
    n = x.shape[0]
    row = lambda i: (i, 0)
    fix = lambda i: (0, 0)
    tri = jnp.asarray(np.triu(np.ones((tm, tm), np.float32), k=1), BF16)
    return pl.pallas_call(
        _out_kernel,
        grid=(n // tm,),
        in_specs=[
            pl.BlockSpec((tm, A_Q_W), row), pl.BlockSpec((tm, B_W), row), pl.BlockSpec((tm, M_W), row),
            pl.BlockSpec((D_MODEL, D_MODEL), fix), pl.BlockSpec((tm, D_MODEL), row),
            pl.BlockSpec((1, D_MODEL), fix), pl.BlockSpec((1, D_MODEL), fix),
            pl.BlockSpec((D_MODEL, 2 * LANES), fix), pl.BlockSpec((N_EXPERTS, 1), fix),
            pl.BlockSpec((tm, tm), fix),
        ],
        out_specs=[pl.BlockSpec((tm, D_MODEL), row), pl.BlockSpec((8, tm), lambda i: (0, i)),
                   pl.BlockSpec((tm, LANES), row), pl.BlockSpec((N_EXPERTS, LANES), fix)],
        out_shape=[jax.ShapeDtypeStruct((n, D_MODEL), F32), jax.ShapeDtypeStruct((8, n), jnp.int32),
                   jax.ShapeDtypeStruct((n, LANES), F32), jax.ShapeDtypeStruct((N_EXPERTS, LANES), F32)],
        scratch_shapes=[pltpu.VMEM((N_EXPERTS, LANES), F32)],
        compiler_params=_params(1),
        name="out_proj_ln_router",
    )(ma, mb, mm, w_out, x, g, b, wr_cat, rbias, tri)


MOE_ROW_TILE = 512
MOE_TOKEN_TILE = 512


SUBLANES = 8


def _index_copy(pos_hbm, idx_ref, isem, step, slot):
    width = pos_hbm.shape[1]
    dst = idx_ref.at[pl.ds(pl.multiple_of(slot * width, width), width)]
    return pltpu.make_async_copy(pos_hbm.at[step], dst, isem.at[slot])


assert D_MODEL == SUBLANES * LANES
ZERO_CHUNK = 64


def _token_tile(ref, p):
    return ref.at[pl.ds(pl.multiple_of(p * SUBLANES, SUBLANES), SUBLANES)]


def _store_token_tiles(dst_ref, x):
    t = x.shape[0]
    for c in range(SUBLANES):
        dst_ref[pl.ds(c, t, stride=SUBLANES), :] = x[:, c * LANES:(c + 1) * LANES]


def _load_token_tiles(src_ref, t):
    return jnp.concatenate([src_ref[pl.ds(c, t, stride=SUBLANES), :] for c in range(SUBLANES)], axis=1)


def _dispatch_kernel(lo_ref, hi_ref, pos_hbm, x_ref, xs_hbm, idx_ref, xt_ref, z_ref, isem, rsem, zsem, *, td):
    i = pl.program_id(0)
    n = pl.num_programs(0)
    slot = lax.rem(i, 2)

    @pl.when(i == 0)
    def _first():
        _index_copy(pos_hbm, idx_ref, isem, 0, 0).start()
        z_ref[...] = jnp.zeros_like(z_ref)
        one = z_ref.at[pl.ds(0, SUBLANES)]

        def fill(e, carry):
            a = lo_ref[e]
            b = hi_ref[e]
            n_single = jnp.minimum(b - a, jnp.bitwise_and(-a, ZERO_CHUNK - 1))
            a2 = a + n_single
            n_chunk = lax.shift_right_logical(b - a2, 6)

            def single(r, c):
                pltpu.make_async_copy(one, _token_tile(xs_hbm, a + r), zsem).start()
                return c

            def chunk(r, c):
                dst = xs_hbm.at[pl.ds(pl.multiple_of((a2 + r * ZERO_CHUNK) * SUBLANES, SUBLANES),
                                      ZERO_CHUNK * SUBLANES)]
                pltpu.make_async_copy(z_ref, dst, zsem).start()
                return c

            def single_wait(r, c):
                pltpu.make_async_copy(one, _token_tile(xs_hbm, 0), zsem).wait()
                return c

            def chunk_wait(r, c):
                pltpu.make_async_copy(z_ref, xs_hbm.at[pl.ds(0, ZERO_CHUNK * SUBLANES)], zsem).wait()
                return c

            lax.fori_loop(0, n_single, single, 0)
            lax.fori_loop(0, n_chunk, chunk, 0)
            lax.fori_loop(0, n_single, single_wait, 0)
            lax.fori_loop(0, n_chunk, chunk_wait, 0)
            return carry

        lax.fori_loop(0, N_EXPERTS, fill, 0)

    _index_copy(pos_hbm, idx_ref, isem, i, slot).wait()

    @pl.when(i + 1 < n)
    def _next():
        _index_copy(pos_hbm, idx_ref, isem, i + 1, 1 - slot).start()

    _store_token_tiles(xt_ref, x_ref[...])
    base = slot * (2 * td)

    def body(j, carry):
        for u in range(SUBLANES):
            t = j * SUBLANES + u
            src = _token_tile(xt_ref, t)
            for k in range(2):
                p = idx_ref[base + k * td + t]
                pltpu.make_async_copy(src, _token_tile(xs_hbm, p), rsem).start(priority=k)
        return carry

    lax.fori_loop(0, td // SUBLANES, body, 0)
    for _ in range(2):
        pltpu.make_async_copy(xt_ref, xs_hbm.at[pl.ds(0, td * SUBLANES)], rsem).wait()


def _dispatch(x1, pos_t, pad_lo, pad_hi, n_rows, td):
    n = x1.shape[0]
    kern = functools.partial(_dispatch_kernel, td=td)
    grid_spec = pltpu.PrefetchScalarGridSpec(
        num_scalar_prefetch=2,
        grid=(n // td,),
        in_specs=[pl.BlockSpec(memory_space=pl.ANY),
                  pl.BlockSpec((td, D_MODEL), lambda i, lo, hi: (i, 0))],
        out_specs=pl.BlockSpec(memory_space=pl.ANY),
        scratch_shapes=[pltpu.SMEM((4 * td,), jnp.int32),
                        pltpu.VMEM((td * SUBLANES, LANES), F32),
                        pltpu.VMEM((ZERO_CHUNK * SUBLANES, LANES), F32),
                        pltpu.SemaphoreType.DMA((2,)), pltpu.SemaphoreType.DMA(()),
                        pltpu.SemaphoreType.DMA(())],
    )
    return pl.pallas_call(
        kern,
        grid_spec=grid_spec,
        out_shape=jax.ShapeDtypeStruct((n_rows * SUBLANES, LANES), F32),
        compiler_params=_params(1),
        name="moe_dispatch",
    )(pad_lo, pad_hi, pos_t, x1)


def _gmm_kernel(te_ref, rows_ref, xs_ref, wg_ref, wu_ref, wd_ref, ys_ref):
    del te_ref
    nrows = rows_ref[pl.program_id(0)]

    @pl.when(nrows > 0)
    def _compute():
        x = _load_token_tiles(xs_ref, xs_ref.shape[0] // SUBLANES).astype(BF16)
        gate = _dot(x, wg_ref[...].astype(BF16))
        up = _dot(x, wu_ref[...].astype(BF16))
        h = (gate * (1.0 / (1.0 + jnp.exp(-gate))) * up).astype(BF16)
        _store_token_tiles(ys_ref, _dot(h, wd_ref[...].astype(BF16)))

    @pl.when(nrows == 0)
    def _empty():
        ys_ref[...] = jnp.zeros_like(ys_ref)


def _grouped_mlp(xs, tile_expert, tile_rows, wg, wu, wd, layer, tm):
    n_rows = xs.shape[0] // SUBLANES
    wspec = lambda shape: pl.BlockSpec((None, None) + shape, lambda g, te, tr: (layer, te[g], 0, 0))
    grid_spec = pltpu.PrefetchScalarGridSpec(
        num_scalar_prefetch=2,
        grid=(n_rows // tm,),
        in_specs=[pl.BlockSpec((tm * SUBLANES, LANES), lambda g, te, tr: (g, 0)),
                  wspec((D_MODEL, D_EXPERT)), wspec((D_MODEL, D_EXPERT)), wspec((D_EXPERT, D_MODEL))],
        out_specs=pl.BlockSpec((tm * SUBLANES, LANES), lambda g, te, tr: (g, 0)),
    )
    return pl.pallas_call(
        _gmm_kernel,
        grid_spec=grid_spec,
        out_shape=jax.ShapeDtypeStruct(xs.shape, F32),
        compiler_params=_params(1),
        name="moe_grouped_mlp",
    )(tile_expert, tile_rows, xs, wg, wu, wd)


def _combine_kernel(pos_hbm, ys_hbm, x1_ref, w_ref, g_ref, b_ref, o_ref, idx_ref, buf_ref, isem, rsem, *, td):
    i = pl.program_id(0)
    n = pl.num_programs(0)
    slot = lax.rem(i, 2)

    def issue_rows(sl):
        base = sl * (2 * td)

        def body(j, carry):
            for u in range(SUBLANES):
                t = j * SUBLANES + u
                for k in range(2):
                    p = idx_ref[base + k * td + t]
                    pltpu.make_async_copy(_token_tile(ys_hbm, p), _token_tile(buf_ref.at[sl, k], t),
                                          rsem.at[sl]).start(priority=k)
            return carry

        lax.fori_loop(0, td // SUBLANES, body, 0)

    @pl.when(i == 0)
    def _first():
        _index_copy(pos_hbm, idx_ref, isem, 0, 0).start()
        _index_copy(pos_hbm, idx_ref, isem, 0, 0).wait()
        issue_rows(0)

        @pl.when(n > 1)
        def _():
            _index_copy(pos_hbm, idx_ref, isem, 1, 1).start()

    @pl.when(i + 1 < n)
    def _next():
        _index_copy(pos_hbm, idx_ref, isem, i + 1, 1 - slot).wait()
        issue_rows(1 - slot)

        @pl.when(i + 2 < n)
        def _():
            _index_copy(pos_hbm, idx_ref, isem, i + 2, slot).start()

    for k in range(2):
        pltpu.make_async_copy(ys_hbm.at[pl.ds(0, td * SUBLANES)], buf_ref.at[slot, k], rsem.at[slot]).wait()
    w = w_ref[...]
    y = (w[:, 0:1] * _load_token_tiles(buf_ref.at[slot, 0], td)
         + w[:, 1:2] * _load_token_tiles(buf_ref.at[slot, 1], td))
    o_ref[...] = _layer_norm(ALPHA * x1_ref[...] + y, g_ref[...], b_ref[...])


def _combine(ys, pos_t, x1, wcol, g, b, td):
    n = x1.shape[0]
    row = lambda i: (i, 0)
    fix = lambda i: (0, 0)
    kern = functools.partial(_combine_kernel, td=td)
    return pl.pallas_call(
        kern,
        grid=(n // td,),
        in_specs=[pl.BlockSpec(memory_space=pl.ANY), pl.BlockSpec(memory_space=pl.ANY),
                  pl.BlockSpec((td, D_MODEL), row), pl.BlockSpec((td, LANES), row),
                  pl.BlockSpec((1, D_MODEL), fix), pl.BlockSpec((1, D_MODEL), fix)],
        out_specs=pl.BlockSpec((td, D_MODEL), row),
        out_shape=jax.ShapeDtypeStruct((n, D_MODEL), F32),
        scratch_shapes=[pltpu.SMEM((4 * td,), jnp.int32),
                        pltpu.VMEM((2, 2, td * SUBLANES, LANES), F32),
                        pltpu.SemaphoreType.DMA((2,)), pltpu.SemaphoreType.DMA((2,))],
        compiler_params=_params(1),
        name="moe_combine_ln2",
    )(pos_t, ys, x1, wcol, g, b)


def _moe_routed(x1, info, wcol, cnt, p):
    n = x1.shape[0]
    tm, td = MOE_ROW_TILE, MOE_TOKEN_TILE
    n_tiles = 2 * n // tm + N_EXPERTS
    counts = cnt[:, 0].astype(jnp.int32)
    padded = (counts + (tm - 1)) // tm * tm
    ends = jnp.cumsum(padded)
    offs = ends - padded
    starts = jnp.arange(n_tiles, dtype=jnp.int32) * tm
    te = jnp.sum((ends[None, :] <= starts[:, None]).astype(jnp.int32), axis=1)
    tile_expert = jnp.minimum(te, N_EXPERTS - 1)
    experts = jnp.arange(N_EXPERTS, dtype=jnp.int32)
    pick = tile_expert[:, None] == experts[None, :]
    seg_end = jnp.sum(jnp.where(pick, (offs + counts)[None, :], 0), axis=1)
    tile_rows = jnp.where(te < N_EXPERTS, jnp.clip(seg_end - starts, 0, tm), 0).astype(jnp.int32)

    def position(e, r):
        return r + jnp.sum(jnp.where(e[None, :] == experts[:, None], offs[:, None], 0), axis=0)

    pos1 = position(info[0], info[2]).reshape(n // td, td)
    pos2 = position(info[1], info[3]).reshape(n // td, td)
    pos_t = jnp.concatenate([pos1, pos2], axis=1)
    pad_lo = offs + counts
    pad_hi = ends.at[N_EXPERTS - 1].set(n_tiles * tm)
    xs = _dispatch(x1, pos_t, pad_lo, pad_hi, n_tiles * tm, td)
    ys = _grouped_mlp(xs, tile_expert, tile_rows, p["w_gate"], p["w_up"], p["w_down"], p["layer"], tm)
    return _combine(ys, pos_t, x1, wcol, p["ln2_g"], p["ln2_b"], td)


def _layer(x, mem, p):
    b, s, d = x.shape
    n = b * s
    xf = x.reshape(n, d)
    proj = _matmul(xf, p["w_in"], 512, "in_proj").reshape(b, s, IN_W)
    mkv = _matmul(mem.reshape(-1, d), p["w_mem_kv"], 512, "mem_kv").reshape(b, -1, 2 * M_W)
    oa = _window_attention(proj, p["sink"], p["g_a"])
    ob = _neighborhood_attention(proj, p["na_bias"], p["g_b"])
    om = _memory_attention(proj, mkv, p["g_m"])
    x1, info, wcol, cnt = _out_proj(oa.reshape(n, A_Q_W), ob.reshape(n, B_W), om.reshape(n, M_W), p["w_out"], xf,
                                    p["ln1_g"], p["ln1_b"], p["wr_cat"], p["rbias"])
    y = _moe_routed(x1, info, wcol, cnt, p)
    return y.reshape(b, s, d)


def kernel(x_prompt, x_sample, mem_prompt, mem_sample, w_in, w_mem_kv, sink_logits, rpb, grp_norm_g, w_out,
           ln1_g, ln1_b, w_router, router_bias, w_gate, w_up, w_down, ln2_g, ln2_b):
    seq = x_prompt.shape[1]
    assert x_sample.shape[1] == seq
    wr = jnp.pad(w_router.astype(F32), ((0, 0), (0, LANES - N_EXPERTS)))
    wr_hi = wr.astype(BF16)
    wr_cat = jnp.concatenate([wr_hi, (wr - wr_hi.astype(F32)).astype(BF16)], axis=1)
    q_scale = np.ones((IN_W,), np.float32)
    for c0, width in ((0, A_Q_W), (A_Q_W + 2 * LANES, B_W), (IN_W - M_W, M_W)):
        q_scale[c0:c0 + width] = SCALE
    rbias = router_bias.astype(F32).reshape(N_EXPERTS, 1)
    layers = []
    for l in range(DEPTH):
        g = grp_norm_g[l].astype(F32)
        layers.append(dict(
            w_in=(w_in[l] * q_scale).astype(BF16), w_mem_kv=w_mem_kv[l].astype(BF16),
            sink=sink_logits[l].astype(F32), na_bias=_na_bias(rpb[l], seq),
            g_a=g[:A_Q_W].reshape(1, -1), g_b=g[A_Q_W:A_Q_W + B_W].reshape(1, -1),
            g_m=g[A_Q_W + B_W:].reshape(1, -1),
            w_out=w_out[l].astype(BF16),
            ln1_g=ln1_g[l].astype(F32).reshape(1, -1), ln1_b=ln1_b[l].astype(F32).reshape(1, -1),
            wr_cat=wr_cat, rbias=rbias,
            w_gate=w_gate, w_up=w_up, w_down=w_down, layer=l,
            ln2_g=ln2_g[l].astype(F32).reshape(1, -1), ln2_b=ln2_b[l].astype(F32).reshape(1, -1)))

    def trunk(x, mem):
        for p in layers:
            x = _layer(x, mem, p)
        return x

    return (trunk(x_prompt, mem_prompt), trunk(x_sample, mem_sample))
```

```python
import functools

import jax
import jax.numpy as jnp
import numpy as np
from jax import lax
from jax.experimental import pallas as pl
from jax.experimental.pallas import tpu as pltpu

F32 = jnp.float32
BF16 = jnp.bfloat16

D_MODEL = 1024
DEPTH = 2
HEAD_DIM = 64
A_HEADS = 8
WINDOW = 128
B_HEADS = 4
GRID_W = 64
NA_ROWS = 8
NA_COLS = 16
M_HEADS = 4
N_EXPERTS = 16
N_GROUPS = 4
D_EXPERT = 512
LN_EPS = 1e-5
A_Q_W = 512
B_W = 256
M_W = 256
IN_W = 1792
ALPHA = (2 * DEPTH) ** 0.25
SCALE = HEAD_DIM ** -0.5
NEG = -1e30

LANES = 128
SOFTMAX_HEADS = 2
NA_TILE_ROWS = 2
NA_STEP_TILES = 4
NA_KEY_ROWS = 10
VMEM_LIMIT = 56 * 1024 * 1024


def _params(n_axes, vmem=VMEM_LIMIT):
    return pltpu.CompilerParams(dimension_semantics=("arbitrary",) * n_axes, vmem_limit_bytes=vmem)


def _dot_nt(a, b):
    return lax.dot_general(a, b, (((1,), (1,)), ((), ())), preferred_element_type=F32)


def _dot(a, b):
    return jnp.dot(a, b, preferred_element_type=F32)


def _lo_mask():
    return lax.broadcasted_iota(jnp.int32, (1, LANES), 1) < HEAD_DIM


def _mm_kernel(x_ref, w_ref, o_ref):
    o_ref[...] = _dot(x_ref[...].astype(BF16), w_ref[...]).astype(o_ref.dtype)


def _matmul(x, w, tm, name):
    n, k = x.shape
    m = w.shape[1]
    tm = min(tm, n)
    assert n % tm == 0
    return pl.pallas_call(
        _mm_kernel,
        grid=(n // tm,),
        in_specs=[pl.BlockSpec((tm, k), lambda i: (i, 0)), pl.BlockSpec((k, m), lambda i: (0, 0))],
        out_specs=pl.BlockSpec((tm, m), lambda i: (i, 0)),
        out_shape=jax.ShapeDtypeStruct((n, m), BF16),
        compiler_params=_params(1),
        name=name,
    )(x, w)


def _group_rms(of, g):
    ms = jnp.mean(of * of, axis=-1, keepdims=True)
    return of * lax.rsqrt(ms + LN_EPS) * g


def _win_kernel(sink_ref, q_ref, k_ref, v_ref, g_ref, o_ref, ks_ref, vs_ref, bias_ref, s_ref, p_ref, *,
                seq, tq):
    i = pl.program_id(1)
    lo = _lo_mask()
    chunk = 512

    @pl.when(i == 0)
    def _build():
        zeros = jnp.zeros((WINDOW, LANES), BF16)
        for t in range(4):
            for dst in (ks_ref, vs_ref):
                dst[t, 0:WINDOW, :] = zeros
                dst[t, seq + WINDOW:seq + 2 * WINDOW, :] = zeros

        def body(c, carry):
            r0 = pl.multiple_of(c * chunk, chunk)
            for src, dst in ((k_ref, ks_ref), (v_ref, vs_ref)):
                x = src[pl.ds(r0, chunk), :].astype(F32)
                xr = pltpu.roll(x, HEAD_DIM, 1)
                rows = pl.ds(r0 + WINDOW, chunk)
                dst[0, rows, :] = jnp.where(lo, x, 0.0).astype(BF16)
                dst[1, rows, :] = jnp.where(lo, 0.0, xr).astype(BF16)
                dst[2, rows, :] = jnp.where(lo, xr, 0.0).astype(BF16)
                dst[3, rows, :] = jnp.where(lo, 0.0, x).astype(BF16)
            return carry

        lax.fori_loop(0, seq // chunk, body, 0)

    kw = 3 * WINDOW

    @pl.when(jnp.logical_and(pl.program_id(0) == 0, i == 0))
    def _build_bias():
        a_i = lax.broadcasted_iota(jnp.int32, (WINDOW, kw), 0)
        j_i = lax.broadcasted_iota(jnp.int32, (WINDOW, kw), 1)
        dist = jnp.abs(a_i + WINDOW - j_i)
        absd = dist.astype(F32)
        for variant in range(3):
            valid = dist <= WINDOW
            if variant == 0:
                valid = valid & (j_i >= WINDOW)
            if variant == 2:
                valid = valid & (j_i < 2 * WINDOW)
            for head in range(A_HEADS):
                slope = 2.0 ** (-8.0 * (head + 1) / A_HEADS)
                bias_ref[variant * A_HEADS + head] = jnp.where(valid, -slope * absd, NEG)

    g = g_ref[...]

    def sub(j, carry):
        row0 = pl.multiple_of(j * WINDOW, WINDOW)
        qs = pl.multiple_of(i * tq + j * WINDOW, WINDOW)
        variant = jnp.where(qs == 0, 0, jnp.where(qs == seq - WINDOW, 2, 1))
        win = pl.ds(qs, kw)
        for head in range(A_HEADS):
            c, t = divmod(head, 2)
            q2 = q_ref[pl.ds(row0, WINDOW), c * LANES:(c + 1) * LANES]
            s_ref[head] = _dot_nt(q2, ks_ref[2 * (c // 2) + t, win, :]) + bias_ref[variant * A_HEADS + head]
        hs = SOFTMAX_HEADS
        invs = []
        for h0 in range(0, A_HEADS, hs):
            s = s_ref[h0:h0 + hs]
            hid = lax.broadcasted_iota(jnp.int32, (hs, 1, 1), 0)
            sk = jnp.zeros((hs, 1, 1), F32)
            for u in range(hs):
                sk = jnp.where(hid == u, sink_ref[h0 + u], sk)
            m = jnp.maximum(jnp.max(s, axis=-1, keepdims=True), sk)
            p = jnp.exp(s - m)
            den = jnp.sum(p, axis=-1, keepdims=True) + jnp.exp(sk - m)
            p_ref[h0:h0 + hs] = p.astype(BF16)
            inv = 1.0 / den
            invs.extend(inv[u] for u in range(hs))
        outs = []
        for c in range(4):
            h = c // 2
            num = (_dot(p_ref[2 * c], vs_ref[2 * h, win, :])
                   + _dot(p_ref[2 * c + 1], vs_ref[2 * h + 1, win, :]))
            outs.append(num * jnp.where(lo, invs[2 * c], invs[2 * c + 1]))
        of = jnp.concatenate(outs, axis=1)
        o_ref[pl.ds(row0, WINDOW), :] = _group_rms(of, g).astype(o_ref.dtype)
        return carry

    lax.fori_loop(0, tq // WINDOW, sub, 0, unroll=2)


def _window_attention(proj, sink, g, tq=512):
    b, s, _ = proj.shape
    assert s % tq == 0 and s >= 2 * WINDOW
    kern = functools.partial(_win_kernel, seq=s, tq=tq)
    return pl.pallas_call(
        kern,
        grid=(b, s // tq),
        in_specs=[
            pl.BlockSpec(memory_space=pltpu.SMEM),
            pl.BlockSpec((None, tq, A_Q_W), lambda bi, i: (bi, i, 0)),
            pl.BlockSpec((None, s, LANES), lambda bi, i: (bi, 0, 4)),
            pl.BlockSpec((None, s, LANES), lambda bi, i: (bi, 0, 5)),
            pl.BlockSpec((1, A_Q_W), lambda bi, i: (0, 0)),
        ],
        out_specs=pl.BlockSpec((None, tq, A_Q_W), lambda bi, i: (bi, i, 0)),
        out_shape=jax.ShapeDtypeStruct((b, s, A_Q_W), BF16),
        scratch_shapes=[pltpu.VMEM((4, s + 2 * WINDOW, LANES), BF16),
                        pltpu.VMEM((4, s + 2 * WINDOW, LANES), BF16),
                        pltpu.VMEM((3 * A_HEADS, WINDOW, 3 * WINDOW), F32),
                        pltpu.VMEM((A_HEADS, WINDOW, 3 * WINDOW), F32),
                        pltpu.VMEM((A_HEADS, WINDOW, 3 * WINDOW), BF16)],
        compiler_params=_params(2),
        name="window_attention",
    )(sink, proj, proj, proj, g)


def _na_tables(seq):
    rows = seq // GRID_W
    kh = min(NA_ROWS, rows)
    assert rows % NA_TILE_ROWS == 0 and rows >= NA_KEY_ROWS and NA_KEY_ROWS % 2 == 0
    nt = rows // NA_TILE_ROWS
    u = np.arange(NA_TILE_ROWS * GRID_W) // GRID_W
    c = np.arange(NA_TILE_ROWS * GRID_W) % GRID_W
    ki = np.arange(NA_KEY_ROWS * GRID_W) // GRID_W
    kc = np.arange(NA_KEY_ROWS * GRID_W) % GRID_W
    cs = np.clip(c - NA_COLS // 2, 0, GRID_W - NA_COLS)
    col_ok = (kc[None, :] >= cs[:, None]) & (kc[None, :] < cs[:, None] + NA_COLS)
    dc = np.clip(kc[None, :] - c[:, None] + NA_COLS - 1, 0, 2 * NA_COLS - 2)
    types, type_of, kstart = [], [], []
    for t in range(nt):
        r0 = t * NA_TILE_ROWS
        k0 = int(np.clip(r0 - kh // 2, 0, rows - NA_KEY_ROWS))
        k0 -= k0 % 2
        r = r0 + u
        rs = np.clip(r - kh // 2, 0, rows - kh)
        kr = k0 + ki
        row_ok = (kr[None, :] >= rs[:, None]) & (kr[None, :] < rs[:, None] + kh)
        assert (rs >= k0).all() and (rs + kh <= k0 + NA_KEY_ROWS).all()
        dr = np.clip(kr[None, :] - r[:, None] + NA_ROWS - 1, 0, 2 * NA_ROWS - 2)
        key = (dr.tobytes(), row_ok.tobytes())
        for idx, (k_, _, _) in enumerate(types):
            if k_ == key:
                type_of.append(idx)
                break
        else:
            type_of.append(len(types))
            types.append((key, dr, row_ok & col_ok))
        kstart.append(k0)
    dr_rows = np.stack([t_[1][::GRID_W, ::GRID_W] for t_ in types])
    ok_all = np.stack([t_[2] for t_ in types])
    dc_cols = dc[:GRID_W, :GRID_W]
    return (np.asarray(type_of, np.int32), np.asarray(kstart, np.int32), dr_rows, dc_cols, ok_all)


def _na_bias(rpb, seq):
    _, _, dr_rows, dc_cols, ok = _na_tables(seq)
    nt = dr_rows.shape[0]
    hi = lax.Precision.HIGHEST
    col_sel = jnp.asarray(np.eye(2 * NA_COLS - 1, dtype=np.float32)[dc_cols])
    row_sel = jnp.asarray(np.eye(2 * NA_ROWS - 1, dtype=np.float32)[dr_rows])
    by_col = jnp.einsum("hdj,ckj->hdck", rpb.astype(F32), col_sel, precision=hi)
    bias = jnp.einsum("tuid,hdck->thucik", row_sel, by_col, precision=hi)
    bias = bias.reshape(nt, B_HEADS, NA_TILE_ROWS * GRID_W, NA_KEY_ROWS * GRID_W)
    return jnp.where(ok[:, None], bias, NEG)


def _na_kernel(type_ref, kstart_ref, q_ref, k_ref, v_ref, *rest, seq):
    del type_ref
    bias_refs = rest[:NA_STEP_TILES]
    g_ref, o_ref, ks_ref, vs_ref, s_ref, p_ref = rest[NA_STEP_TILES:]
    i = pl.program_id(1)
    lo = _lo_mask()
    chunk = 512

    @pl.when(i == 0)
    def _build():
        def body(c, carry):
            rows = pl.ds(pl.multiple_of(c * chunk, chunk), chunk)
            for src, dst in ((k_ref, ks_ref), (v_ref, vs_ref)):
                for p in range(2):
                    x = src[rows, p * LANES:(p + 1) * LANES]
                    dst[2 * p, rows, :] = jnp.where(lo, x, jnp.zeros_like(x))
                    dst[2 * p + 1, rows, :] = jnp.where(lo, jnp.zeros_like(x), x)
            return carry

        lax.fori_loop(0, seq // chunk, body, 0)

    nk = NA_KEY_ROWS * GRID_W
    tq = NA_TILE_ROWS * GRID_W
    units = [(t, head) for t in range(NA_STEP_TILES) for head in range(B_HEADS)]
    wins = [pl.ds(pl.multiple_of(kstart_ref[i * NA_STEP_TILES + t] * GRID_W, LANES), nk)
            for t in range(NA_STEP_TILES)]
    for u, (t, head) in enumerate(units):
        q2 = q_ref[t * tq:(t + 1) * tq, (head // 2) * LANES:(head // 2 + 1) * LANES]
        s_ref[u] = _dot_nt(q2, ks_ref[head, wins[t], :]) + bias_refs[t][head]
    invs = []
    for u in range(len(units)):
        s = s_ref[u]
        m = jnp.max(s, axis=-1, keepdims=True)
        e = jnp.exp(s - m)
        invs.append(1.0 / jnp.sum(e, axis=-1, keepdims=True))
        p_ref[u] = e.astype(BF16)
    g = g_ref[...]
    for t in range(NA_STEP_TILES):
        outs = []
        for p in range(2):
            u = t * B_HEADS + 2 * p
            num = _dot(p_ref[u], vs_ref[2 * p, wins[t], :]) + _dot(p_ref[u + 1], vs_ref[2 * p + 1, wins[t], :])
            outs.append(num * jnp.where(lo, invs[u], invs[u + 1]))
        of = jnp.concatenate(outs, axis=1)
        o_ref[t * tq:(t + 1) * tq, :] = _group_rms(of, g).astype(o_ref.dtype)


def _neighborhood_attention(proj, bias, g):
    b, s, _ = proj.shape
    type_of, kstart, _, _, _ = _na_tables(s)
    tq = NA_TILE_ROWS * GRID_W
    nk = NA_KEY_ROWS * GRID_W
    step_q = NA_STEP_TILES * tq
    assert s % step_q == 0
    kern = functools.partial(_na_kernel, seq=s)

    def bias_spec(t):
        return pl.BlockSpec((None, B_HEADS, tq, nk), lambda bi, i, ty, ks: (ty[i * NA_STEP_TILES + t], 0, 0, 0))

    grid_spec = pltpu.PrefetchScalarGridSpec(
        num_scalar_prefetch=2,
        grid=(b, s // step_q),
        in_specs=[
            pl.BlockSpec((None, step_q, B_W), lambda bi, i, ty, ks: (bi, i, 3)),
            pl.BlockSpec((None, s, B_W), lambda bi, i, ty, ks: (bi, 0, 4)),
            pl.BlockSpec((None, s, B_W), lambda bi, i, ty, ks: (bi, 0, 5)),
            *[bias_spec(t) for t in range(NA_STEP_TILES)],
            pl.BlockSpec((1, B_W), lambda bi, i, ty, ks: (0, 0)),
        ],
        out_specs=pl.BlockSpec((None, step_q, B_W), lambda bi, i, ty, ks: (bi, i, 0)),
        scratch_shapes=[pltpu.VMEM((4, s, LANES), BF16), pltpu.VMEM((4, s, LANES), BF16),
                        pltpu.VMEM((NA_STEP_TILES * B_HEADS, tq, nk), F32),
                        pltpu.VMEM((NA_STEP_TILES * B_HEADS, tq, nk), BF16)],
    )
    return pl.pallas_call(
        kern,
        grid_spec=grid_spec,
        out_shape=jax.ShapeDtypeStruct((b, s, B_W), BF16),
        compiler_params=_params(2),
        name="neighborhood_attention",
    )(jnp.asarray(type_of), jnp.asarray(kstart), proj, proj, proj, *([bias] * NA_STEP_TILES), g)


def _mem_kernel(q_ref, mkv_ref, g_ref, o_ref):
    lo = _lo_mask()
    outs = []
    for p in range(2):
        q2 = q_ref[:, p * LANES:(p + 1) * LANES]
        mk = mkv_ref[:, p * LANES:(p + 1) * LANES]
        mv = mkv_ref[:, M_W + p * LANES:M_W + (p + 1) * LANES]
        zero = jnp.zeros_like(mk)
        num = None
        invs = []
        for t in range(2):
            keep = lo if t == 0 else jnp.logical_not(lo)
            s = _dot_nt(q2, jnp.where(keep, mk, zero))
            m = jnp.max(s, axis=-1, keepdims=True)
            e = jnp.exp(s - m)
            den = jnp.sum(e, axis=-1, keepdims=True)
            pv = _dot(e.astype(BF16), jnp.where(keep, mv, zero))
            num = pv if num is None else num + pv
            invs.append(1.0 / den)
        outs.append(num * jnp.where(lo, invs[0], invs[1]))
    of = jnp.concatenate(outs, axis=1)
    o_ref[...] = _group_rms(of, g_ref[...]).astype(o_ref.dtype)


def _memory_attention(proj, mkv, g, tq=512):
    b, s, _ = proj.shape
    mlen = mkv.shape[1]
    return pl.pallas_call(
        _mem_kernel,
        grid=(b, s // tq),
        in_specs=[
            pl.BlockSpec((None, tq, M_W), lambda bi, i: (bi, i, 6)),
            pl.BlockSpec((None, mlen, 2 * M_W), lambda bi, i: (bi, 0, 0)),
            pl.BlockSpec((1, M_W), lambda bi, i: (0, 0)),
        ],
        out_specs=pl.BlockSpec((None, tq, M_W), lambda bi, i: (bi, i, 0)),
        out_shape=jax.ShapeDtypeStruct((b, s, M_W), BF16),
        compiler_params=_params(2),
        name="memory_attention",
    )(proj, mkv, g)


def _layer_norm(z, g, b):
    mu = jnp.mean(z, axis=-1, keepdims=True)
    zc = z - mu
    var = jnp.mean(zc * zc, axis=-1, keepdims=True)
    return zc * lax.rsqrt(var + LN_EPS) * g + b


def _route(lg_t, rbias):
    e, t = lg_t.shape
    eg = e // N_GROUPS
    scores = 1.0 / (1.0 + jnp.exp(-lg_t))
    sel = scores + rbias
    row = lax.broadcasted_iota(jnp.int32, (e, t), 0)
    best = None
    gidx = None
    for gi in range(N_GROUPS):
        r = [sel[gi * eg + k:gi * eg + k + 1, :] for k in range(eg)]
        top2 = None
        for a in range(eg):
            for b in range(a + 1, eg):
                pair = r[a] + r[b]
                top2 = pair if top2 is None else jnp.maximum(top2, pair)
        if best is None:
            best, gidx = top2, jnp.zeros((1, t), jnp.int32)
        else:
            better = top2 > best
            gidx = jnp.where(better, gi, gidx)
            best = jnp.maximum(best, top2)
    in_group = (row // eg) == gidx
    masked = jnp.where(in_group, sel, -jnp.inf)
    m1 = jnp.max(masked, axis=0, keepdims=True)
    i1 = jnp.min(jnp.where(masked == m1, row, e), axis=0, keepdims=True)
    rest = jnp.where(row == i1, -jnp.inf, masked)
    m2 = jnp.max(rest, axis=0, keepdims=True)
    i2 = jnp.min(jnp.where(rest == m2, row, e), axis=0, keepdims=True)
    pick1 = row == i1
    pick2 = row == i2
    w1 = jnp.sum(jnp.where(pick1, scores, 0.0), axis=0, keepdims=True)
    w2 = jnp.sum(jnp.where(pick2, scores, 0.0), axis=0, keepdims=True)
    tot = w1 + w2
    return i1, i2, pick1, pick2, w1 / tot, w2 / tot


def _out_kernel(ma_ref, mb_ref, mm_ref, w_ref, x_ref, g_ref, b_ref, wr_ref, rb_ref, tri_ref,
                x1_ref, info_ref, wcol_ref, cnt_ref, carry_ref):
    @pl.when(pl.program_id(0) == 0)
    def _init():
        carry_ref[...] = jnp.zeros_like(carry_ref)

    y = _dot(ma_ref[...], w_ref[0:A_Q_W, :])
    y = y + _dot(mb_ref[...], w_ref[A_Q_W:A_Q_W + B_W, :])
    y = y + _dot(mm_ref[...], w_ref[A_Q_W + B_W:, :])
    x1 = _layer_norm(ALPHA * x_ref[...] + y, g_ref[...], b_ref[...])
    x1_ref[...] = x1
    x_hi = x1.astype(BF16)
    x_lo = (x1 - x_hi.astype(F32)).astype(BF16)
    hi2 = _dot(x_hi, wr_ref[...])
    lg = hi2[:, 0:LANES] + hi2[:, LANES:] + _dot(x_lo, wr_ref[:, 0:LANES])
    lg_t = lg.T[0:N_EXPERTS, :]
    i1, i2, pick1, pick2, w1, w2 = _route(lg_t, rb_ref[...])
    tm = lg_t.shape[1]
    member = jnp.where(jnp.logical_or(pick1, pick2), 1.0, 0.0)
    before = carry_ref[:, 0:1] + _dot(member.astype(BF16), tri_ref[...])
    r1 = jnp.sum(jnp.where(pick1, before, 0.0), axis=0, keepdims=True)
    r2 = jnp.sum(jnp.where(pick2, before, 0.0), axis=0, keepdims=True)
    total = carry_ref[...] + jnp.sum(member, axis=1, keepdims=True)
    carry_ref[...] = total
    cnt_ref[...] = total
    info_ref[...] = jnp.concatenate(
        [i1, i2, r1.astype(jnp.int32), r2.astype(jnp.int32), jnp.zeros((4, tm), jnp.int32)], axis=0)
    w_pad = jnp.concatenate([w1, w2, jnp.zeros((LANES - 2, tm), F32)], axis=0)
    wcol_ref[...] = w_pad.T


def _out_proj(ma, mb, mm, w_out, x, g, b, wr_cat, rbias, tm=1024):
    n = x.shape[0]
    row = lambda i: (i, 0)
    fix = lambda i: (0, 0)
    tri = jnp.asarray(np.triu(np.ones((tm, tm), np.float32), k=1), BF16)
    return pl.pallas_call(
        _out_kernel,
        grid=(n // tm,),
        in_specs=[
            pl.BlockSpec((tm, A_Q_W), row), pl.BlockSpec((tm, B_W), row), pl.BlockSpec((tm, M_W), row),
            pl.BlockSpec((D_MODEL, D_MODEL), fix), pl.BlockSpec((tm, D_MODEL), row),
            pl.BlockSpec((1, D_MODEL), fix), pl.BlockSpec((1, D_MODEL), fix),
            pl.BlockSpec((D_MODEL, 2 * LANES), fix), pl.BlockSpec((N_EXPERTS, 1), fix),
            pl.BlockSpec((tm, tm), fix),
        ],
        out_specs=[pl.BlockSpec((tm, D_MODEL), row), pl.BlockSpec((8, tm), lambda i: (0, i)),
                   pl.BlockSpec((tm, LANES), row), pl.BlockSpec((N_EXPERTS, LANES), fix)],
        out_shape=[jax.ShapeDtypeStruct((n, D_MODEL), F32), jax.ShapeDtypeStruct((8, n), jnp.int32),
                   jax.ShapeDtypeStruct((n, LANES), F32), jax.ShapeDtypeStruct((N_EXPERTS, LANES), F32)],
        scratch_shapes=[pltpu.VMEM((N_EXPERTS, LANES), F32)],
        compiler_params=_params(1),
        name="out_proj_ln_router",
    )(ma, mb, mm, w_out, x, g, b, wr_cat, rbias, tri)


MOE_ROW_TILE = 512
MOE_TOKEN_TILE = 512


SUBLANES = 8


def _index_copy(pos_hbm, idx_ref, isem, step, slot):
    width = pos_hbm.shape[1]
    dst = idx_ref.at[pl.ds(pl.multiple_of(slot * width, width), width)]
    return pltpu.make_async_copy(pos_hbm.at[step], dst, isem.at[slot])


assert D_MODEL == SUBLANES * LANES
ZERO_CHUNK = 64


def _token_tile(ref, p):
    return ref.at[pl.ds(pl.multiple_of(p * SUBLANES, SUBLANES), SUBLANES)]


def _store_token_tiles(dst_ref, x):
    t = x.shape[0]
    for c in range(SUBLANES):
        dst_ref[pl.ds(c, t, stride=SUBLANES), :] = x[:, c * LANES:(c + 1) * LANES]


def _load_token_tiles(src_ref, t):
    return jnp.concatenate([src_ref[pl.ds(c, t, stride=SUBLANES), :] for c in range(SUBLANES)], axis=1)


def _dispatch_kernel(lo_ref, hi_ref, pos_hbm, x_ref, xs_hbm, idx_ref, xt_ref, z_ref, isem, rsem, zsem, *, td):
    i = pl.program_id(0)
    n = pl.num_programs(0)
    slot = lax.rem(i, 2)

    @pl.when(i == 0)
    def _first():
        _index_copy(pos_hbm, idx_ref, isem, 0, 0).start()
        z_ref[...] = jnp.zeros_like(z_ref)
        one = z_ref.at[pl.ds(0, SUBLANES)]

        def fill(e, carry):
            a = lo_ref[e]
            b = hi_ref[e]
            n_single = jnp.minimum(b - a, jnp.bitwise_and(-a, ZERO_CHUNK - 1))
            a2 = a + n_single
            n_chunk = lax.shift_right_logical(b - a2, 6)

            def single(r, c):
                pltpu.make_async_copy(one, _token_tile(xs_hbm, a + r), zsem).start()
                return c

            def chunk(r, c):
                dst = xs_hbm.at[pl.ds(pl.multiple_of((a2 + r * ZERO_CHUNK) * SUBLANES, SUBLANES),
                                      ZERO_CHUNK * SUBLANES)]
                pltpu.make_async_copy(z_ref, dst, zsem).start()
                return c

            def single_wait(r, c):
                pltpu.make_async_copy(one, _token_tile(xs_hbm, 0), zsem).wait()
                return c

            def chunk_wait(r, c):
                pltpu.make_async_copy(z_ref, xs_hbm.at[pl.ds(0, ZERO_CHUNK * SUBLANES)], zsem).wait()
                return c

            lax.fori_loop(0, n_single, single, 0)
            lax.fori_loop(0, n_chunk, chunk, 0)
            lax.fori_loop(0, n_single, single_wait, 0)
            lax.fori_loop(0, n_chunk, chunk_wait, 0)
            return carry

        lax.fori_loop(0, N_EXPERTS, fill, 0)

    _index_copy(pos_hbm, idx_ref, isem, i, slot).wait()

    @pl.when(i + 1 < n)
    def _next():
        _index_copy(pos_hbm, idx_ref, isem, i + 1, 1 - slot).start()

    _store_token_tiles(xt_ref, x_ref[...])
    base = slot * (2 * td)

    def body(j, carry):
        for u in range(SUBLANES):
            t = j * SUBLANES + u
            src = _token_tile(xt_ref, t)
            for k in range(2):
                p = idx_ref[base + k * td + t]
                pltpu.make_async_copy(src, _token_tile(xs_hbm, p), rsem).start(priority=k)
        return carry

    lax.fori_loop(0, td // SUBLANES, body, 0)
    for _ in range(2):
        pltpu.make_async_copy(xt_ref, xs_hbm.at[pl.ds(0, td * SUBLANES)], rsem).wait()


def _dispatch(x1, pos_t, pad_lo, pad_hi, n_rows, td):
    n = x1.shape[0]
    kern = functools.partial(_dispatch_kernel, td=td)
    grid_spec = pltpu.PrefetchScalarGridSpec(
        num_scalar_prefetch=2,
        grid=(n // td,),
        in_specs=[pl.BlockSpec(memory_space=pl.ANY),
                  pl.BlockSpec((td, D_MODEL), lambda i, lo, hi: (i, 0))],
        out_specs=pl.BlockSpec(memory_space=pl.ANY),
        scratch_shapes=[pltpu.SMEM((4 * td,), jnp.int32),
                        pltpu.VMEM((td * SUBLANES, LANES), F32),
                        pltpu.VMEM((ZERO_CHUNK * SUBLANES, LANES), F32),
                        pltpu.SemaphoreType.DMA((2,)), pltpu.SemaphoreType.DMA(()),
                        pltpu.SemaphoreType.DMA(())],
    )
    return pl.pallas_call(
        kern,
        grid_spec=grid_spec,
        out_shape=jax.ShapeDtypeStruct((n_rows * SUBLANES, LANES), F32),
        compiler_params=_params(1),
        name="moe_dispatch",
    )(pad_lo, pad_hi, pos_t, x1)


def _gmm_kernel(te_ref, rows_ref, xs_ref, wg_ref, wu_ref, wd_ref, ys_ref):
    del te_ref
    nrows = rows_ref[pl.program_id(0)]

    @pl.when(nrows > 0)
    def _compute():
        x = _load_token_tiles(xs_ref, xs_ref.shape[0] // SUBLANES).astype(BF16)
        gate = _dot(x, wg_ref[...].astype(BF16))
        up = _dot(x, wu_ref[...].astype(BF16))
        h = (gate * (1.0 / (1.0 + jnp.exp(-gate))) * up).astype(BF16)
        _store_token_tiles(ys_ref, _dot(h, wd_ref[...].astype(BF16)))

    @pl.when(nrows == 0)
    def _empty():
        ys_ref[...] = jnp.zeros_like(ys_ref)


def _grouped_mlp(xs, tile_expert, tile_rows, wg, wu, wd, layer, tm):
    n_rows = xs.shape[0] // SUBLANES
    wspec = lambda shape: pl.BlockSpec((None, None) + shape, lambda g, te, tr: (layer, te[g], 0, 0))
    grid_spec = pltpu.PrefetchScalarGridSpec(
        num_scalar_prefetch=2,
        grid=(n_rows // tm,),
        in_specs=[pl.BlockSpec((tm * SUBLANES, LANES), lambda g, te, tr: (g, 0)),
                  wspec((D_MODEL, D_EXPERT)), wspec((D_MODEL, D_EXPERT)), wspec((D_EXPERT, D_MODEL))],
        out_specs=pl.BlockSpec((tm * SUBLANES, LANES), lambda g, te, tr: (g, 0)),
    )
    return pl.pallas_call(
        _gmm_kernel,
        grid_spec=grid_spec,
        out_shape=jax.ShapeDtypeStruct(xs.shape, F32),
        compiler_params=_params(1),
        name="moe_grouped_mlp",
    )(tile_expert, tile_rows, xs, wg, wu, wd)


def _combine_kernel(pos_hbm, ys_hbm, x1_ref, w_ref, g_ref, b_ref, o_ref, idx_ref, buf_ref, isem, rsem, *, td):
    i = pl.program_id(0)
    n = pl.num_programs(0)
    slot = lax.rem(i, 2)

    def issue_rows(sl):
        base = sl * (2 * td)

        def body(j, carry):
            for u in range(SUBLANES):
                t = j * SUBLANES + u
                for k in range(2):
                    p = idx_ref[base + k * td + t]
                    pltpu.make_async_copy(_token_tile(ys_hbm, p), _token_tile(buf_ref.at[sl, k], t),
                                          rsem.at[sl]).start(priority=k)
            return carry

        lax.fori_loop(0, td // SUBLANES, body, 0)

    @pl.when(i == 0)
    def _first():
        _index_copy(pos_hbm, idx_ref, isem, 0, 0).start()
        _index_copy(pos_hbm, idx_ref, isem, 0, 0).wait()
        issue_rows(0)

        @pl.when(n > 1)
        def _():
            _index_copy(pos_hbm, idx_ref, isem, 1, 1).start()

    @pl.when(i + 1 < n)
    def _next():
        _index_copy(pos_hbm, idx_ref, isem, i + 1, 1 - slot).wait()
        issue_rows(1 - slot)

        @pl.when(i + 2 < n)
        def _():
            _index_copy(pos_hbm, idx_ref, isem, i + 2, slot).start()

    for k in range(2):
        pltpu.make_async_copy(ys_hbm.at[pl.ds(0, td * SUBLANES)], buf_ref.at[slot, k], rsem.at[slot]).wait()
    w = w_ref[...]
    y = (w[:, 0:1] * _load_token_tiles(buf_ref.at[slot, 0], td)
         + w[:, 1:2] * _load_token_tiles(buf_ref.at[slot, 1], td))
    o_ref[...] = _layer_norm(ALPHA * x1_ref[...] + y, g_ref[...], b_ref[...])


def _combine(ys, pos_t, x1, wcol, g, b, td):
    n = x1.shape[0]
    row = lambda i: (i, 0)
    fix = lambda i: (0, 0)
    kern = functools.partial(_combine_kernel, td=td)
    return pl.pallas_call(
        kern,
        grid=(n // td,),
        in_specs=[pl.BlockSpec(memory_space=pl.ANY), pl.BlockSpec(memory_space=pl.ANY),
                  pl.BlockSpec((td, D_MODEL), row), pl.BlockSpec((td, LANES), row),
                  pl.BlockSpec((1, D_MODEL), fix), pl.BlockSpec((1, D_MODEL), fix)],
        out_specs=pl.BlockSpec((td, D_MODEL), row),
        out_shape=jax.ShapeDtypeStruct((n, D_MODEL), F32),
        scratch_shapes=[pltpu.SMEM((4 * td,), jnp.int32),
                        pltpu.VMEM((2, 2, td * SUBLANES, LANES), F32),
                        pltpu.SemaphoreType.DMA((2,)), pltpu.SemaphoreType.DMA((2,))],
        compiler_params=_params(1),
        name="moe_combine_ln2",
    )(pos_t, ys, x1, wcol, g, b)


def _moe_routed(x1, info, wcol, cnt, p):
    n = x1.shape[0]
    tm, td = MOE_ROW_TILE, MOE_TOKEN_TILE
    n_tiles = 2 * n // tm + N_EXPERTS
    counts = cnt[:, 0].astype(jnp.int32)
    padded = (counts + (tm - 1)) // tm * tm
    ends = jnp.cumsum(padded)
    offs = ends - padded
    starts = jnp.arange(n_tiles, dtype=jnp.int32) * tm
    te = jnp.sum((ends[None, :] <= starts[:, None]).astype(jnp.int32), axis=1)
    tile_expert = jnp.minimum(te, N_EXPERTS - 1)
    experts = jnp.arange(N_EXPERTS, dtype=jnp.int32)
    pick = tile_expert[:, None] == experts[None, :]
    seg_end = jnp.sum(jnp.where(pick, (offs + counts)[None, :], 0), axis=1)
    tile_rows = jnp.where(te < N_EXPERTS, jnp.clip(seg_end - starts, 0, tm), 0).astype(jnp.int32)

    def position(e, r):
        return r + jnp.sum(jnp.where(e[None, :] == experts[:, None], offs[:, None], 0), axis=0)

    pos1 = position(info[0], info[2]).reshape(n // td, td)
    pos2 = position(info[1], info[3]).reshape(n // td, td)
    pos_t = jnp.concatenate([pos1, pos2], axis=1)
    pad_lo = offs + counts
    pad_hi = ends.at[N_EXPERTS - 1].set(n_tiles * tm)
    xs = _dispatch(x1, pos_t, pad_lo, pad_hi, n_tiles * tm, td)
    ys = _grouped_mlp(xs, tile_expert, tile_rows, p["w_gate"], p["w_up"], p["w_down"], p["layer"], tm)
    return _combine(ys, pos_t, x1, wcol, p["ln2_g"], p["ln2_b"], td)


def _layer(x, mem, p):
    b, s, d = x.shape
    n = b * s
    xf = x.reshape(n, d)
    proj = _matmul(xf, p["w_in"], 512, "in_proj").reshape(b, s, IN_W)
    mkv = _matmul(mem.reshape(-1, d), p["w_mem_kv"], 512, "mem_kv").reshape(b, -1, 2 * M_W)
    oa = _window_attention(proj, p["sink"], p["g_a"])
    ob = _neighborhood_attention(proj, p["na_bias"], p["g_b"])
    om = _memory_attention(proj, mkv, p["g_m"])
    x1, info, wcol, cnt = _out_proj(oa.reshape(n, A_Q_W), ob.reshape(n, B_W), om.reshape(n, M_W), p["w_out"], xf,
                                    p["ln1_g"], p["ln1_b"], p["wr_cat"], p["rbias"])
    y = _moe_routed(x1, info, wcol, cnt, p)
    return y.reshape(b, s, d)


def kernel(x_prompt, x_sample, mem_prompt, mem_sample, w_in, w_mem_kv, sink_logits, rpb, grp_norm_g, w_out,
           ln1_g, ln1_b, w_router, router_bias, w_gate, w_up, w_down, ln2_g, ln2_b):
    seq = x_prompt.shape[1]
    assert x_sample.shape[1] == seq
    wr = jnp.pad(w_router.astype(F32), ((0, 0), (0, LANES - N_EXPERTS)))
    wr_hi = wr.astype(BF16)
    wr_cat = jnp.concatenate([wr_hi, (wr - wr_hi.astype(F32)).astype(BF16)], axis=1)
    q_scale = np.ones((IN_W,), np.float32)
    for c0, width in ((0, A_Q_W), (A_Q_W + 2 * LANES, B_W), (IN_W - M_W, M_W)):
        q_scale[c0:c0 + width] = SCALE
    rbias = router_bias.astype(F32).reshape(N_EXPERTS, 1)
    layers = []
    for l in range(DEPTH):
        g = grp_norm_g[l].astype(F32)
        layers.append(dict(
            w_in=(w_in[l] * q_scale).astype(BF16), w_mem_kv=w_mem_kv[l].astype(BF16),
            sink=sink_logits[l].astype(F32), na_bias=_na_bias(rpb[l], seq),
            g_a=g[:A_Q_W].reshape(1, -1), g_b=g[A_Q_W:A_Q_W + B_W].reshape(1, -1),
            g_m=g[A_Q_W + B_W:].reshape(1, -1),
            w_out=w_out[l].astype(BF16),
            ln1_g=ln1_g[l].astype(F32).reshape(1, -1), ln1_b=ln1_b[l].astype(F32).reshape(1, -1),
            wr_cat=wr_cat, rbias=rbias,
            w_gate=w_gate, w_up=w_up, w_down=w_down, layer=l,
            ln2_g=ln2_g[l].astype(F32).reshape(1, -1), ln2_b=ln2_b[l].astype(F32).reshape(1, -1)))

    def trunk(x, mem):
        for p in layers:
            x = _layer(x, mem, p)
        return x

    return (trunk(x_prompt, mem_prompt), trunk(x_sample, mem_sample))
```

```python
import functools

import jax
import jax.numpy as jnp
import numpy as np
from jax import lax
from jax.experimental import pallas as pl
from jax.experimental.pallas import tpu as pltpu

F32 = jnp.float32
BF16 = jnp.bfloat16

D_MODEL = 1024
DEPTH = 2
HEAD_DIM = 64
A_HEADS = 8
WINDOW = 128
B_HEADS = 4
GRID_W = 64
NA_ROWS = 8
NA_COLS = 16
M_HEADS = 4
N_EXPERTS = 16
N_GROUPS = 4
D_EXPERT = 512
LN_EPS = 1e-5
A_Q_W = 512
B_W = 256
M_W = 256
IN_W = 1792
ALPHA = (2 * DEPTH) ** 0.25
SCALE = HEAD_DIM ** -0.5
LOG2E = 1.4426950408889634
NEG = -1e30

LANES = 128
SOFTMAX_HEADS = 2
NA_TILE_ROWS = 2
NA_STEP_TILES = 4
NA_KEY_ROWS = 10
VMEM_LIMIT = 56 * 1024 * 1024


def _params(n_axes, vmem=VMEM_LIMIT):
    return pltpu.CompilerParams(dimension_semantics=("arbitrary",) * n_axes, vmem_limit_bytes=vmem)


def _dot_nt(a, b):
    return lax.dot_general(a, b, (((1,), (1,)), ((), ())), preferred_element_type=F32)


def _dot(a, b):
    return jnp.dot(a, b, preferred_element_type=F32)


def _lo_mask():
    return lax.broadcasted_iota(jnp.int32, (1, LANES), 1) < HEAD_DIM


def _mm_kernel(x_ref, w_ref, o_ref):
    o_ref[...] = _dot(x_ref[...].astype(BF16), w_ref[...]).astype(o_ref.dtype)


def _matmul(x, w, tm, name):
    n, k = x.shape
    m = w.shape[1]
    tm = min(tm, n)
    assert n % tm == 0
    return pl.pallas_call(
        _mm_kernel,
        grid=(n // tm,),
        in_specs=[pl.BlockSpec((tm, k), lambda i: (i, 0)), pl.BlockSpec((k, m), lambda i: (0, 0))],
        out_specs=pl.BlockSpec((tm, m), lambda i: (i, 0)),
        out_shape=jax.ShapeDtypeStruct((n, m), BF16),
        compiler_params=_params(1),
        name=name,
    )(x, w)


def _group_rms(of, g):
    ms = jnp.mean(of * of, axis=-1, keepdims=True)
    return of * lax.rsqrt(ms + LN_EPS) * g


def _win_kernel(sink_ref, q_ref, k_ref, v_ref, g_ref, o_ref, ks_ref, vs_ref, bias_ref, s_ref, p_ref, d_ref, *,
                seq, tq):
    i = pl.program_id(1)
    lo = _lo_mask()
    chunk = 512

    @pl.when(i == 0)
    def _build():
        zeros = jnp.zeros((WINDOW, LANES), BF16)
        for t in range(4):
            for dst in (ks_ref, vs_ref):
                dst[t, 0:WINDOW, :] = zeros
                dst[t, seq + WINDOW:seq + 2 * WINDOW, :] = zeros

        def body(c, carry):
            r0 = pl.multiple_of(c * chunk, chunk)
            for src, dst in ((k_ref, ks_ref), (v_ref, vs_ref)):
                x = src[pl.ds(r0, chunk), :].astype(F32)
                xr = pltpu.roll(x, HEAD_DIM, 1)
                rows = pl.ds(r0 + WINDOW, chunk)
                dst[0, rows, :] = jnp.where(lo, x, 0.0).astype(BF16)
                dst[1, rows, :] = jnp.where(lo, 0.0, xr).astype(BF16)
                dst[2, rows, :] = jnp.where(lo, xr, 0.0).astype(BF16)
                dst[3, rows, :] = jnp.where(lo, 0.0, x).astype(BF16)
            return carry

        lax.fori_loop(0, seq // chunk, body, 0)

    kw = 3 * WINDOW

    @pl.when(jnp.logical_and(pl.program_id(0) == 0, i == 0))
    def _build_bias():
        a_i = lax.broadcasted_iota(jnp.int32, (WINDOW, kw), 0)
        j_i = lax.broadcasted_iota(jnp.int32, (WINDOW, kw), 1)
        dist = jnp.abs(a_i + WINDOW - j_i)
        absd = dist.astype(F32)
        for variant in range(3):
            valid = dist <= WINDOW
            if variant == 0:
                valid = valid & (j_i >= WINDOW)
            if variant == 2:
                valid = valid & (j_i < 2 * WINDOW)
            for head in range(A_HEADS):
                slope = 2.0 ** (-8.0 * (head + 1) / A_HEADS)
                bias_ref[variant * A_HEADS + head] = jnp.where(valid, (-slope * LOG2E) * absd, NEG)

    g = g_ref[...]

    nblk = tq // WINDOW
    wins, variants = [], []
    for j in range(nblk):
        qs = pl.multiple_of(i * tq + j * WINDOW, WINDOW)
        wins.append(pl.ds(qs, kw))
        variants.append(jnp.where(qs == 0, 0, jnp.where(qs == seq - WINDOW, 2, 1)))
    for j in range(nblk):
        for head in range(A_HEADS):
            c, t = divmod(head, 2)
            q2 = q_ref[j * WINDOW:(j + 1) * WINDOW, c * LANES:(c + 1) * LANES]
            s_ref[j * A_HEADS + head] = (_dot_nt(q2, ks_ref[2 * (c // 2) + t, wins[j], :])
                                         + bias_ref[variants[j] * A_HEADS + head])
    for j in range(nblk):
        for head in range(A_HEADS):
            s = s_ref[j * A_HEADS + head]
            sk = sink_ref[head] * LOG2E
            m = jnp.maximum(jnp.max(s, axis=-1, keepdims=True), sk)
            p = jnp.exp2(s - m)
            den = jnp.sum(p, axis=-1, keepdims=True) + jnp.exp2(sk - m)
            p_ref[j * A_HEADS + head] = p.astype(BF16)
            d_ref[j * A_HEADS + head] = jnp.broadcast_to(den, (WINDOW, LANES))
    for j in range(nblk):
        outs = []
        for c in range(4):
            h = c // 2
            u = j * A_HEADS + 2 * c
            num = (_dot(p_ref[u], vs_ref[2 * h, wins[j], :])
                   + _dot(p_ref[u + 1], vs_ref[2 * h + 1, wins[j], :]))
            outs.append(num / jnp.where(lo, d_ref[u], d_ref[u + 1]))
        of = jnp.concatenate(outs, axis=1)
        o_ref[j * WINDOW:(j + 1) * WINDOW, :] = _group_rms(of, g).astype(o_ref.dtype)


def _window_attention(proj, sink, g, tq=512):
    b, s, _ = proj.shape
    assert s % tq == 0 and s >= 2 * WINDOW
    kern = functools.partial(_win_kernel, seq=s, tq=tq)
    return pl.pallas_call(
        kern,
        grid=(b, s // tq),
        in_specs=[
            pl.BlockSpec(memory_space=pltpu.SMEM),
            pl.BlockSpec((None, tq, A_Q_W), lambda bi, i: (bi, i, 0)),
            pl.BlockSpec((None, s, LANES), lambda bi, i: (bi, 0, 4)),
            pl.BlockSpec((None, s, LANES), lambda bi, i: (bi, 0, 5)),
            pl.BlockSpec((1, A_Q_W), lambda bi, i: (0, 0)),
        ],
        out_specs=pl.BlockSpec((None, tq, A_Q_W), lambda bi, i: (bi, i, 0)),
        out_shape=jax.ShapeDtypeStruct((b, s, A_Q_W), BF16),
        scratch_shapes=[pltpu.VMEM((4, s + 2 * WINDOW, LANES), BF16),
                        pltpu.VMEM((4, s + 2 * WINDOW, LANES), BF16),
                        pltpu.VMEM((3 * A_HEADS, WINDOW, 3 * WINDOW), F32),
                        pltpu.VMEM((tq // WINDOW * A_HEADS, WINDOW, 3 * WINDOW), F32),
                        pltpu.VMEM((tq // WINDOW * A_HEADS, WINDOW, 3 * WINDOW), BF16),
                        pltpu.VMEM((tq // WINDOW * A_HEADS, WINDOW, LANES), F32)],
        compiler_params=_params(2),
        name="window_attention",
    )(sink, proj, proj, proj, g)


def _na_tables(seq):
    rows = seq // GRID_W
    kh = min(NA_ROWS, rows)
    assert rows % NA_TILE_ROWS == 0 and rows >= NA_KEY_ROWS and NA_KEY_ROWS % 2 == 0
    nt = rows // NA_TILE_ROWS
    u = np.arange(NA_TILE_ROWS * GRID_W) // GRID_W
    c = np.arange(NA_TILE_ROWS * GRID_W) % GRID_W
    ki = np.arange(NA_KEY_ROWS * GRID_W) // GRID_W
    kc = np.arange(NA_KEY_ROWS * GRID_W) % GRID_W
    cs = np.clip(c - NA_COLS // 2, 0, GRID_W - NA_COLS)
    col_ok = (kc[None, :] >= cs[:, None]) & (kc[None, :] < cs[:, None] + NA_COLS)
    dc = np.clip(kc[None, :] - c[:, None] + NA_COLS - 1, 0, 2 * NA_COLS - 2)
    types, type_of, kstart = [], [], []
    for t in range(nt):
        r0 = t * NA_TILE_ROWS
        k0 = int(np.clip(r0 - kh // 2, 0, rows - NA_KEY_ROWS))
        k0 -= k0 % 2
        r = r0 + u
        rs = np.clip(r - kh // 2, 0, rows - kh)
        kr = k0 + ki
        row_ok = (kr[None, :] >= rs[:, None]) & (kr[None, :] < rs[:, None] + kh)
        assert (rs >= k0).all() and (rs + kh <= k0 + NA_KEY_ROWS).all()
        dr = np.clip(kr[None, :] - r[:, None] + NA_ROWS - 1, 0, 2 * NA_ROWS - 2)
        key = (dr.tobytes(), row_ok.tobytes())
        for idx, (k_, _, _) in enumerate(types):
            if k_ == key:
                type_of.append(idx)
                break
        else:
            type_of.append(len(types))
            types.append((key, dr, row_ok & col_ok))
        kstart.append(k0)
    dr_rows = np.stack([t_[1][::GRID_W, ::GRID_W] for t_ in types])
    ok_all = np.stack([t_[2] for t_ in types])
    dc_cols = dc[:GRID_W, :GRID_W]
    return (np.asarray(type_of, np.int32), np.asarray(kstart, np.int32), dr_rows, dc_cols, ok_all)


def _na_bias(rpb, seq):
    _, _, dr_rows, dc_cols, ok = _na_tables(seq)
    nt = dr_rows.shape[0]
    hi = lax.Precision.HIGHEST
    col_sel = jnp.asarray(np.eye(2 * NA_COLS - 1, dtype=np.float32)[dc_cols])
    row_sel = jnp.asarray(np.eye(2 * NA_ROWS - 1, dtype=np.float32)[dr_rows])
    by_col = jnp.einsum("hdj,ckj->hdck", rpb.astype(F32), col_sel, precision=hi)
    bias = jnp.einsum("tuid,hdck->thucik", row_sel, by_col, precision=hi)
    bias = bias.reshape(nt, B_HEADS, NA_TILE_ROWS * GRID_W, NA_KEY_ROWS * GRID_W)
    return jnp.where(ok[:, None], bias * LOG2E, NEG)


def _na_kernel(type_ref, kstart_ref, q_ref, k_ref, v_ref, *rest, seq):
    del type_ref
    bias_refs = rest[:NA_STEP_TILES]
    g_ref, o_ref, ks_ref, vs_ref, s_ref, p_ref = rest[NA_STEP_TILES:]
    i = pl.program_id(1)
    lo = _lo_mask()
    chunk = 512

    @pl.when(i == 0)
    def _build():
        def body(c, carry):
            rows = pl.ds(pl.multiple_of(c * chunk, chunk), chunk)
            for src, dst in ((k_ref, ks_ref), (v_ref, vs_ref)):
                for p in range(2):
                    x = src[rows, p * LANES:(p + 1) * LANES]
                    dst[2 * p, rows, :] = jnp.where(lo, x, jnp.zeros_like(x))
                    dst[2 * p + 1, rows, :] = jnp.where(lo, jnp.zeros_like(x), x)
            return carry

        lax.fori_loop(0, seq // chunk, body, 0)

    nk = NA_KEY_ROWS * GRID_W
    tq = NA_TILE_ROWS * GRID_W
    units = [(t, head) for t in range(NA_STEP_TILES) for head in range(B_HEADS)]
    wins = [pl.ds(pl.multiple_of(kstart_ref[i * NA_STEP_TILES + t] * GRID_W, LANES), nk)
            for t in range(NA_STEP_TILES)]
    for u, (t, head) in enumerate(units):
        q2 = q_ref[t * tq:(t + 1) * tq, (head // 2) * LANES:(head // 2 + 1) * LANES]
        s_ref[u] = _dot_nt(q2, ks_ref[head, wins[t], :]) + bias_refs[t][head]
    invs = []
    for u in range(len(units)):
        s = s_ref[u]
        m = jnp.max(s, axis=-1, keepdims=True)
        e = jnp.exp2(s - m)
        invs.append(1.0 / jnp.sum(e, axis=-1, keepdims=True))
        p_ref[u] = e.astype(BF16)
    g = g_ref[...]
    for t in range(NA_STEP_TILES):
        outs = []
        for p in range(2):
            u = t * B_HEADS + 2 * p
            num = _dot(p_ref[u], vs_ref[2 * p, wins[t], :]) + _dot(p_ref[u + 1], vs_ref[2 * p + 1, wins[t], :])
            outs.append(num * jnp.where(lo, invs[u], invs[u + 1]))
        of = jnp.concatenate(outs, axis=1)
        o_ref[t * tq:(t + 1) * tq, :] = _group_rms(of, g).astype(o_ref.dtype)


def _neighborhood_attention(proj, bias, g):
    b, s, _ = proj.shape
    type_of, kstart, _, _, _ = _na_tables(s)
    tq = NA_TILE_ROWS * GRID_W
    nk = NA_KEY_ROWS * GRID_W
    step_q = NA_STEP_TILES * tq
    assert s % step_q == 0
    kern = functools.partial(_na_kernel, seq=s)

    def bias_spec(t):
        return pl.BlockSpec((None, B_HEADS, tq, nk), lambda bi, i, ty, ks: (ty[i * NA_STEP_TILES + t], 0, 0, 0))

    grid_spec = pltpu.PrefetchScalarGridSpec(
        num_scalar_prefetch=2,
        grid=(b, s // step_q),
        in_specs=[
            pl.BlockSpec((None, step_q, B_W), lambda bi, i, ty, ks: (bi, i, 3)),
            pl.BlockSpec((None, s, B_W), lambda bi, i, ty, ks: (bi, 0, 4)),
            pl.BlockSpec((None, s, B_W), lambda bi, i, ty, ks: (bi, 0, 5)),
            *[bias_spec(t) for t in range(NA_STEP_TILES)],
            pl.BlockSpec((1, B_W), lambda bi, i, ty, ks: (0, 0)),
        ],
        out_specs=pl.BlockSpec((None, step_q, B_W), lambda bi, i, ty, ks: (bi, i, 0)),
        scratch_shapes=[pltpu.VMEM((4, s, LANES), BF16), pltpu.VMEM((4, s, LANES), BF16),
                        pltpu.VMEM((NA_STEP_TILES * B_HEADS, tq, nk), F32),
                        pltpu.VMEM((NA_STEP_TILES * B_HEADS, tq, nk), BF16)],
    )
    return pl.pallas_call(
        kern,
        grid_spec=grid_spec,
        out_shape=jax.ShapeDtypeStruct((b, s, B_W), BF16),
        compiler_params=_params(2),
        name="neighborhood_attention",
    )(jnp.asarray(type_of), jnp.asarray(kstart), proj, proj, proj, *([bias] * NA_STEP_TILES), g)


def _mem_kernel(q_ref, mkv_ref, g_ref, o_ref):
    lo = _lo_mask()
    outs = []
    for p in range(2):
        q2 = q_ref[:, p * LANES:(p + 1) * LANES]
        mk = mkv_ref[:, p * LANES:(p + 1) * LANES]
        mv = mkv_ref[:, M_W + p * LANES:M_W + (p + 1) * LANES]
        zero = jnp.zeros_like(mk)
        num = None
        invs = []
        for t in range(2):
            keep = lo if t == 0 else jnp.logical_not(lo)
            s = _dot_nt(q2, jnp.where(keep, mk, zero))
            m = jnp.max(s, axis=-1, keepdims=True)
            e = jnp.exp2(s - m)
            den = jnp.sum(e, axis=-1, keepdims=True)
            pv = _dot(e.astype(BF16), jnp.where(keep, mv, zero))
            num = pv if num is None else num + pv
            invs.append(1.0 / den)
        outs.append(num * jnp.where(lo, invs[0], invs[1]))
    of = jnp.concatenate(outs, axis=1)
    o_ref[...] = _group_rms(of, g_ref[...]).astype(o_ref.dtype)


def _memory_attention(proj, mkv, g, tq=512):
    b, s, _ = proj.shape
    mlen = mkv.shape[1]
    return pl.pallas_call(
        _mem_kernel,
        grid=(b, s // tq),
        in_specs=[
            pl.BlockSpec((None, tq, M_W), lambda bi, i: (bi, i, 6)),
            pl.BlockSpec((None, mlen, 2 * M_W), lambda bi, i: (bi, 0, 0)),
            pl.BlockSpec((1, M_W), lambda bi, i: (0, 0)),
        ],
        out_specs=pl.BlockSpec((None, tq, M_W), lambda bi, i: (bi, i, 0)),
        out_shape=jax.ShapeDtypeStruct((b, s, M_W), BF16),
        compiler_params=_params(2),
        name="memory_attention",
    )(proj, mkv, g)


def _layer_norm(z, g, b):
    mu = jnp.mean(z, axis=-1, keepdims=True)
    zc = z - mu
    var = jnp.mean(zc * zc, axis=-1, keepdims=True)
    return zc * lax.rsqrt(var + LN_EPS) * g + b


def _route(lg_t, rbias):
    e, t = lg_t.shape
    eg = e // N_GROUPS
    scores = 1.0 / (1.0 + jnp.exp(-lg_t))
    sel = scores + rbias
    row = lax.broadcasted_iota(jnp.int32, (e, t), 0)
    best = None
    gidx = None
    for gi in range(N_GROUPS):
        r = [sel[gi * eg + k:gi * eg + k + 1, :] for k in range(eg)]
        top2 = None
        for a in range(eg):
            for b in range(a + 1, eg):
                pair = r[a] + r[b]
                top2 = pair if top2 is None else jnp.maximum(top2, pair)
        if best is None:
            best, gidx = top2, jnp.zeros((1, t), jnp.int32)
        else:
            better = top2 > best
            gidx = jnp.where(better, gi, gidx)
            best = jnp.maximum(best, top2)
    in_group = (row // eg) == gidx
    masked = jnp.where(in_group, sel, -jnp.inf)
    m1 = jnp.max(masked, axis=0, keepdims=True)
    i1 = jnp.min(jnp.where(masked == m1, row, e), axis=0, keepdims=True)
    rest = jnp.where(row == i1, -jnp.inf, masked)
    m2 = jnp.max(rest, axis=0, keepdims=True)
    i2 = jnp.min(jnp.where(rest == m2, row, e), axis=0, keepdims=True)
    pick1 = row == i1
    pick2 = row == i2
    w1 = jnp.sum(jnp.where(pick1, scores, 0.0), axis=0, keepdims=True)
    w2 = jnp.sum(jnp.where(pick2, scores, 0.0), axis=0, keepdims=True)
    tot = w1 + w2
    return i1, i2, pick1, pick2, w1 / tot, w2 / tot


def _out_kernel(ma_ref, mb_ref, mm_ref, w_ref, x_ref, g_ref, b_ref, wr_ref, rb_ref, tri_ref,
                x1_ref, info_ref, wcol_ref, cnt_ref, carry_ref):
    @pl.when(pl.program_id(0) == 0)
    def _init():
        carry_ref[...] = jnp.zeros_like(carry_ref)

    y = _dot(ma_ref[...], w_ref[0:A_Q_W, :])
    y = y + _dot(mb_ref[...], w_ref[A_Q_W:A_Q_W + B_W, :])
    y = y + _dot(mm_ref[...], w_ref[A_Q_W + B_W:, :])
    x1 = _layer_norm(ALPHA * x_ref[...] + y, g_ref[...], b_ref[...])
    x1_ref[...] = x1
    x_hi = x1.astype(BF16)
    x_lo = (x1 - x_hi.astype(F32)).astype(BF16)
    hi2 = _dot(x_hi, wr_ref[...])
    lg = hi2[:, 0:LANES] + hi2[:, LANES:] + _dot(x_lo, wr_ref[:, 0:LANES])
    lg_t = lg.T[0:N_EXPERTS, :]
    i1, i2, pick1, pick2, w1, w2 = _route(lg_t, rb_ref[...])
    tm = lg_t.shape[1]
    member = jnp.where(jnp.logical_or(pick1, pick2), 1.0, 0.0)
    before = carry_ref[:, 0:1] + _dot(member.astype(BF16), tri_ref[...])
    r1 = jnp.sum(jnp.where(pick1, before, 0.0), axis=0, keepdims=True)
    r2 = jnp.sum(jnp.where(pick2, before, 0.0), axis=0, keepdims=True)
    total = carry_ref[...] + jnp.sum(member, axis=1, keepdims=True)
    carry_ref[...] = total
    cnt_ref[...] = total
    info_ref[...] = jnp.concatenate(
        [i1, i2, r1.astype(jnp.int32), r2.astype(jnp.int32), jnp.zeros((4, tm), jnp.int32)], axis=0)
    w_pad = jnp.concatenate([w1, w2, jnp.zeros((LANES - 2, tm), F32)], axis=0)
    wcol_ref[...] = w_pad.T


def _out_proj(ma, mb, mm, w_out, x, g, b, wr_cat, rbias, tm=1024):
    n = x.shape[0]
    row = lambda i: (i, 0)
    fix = lambda i: (0, 0)
    tri = jnp.asarray(np.triu(np.ones((tm, tm), np.float32), k=1), BF16)
    return pl.pallas_call(
        _out_kernel,
        grid=(n // tm,),
        in_specs=[
            pl.BlockSpec((tm, A_Q_W), row), pl.BlockSpec((tm, B_W), row), pl.BlockSpec((tm, M_W), row),
            pl.BlockSpec((D_MODEL, D_MODEL), fix), pl.BlockSpec((tm, D_MODEL), row),
            pl.BlockSpec((1, D_MODEL), fix), pl.BlockSpec((1, D_MODEL), fix),
            pl.BlockSpec((D_MODEL, 2 * LANES), fix), pl.BlockSpec((N_EXPERTS, 1), fix),
            pl.BlockSpec((tm, tm), fix),
        ],
        out_specs=[pl.BlockSpec((tm, D_MODEL), row), pl.BlockSpec((8, tm), lambda i: (0, i)),
                   pl.BlockSpec((tm, LANES), row), pl.BlockSpec((N_EXPERTS, LANES), fix)],
        out_shape=[jax.ShapeDtypeStruct((n, D_MODEL), F32), jax.ShapeDtypeStruct((8, n), jnp.int32),
                   jax.ShapeDtypeStruct((n, LANES), F32), jax.ShapeDtypeStruct((N_EXPERTS, LANES), F32)],
        scratch_shapes=[pltpu.VMEM((N_EXPERTS, LANES), F32)],
        compiler_params=_params(1),
        name="out_proj_ln_router",
    )(ma, mb, mm, w_out, x, g, b, wr_cat, rbias, tri)


MOE_ROW_TILE = 512
MOE_TOKEN_TILE = 512


SUBLANES = 8


def _index_copy(pos_hbm, idx_ref, isem, step, slot):
    width = pos_hbm.shape[1]
    dst = idx_ref.at[pl.ds(pl.multiple_of(slot * width, width), width)]
    return pltpu.make_async_copy(pos_hbm.at[step], dst, isem.at[slot])


assert D_MODEL == SUBLANES * LANES
ZERO_CHUNK = 64


def _token_tile(ref, p):
    return ref.at[pl.ds(pl.multiple_of(p * SUBLANES, SUBLANES), SUBLANES)]


def _store_token_tiles(dst_ref, x):
    t = x.shape[0]
    for c in range(SUBLANES):
        dst_ref[pl.ds(c, t, stride=SUBLANES), :] = x[:, c * LANES:(c + 1) * LANES]


def _load_token_tiles(src_ref, t):
    return jnp.concatenate([src_ref[pl.ds(c, t, stride=SUBLANES), :] for c in range(SUBLANES)], axis=1)


def _dispatch_kernel(lo_ref, hi_ref, pos_hbm, x_ref, xs_hbm, idx_ref, xt_ref, z_ref, isem, rsem, zsem, *, td):
    i = pl.program_id(0)
    n = pl.num_programs(0)
    slot = lax.rem(i, 2)

    @pl.when(i == 0)
    def _first():
        _index_copy(pos_hbm, idx_ref, isem, 0, 0).start()
        z_ref[...] = jnp.zeros_like(z_ref)
        one = z_ref.at[pl.ds(0, SUBLANES)]

        def fill(e, carry):
            a = lo_ref[e]
            b = hi_ref[e]
            n_single = jnp.minimum(b - a, jnp.bitwise_and(-a, ZERO_CHUNK - 1))
            a2 = a + n_single
            n_chunk = lax.shift_right_logical(b - a2, 6)

            def single(r, c):
                pltpu.make_async_copy(one, _token_tile(xs_hbm, a + r), zsem).start()
                return c

            def chunk(r, c):
                dst = xs_hbm.at[pl.ds(pl.multiple_of((a2 + r * ZERO_CHUNK) * SUBLANES, SUBLANES),
                                      ZERO_CHUNK * SUBLANES)]
                pltpu.make_async_copy(z_ref, dst, zsem).start()
                return c

            def single_wait(r, c):
                pltpu.make_async_copy(one, _token_tile(xs_hbm, 0), zsem).wait()
                return c

            def chunk_wait(r, c):
                pltpu.make_async_copy(z_ref, xs_hbm.at[pl.ds(0, ZERO_CHUNK * SUBLANES)], zsem).wait()
                return c

            lax.fori_loop(0, n_single, single, 0)
            lax.fori_loop(0, n_chunk, chunk, 0)
            lax.fori_loop(0, n_single, single_wait, 0)
            lax.fori_loop(0, n_chunk, chunk_wait, 0)
            return carry

        lax.fori_loop(0, N_EXPERTS, fill, 0)

    _index_copy(pos_hbm, idx_ref, isem, i, slot).wait()

    @pl.when(i + 1 < n)
    def _next():
        _index_copy(pos_hbm, idx_ref, isem, i + 1, 1 - slot).start()

    _store_token_tiles(xt_ref, x_ref[...])
    base = slot * (2 * td)

    def body(j, carry):
        for u in range(SUBLANES):
            t = j * SUBLANES + u
            src = _token_tile(xt_ref, t)
            for k in range(2):
                p = idx_ref[base + k * td + t]
                pltpu.make_async_copy(src, _token_tile(xs_hbm, p), rsem).start(priority=k)
        return carry

    lax.fori_loop(0, td // SUBLANES, body, 0)
    for _ in range(2):
        pltpu.make_async_copy(xt_ref, xs_hbm.at[pl.ds(0, td * SUBLANES)], rsem).wait()


def _dispatch(x1, pos_t, pad_lo, pad_hi, n_rows, td):
    n = x1.shape[0]
    kern = functools.partial(_dispatch_kernel, td=td)
    grid_spec = pltpu.PrefetchScalarGridSpec(
        num_scalar_prefetch=2,
        grid=(n // td,),
        in_specs=[pl.BlockSpec(memory_space=pl.ANY),
                  pl.BlockSpec((td, D_MODEL), lambda i, lo, hi: (i, 0))],
        out_specs=pl.BlockSpec(memory_space=pl.ANY),
        scratch_shapes=[pltpu.SMEM((4 * td,), jnp.int32),
                        pltpu.VMEM((td * SUBLANES, LANES), F32),
                        pltpu.VMEM((ZERO_CHUNK * SUBLANES, LANES), F32),
                        pltpu.SemaphoreType.DMA((2,)), pltpu.SemaphoreType.DMA(()),
                        pltpu.SemaphoreType.DMA(())],
    )
    return pl.pallas_call(
        kern,
        grid_spec=grid_spec,
        out_shape=jax.ShapeDtypeStruct((n_rows * SUBLANES, LANES), F32),
        compiler_params=_params(1),
        name="moe_dispatch",
    )(pad_lo, pad_hi, pos_t, x1)


def _gmm_kernel(te_ref, rows_ref, xs_ref, wg_ref, wu_ref, wd_ref, ys_ref):
    del te_ref
    nrows = rows_ref[pl.program_id(0)]

    @pl.when(nrows > 0)
    def _compute():
        x = _load_token_tiles(xs_ref, xs_ref.shape[0] // SUBLANES).astype(BF16)
        gate = _dot(x, wg_ref[...].astype(BF16))
        up = _dot(x, wu_ref[...].astype(BF16))
        h = (gate * (1.0 / (1.0 + jnp.exp(-gate))) * up).astype(BF16)
        _store_token_tiles(ys_ref, _dot(h, wd_ref[...].astype(BF16)))

    @pl.when(nrows == 0)
    def _empty():
        ys_ref[...] = jnp.zeros_like(ys_ref)


def _grouped_mlp(xs, tile_expert, tile_rows, wg, wu, wd, layer, tm):
    n_rows = xs.shape[0] // SUBLANES
    wspec = lambda shape: pl.BlockSpec((None, None) + shape, lambda g, te, tr: (layer, te[g], 0, 0))
    grid_spec = pltpu.PrefetchScalarGridSpec(
        num_scalar_prefetch=2,
        grid=(n_rows // tm,),
        in_specs=[pl.BlockSpec((tm * SUBLANES, LANES), lambda g, te, tr: (g, 0)),
                  wspec((D_MODEL, D_EXPERT)), wspec((D_MODEL, D_EXPERT)), wspec((D_EXPERT, D_MODEL))],
        out_specs=pl.BlockSpec((tm * SUBLANES, LANES), lambda g, te, tr: (g, 0)),
    )
    return pl.pallas_call(
        _gmm_kernel,
        grid_spec=grid_spec,
        out_shape=jax.ShapeDtypeStruct(xs.shape, F32),
        compiler_params=_params(1),
        name="moe_grouped_mlp",
    )(tile_expert, tile_rows, xs, wg, wu, wd)


def _combine_kernel(pos_hbm, ys_hbm, x1_ref, w_ref, g_ref, b_ref, o_ref, idx_ref, buf_ref, isem, rsem, *, td):
    i = pl.program_id(0)
    n = pl.num_programs(0)
    slot = lax.rem(i, 2)

    def issue_rows(sl):
        base = sl * (2 * td)

        def body(j, carry):
            for u in range(SUBLANES):
                t = j * SUBLANES + u
                for k in range(2):
                    p = idx_ref[base + k * td + t]
                    pltpu.make_async_copy(_token_tile(ys_hbm, p), _token_tile(buf_ref.at[sl, k], t),
                                          rsem.at[sl]).start(priority=k)
            return carry

        lax.fori_loop(0, td // SUBLANES, body, 0)

    @pl.when(i == 0)
    def _first():
        _index_copy(pos_hbm, idx_ref, isem, 0, 0).start()
        _index_copy(pos_hbm, idx_ref, isem, 0, 0).wait()
        issue_rows(0)

        @pl.when(n > 1)
        def _():
            _index_copy(pos_hbm, idx_ref, isem, 1, 1).start()

    @pl.when(i + 1 < n)
    def _next():
        _index_copy(pos_hbm, idx_ref, isem, i + 1, 1 - slot).wait()
        issue_rows(1 - slot)

        @pl.when(i + 2 < n)
        def _():
            _index_copy(pos_hbm, idx_ref, isem, i + 2, slot).start()

    for k in range(2):
        pltpu.make_async_copy(ys_hbm.at[pl.ds(0, td * SUBLANES)], buf_ref.at[slot, k], rsem.at[slot]).wait()
    w = w_ref[...]
    y = (w[:, 0:1] * _load_token_tiles(buf_ref.at[slot, 0], td)
         + w[:, 1:2] * _load_token_tiles(buf_ref.at[slot, 1], td))
    o_ref[...] = _layer_norm(ALPHA * x1_ref[...] + y, g_ref[...], b_ref[...])


def _combine(ys, pos_t, x1, wcol, g, b, td):
    n = x1.shape[0]
    row = lambda i: (i, 0)
    fix = lambda i: (0, 0)
    kern = functools.partial(_combine_kernel, td=td)
    return pl.pallas_call(
        kern,
        grid=(n // td,),
        in_specs=[pl.BlockSpec(memory_space=pl.ANY), pl.BlockSpec(memory_space=pl.ANY),
                  pl.BlockSpec((td, D_MODEL), row), pl.BlockSpec((td, LANES), row),
                  pl.BlockSpec((1, D_MODEL), fix), pl.BlockSpec((1, D_MODEL), fix)],
        out_specs=pl.BlockSpec((td, D_MODEL), row),
        out_shape=jax.ShapeDtypeStruct((n, D_MODEL), F32),
        scratch_shapes=[pltpu.SMEM((4 * td,), jnp.int32),
                        pltpu.VMEM((2, 2, td * SUBLANES, LANES), F32),
                        pltpu.SemaphoreType.DMA((2,)), pltpu.SemaphoreType.DMA((2,))],
        compiler_params=_params(1),
        name="moe_combine_ln2",
    )(pos_t, ys, x1, wcol, g, b)


def _moe_routed(x1, info, wcol, cnt, p):
    n = x1.shape[0]
    tm, td = MOE_ROW_TILE, MOE_TOKEN_TILE
    n_tiles = 2 * n // tm + N_EXPERTS
    counts = cnt[:, 0].astype(jnp.int32)
    padded = (counts + (tm - 1)) // tm * tm
    ends = jnp.cumsum(padded)
    offs = ends - padded
    starts = jnp.arange(n_tiles, dtype=jnp.int32) * tm
    te = jnp.sum((ends[None, :] <= starts[:, None]).astype(jnp.int32), axis=1)
    tile_expert = jnp.minimum(te, N_EXPERTS - 1)
    experts = jnp.arange(N_EXPERTS, dtype=jnp.int32)
    pick = tile_expert[:, None] == experts[None, :]
    seg_end = jnp.sum(jnp.where(pick, (offs + counts)[None, :], 0), axis=1)
    tile_rows = jnp.where(te < N_EXPERTS, jnp.clip(seg_end - starts, 0, tm), 0).astype(jnp.int32)

    def position(e, r):
        return r + jnp.sum(jnp.where(e[None, :] == experts[:, None], offs[:, None], 0), axis=0)

    pos1 = position(info[0], info[2]).reshape(n // td, td)
    pos2 = position(info[1], info[3]).reshape(n // td, td)
    pos_t = jnp.concatenate([pos1, pos2], axis=1)
    pad_lo = offs + counts
    pad_hi = ends.at[N_EXPERTS - 1].set(n_tiles * tm)
    xs = _dispatch(x1, pos_t, pad_lo, pad_hi, n_tiles * tm, td)
    ys = _grouped_mlp(xs, tile_expert, tile_rows, p["w_gate"], p["w_up"], p["w_down"], p["layer"], tm)
    return _combine(ys, pos_t, x1, wcol, p["ln2_g"], p["ln2_b"], td)


def _layer(x, mem, p):
    b, s, d = x.shape
    n = b * s
    xf = x.reshape(n, d)
    proj = _matmul(xf, p["w_in"], 512, "in_proj").reshape(b, s, IN_W)
    mkv = _matmul(mem.reshape(-1, d), p["w_mem_kv"], 512, "mem_kv").reshape(b, -1, 2 * M_W)
    oa = _window_attention(proj, p["sink"], p["g_a"])
    ob = _neighborhood_attention(proj, p["na_bias"], p["g_b"])
    om = _memory_attention(proj, mkv, p["g_m"])
    x1, info, wcol, cnt = _out_proj(oa.reshape(n, A_Q_W), ob.reshape(n, B_W), om.reshape(n, M_W), p["w_out"], xf,
                                    p["ln1_g"], p["ln1_b"], p["wr_cat"], p["rbias"])
    y = _moe_routed(x1, info, wcol, cnt, p)
    return y.reshape(b, s, d)


def kernel(x_prompt, x_sample, mem_prompt, mem_sample, w_in, w_mem_kv, sink_logits, rpb, grp_norm_g, w_out,
           ln1_g, ln1_b, w_router, router_bias, w_gate, w_up, w_down, ln2_g, ln2_b):
    seq = x_prompt.shape[1]
    assert x_sample.shape[1] == seq
    wr = jnp.pad(w_router.astype(F32), ((0, 0), (0, LANES - N_EXPERTS)))
    wr_hi = wr.astype(BF16)
    wr_cat = jnp.concatenate([wr_hi, (wr - wr_hi.astype(F32)).astype(BF16)], axis=1)
    q_scale = np.ones((IN_W,), np.float32)
    for c0, width in ((0, A_Q_W), (A_Q_W + 2 * LANES, B_W), (IN_W - M_W, M_W)):
        q_scale[c0:c0 + width] = SCALE * LOG2E
    rbias = router_bias.astype(F32).reshape(N_EXPERTS, 1)
    layers = []
    for l in range(DEPTH):
        g = grp_norm_g[l].astype(F32)
        layers.append(dict(
            w_in=(w_in[l] * q_scale).astype(BF16), w_mem_kv=w_mem_kv[l].astype(BF16),
            sink=sink_logits[l].astype(F32), na_bias=_na_bias(rpb[l], seq),
            g_a=g[:A_Q_W].reshape(1, -1), g_b=g[A_Q_W:A_Q_W + B_W].reshape(1, -1),
            g_m=g[A_Q_W + B_W:].reshape(1, -1),
            w_out=w_out[l].astype(BF16),
            ln1_g=ln1_g[l].astype(F32).reshape(1, -1), ln1_b=ln1_b[l].astype(F32).reshape(1, -1),
            wr_cat=wr_cat, rbias=rbias,
            w_gate=w_gate, w_up=w_up, w_down=w_down, layer=l,
            ln2_g=ln2_g[l].astype(F32).reshape(1, -1), ln2_b=ln2_b[l].astype(F32).reshape(1, -1)))

    def trunk(x, mem):
        for p in layers:
            x = _layer(x, mem, p)
        return x

    return (trunk(x_prompt, mem_prompt), trunk(x_sample, mem_sample))
```

```python
import functools

import jax
import jax.numpy as jnp
import numpy as np
from jax import lax
from jax.experimental import pallas as pl
from jax.experimental.pallas import tpu as pltpu

F32 = jnp.float32
BF16 = jnp.bfloat16

D_MODEL = 1024
DEPTH = 2
HEAD_DIM = 64
A_HEADS = 8
WINDOW = 128
B_HEADS = 4
GRID_W = 64
NA_ROWS = 8
NA_COLS = 16
M_HEADS = 4
N_EXPERTS = 16
N_GROUPS = 4
D_EXPERT = 512
PAIRS_PER_GROUP = 6
N_CLASSES = N_GROUPS * PAIRS_PER_GROUP
LN_EPS = 1e-5
A_Q_W = 512
B_W = 256
M_W = 256
IN_W = 1792
ALPHA = (2 * DEPTH) ** 0.25
SCALE = HEAD_DIM ** -0.5
LOG2E = 1.4426950408889634
NEG = -1e30

LANES = 128
SOFTMAX_HEADS = 2
NA_TILE_ROWS = 2
NA_STEP_TILES = 4
NA_KEY_ROWS = 10
VMEM_LIMIT = 56 * 1024 * 1024


def _params(n_axes, vmem=VMEM_LIMIT):
    return pltpu.CompilerParams(dimension_semantics=("arbitrary",) * n_axes, vmem_limit_bytes=vmem)


def _dot_nt(a, b):
    return lax.dot_general(a, b, (((1,), (1,)), ((), ())), preferred_element_type=F32)


def _dot(a, b):
    return jnp.dot(a, b, preferred_element_type=F32)


def _lo_mask():
    return lax.broadcasted_iota(jnp.int32, (1, LANES), 1) < HEAD_DIM


def _mm_kernel(x_ref, w_ref, o_ref):
    o_ref[...] = _dot(x_ref[...].astype(BF16), w_ref[...]).astype(o_ref.dtype)


def _matmul(x, w, tm, name):
    n, k = x.shape
    m = w.shape[1]
    tm = min(tm, n)
    assert n % tm == 0
    return pl.pallas_call(
        _mm_kernel,
        grid=(n // tm,),
        in_specs=[pl.BlockSpec((tm, k), lambda i: (i, 0)), pl.BlockSpec((k, m), lambda i: (0, 0))],
        out_specs=pl.BlockSpec((tm, m), lambda i: (i, 0)),
        out_shape=jax.ShapeDtypeStruct((n, m), BF16),
        compiler_params=_params(1),
        name=name,
    )(x, w)


def _group_rms(of, g):
    ms = jnp.mean(of * of, axis=-1, keepdims=True)
    return of * lax.rsqrt(ms + LN_EPS) * g


def _win_kernel(sink_ref, q_ref, k_ref, v_ref, g_ref, o_ref, ks_ref, vs_ref, bias_ref, s_ref, p_ref, d_ref, *,
                seq, tq):
    i = pl.program_id(1)
    lo = _lo_mask()
    chunk = 512

    @pl.when(i == 0)
    def _build():
        zeros = jnp.zeros((WINDOW, LANES), BF16)
        for t in range(4):
            for dst in (ks_ref, vs_ref):
                dst[t, 0:WINDOW, :] = zeros
                dst[t, seq + WINDOW:seq + 2 * WINDOW, :] = zeros

        def body(c, carry):
            r0 = pl.multiple_of(c * chunk, chunk)
            for src, dst in ((k_ref, ks_ref), (v_ref, vs_ref)):
                x = src[pl.ds(r0, chunk), :].astype(F32)
                xr = pltpu.roll(x, HEAD_DIM, 1)
                rows = pl.ds(r0 + WINDOW, chunk)
                dst[0, rows, :] = jnp.where(lo, x, 0.0).astype(BF16)
                dst[1, rows, :] = jnp.where(lo, 0.0, xr).astype(BF16)
                dst[2, rows, :] = jnp.where(lo, xr, 0.0).astype(BF16)
                dst[3, rows, :] = jnp.where(lo, 0.0, x).astype(BF16)
            return carry

        lax.fori_loop(0, seq // chunk, body, 0)

    kw = 3 * WINDOW

    @pl.when(jnp.logical_and(pl.program_id(0) == 0, i == 0))
    def _build_bias():
        a_i = lax.broadcasted_iota(jnp.int32, (WINDOW, kw), 0)
        j_i = lax.broadcasted_iota(jnp.int32, (WINDOW, kw), 1)
        dist = jnp.abs(a_i + WINDOW - j_i)
        absd = dist.astype(F32)
        for variant in range(3):
            valid = dist <= WINDOW
            if variant == 0:
                valid = valid & (j_i >= WINDOW)
            if variant == 2:
                valid = valid & (j_i < 2 * WINDOW)
            for head in range(A_HEADS):
                slope = 2.0 ** (-8.0 * (head + 1) / A_HEADS)
                bias_ref[variant * A_HEADS + head] = jnp.where(valid, (-slope * LOG2E) * absd, NEG)

    g = g_ref[...]

    nblk = tq // WINDOW
    wins, variants = [], []
    for j in range(nblk):
        qs = pl.multiple_of(i * tq + j * WINDOW, WINDOW)
        wins.append(pl.ds(qs, kw))
        variants.append(jnp.where(qs == 0, 0, jnp.where(qs == seq - WINDOW, 2, 1)))
    for j in range(nblk):
        for head in range(A_HEADS):
            c, t = divmod(head, 2)
            q2 = q_ref[j * WINDOW:(j + 1) * WINDOW, c * LANES:(c + 1) * LANES]
            s_ref[j * A_HEADS + head] = (_dot_nt(q2, ks_ref[2 * (c // 2) + t, wins[j], :])
                                         + bias_ref[variants[j] * A_HEADS + head])
    for j in range(nblk):
        for head in range(A_HEADS):
            s = s_ref[j * A_HEADS + head]
            sk = sink_ref[head] * LOG2E
            m = jnp.maximum(jnp.max(s, axis=-1, keepdims=True), sk)
            p = jnp.exp2(s - m)
            den = jnp.sum(p, axis=-1, keepdims=True) + jnp.exp2(sk - m)
            p_ref[j * A_HEADS + head] = p.astype(BF16)
            d_ref[j * A_HEADS + head] = jnp.broadcast_to(den, (WINDOW, LANES))
    for j in range(nblk):
        outs = []
        for c in range(4):
            h = c // 2
            u = j * A_HEADS + 2 * c
            num = (_dot(p_ref[u], vs_ref[2 * h, wins[j], :])
                   + _dot(p_ref[u + 1], vs_ref[2 * h + 1, wins[j], :]))
            outs.append(num / jnp.where(lo, d_ref[u], d_ref[u + 1]))
        of = jnp.concatenate(outs, axis=1)
        o_ref[j * WINDOW:(j + 1) * WINDOW, :] = _group_rms(of, g).astype(o_ref.dtype)


def _window_attention(proj, sink, g, tq=512):
    b, s, _ = proj.shape
    assert s % tq == 0 and s >= 2 * WINDOW
    kern = functools.partial(_win_kernel, seq=s, tq=tq)
    return pl.pallas_call(
        kern,
        grid=(b, s // tq),
        in_specs=[
            pl.BlockSpec(memory_space=pltpu.SMEM),
            pl.BlockSpec((None, tq, A_Q_W), lambda bi, i: (bi, i, 0)),
            pl.BlockSpec((None, s, LANES), lambda bi, i: (bi, 0, 4)),
            pl.BlockSpec((None, s, LANES), lambda bi, i: (bi, 0, 5)),
            pl.BlockSpec((1, A_Q_W), lambda bi, i: (0, 0)),
        ],
        out_specs=pl.BlockSpec((None, tq, A_Q_W), lambda bi, i: (bi, i, 0)),
        out_shape=jax.ShapeDtypeStruct((b, s, A_Q_W), BF16),
        scratch_shapes=[pltpu.VMEM((4, s + 2 * WINDOW, LANES), BF16),
                        pltpu.VMEM((4, s + 2 * WINDOW, LANES), BF16),
                        pltpu.VMEM((3 * A_HEADS, WINDOW, 3 * WINDOW), F32),
                        pltpu.VMEM((tq // WINDOW * A_HEADS, WINDOW, 3 * WINDOW), F32),
                        pltpu.VMEM((tq // WINDOW * A_HEADS, WINDOW, 3 * WINDOW), BF16),
                        pltpu.VMEM((tq // WINDOW * A_HEADS, WINDOW, LANES), F32)],
        compiler_params=_params(2),
        name="window_attention",
    )(sink, proj, proj, proj, g)


def _na_tables(seq):
    rows = seq // GRID_W
    kh = min(NA_ROWS, rows)
    assert rows % NA_TILE_ROWS == 0 and rows >= NA_KEY_ROWS and NA_KEY_ROWS % 2 == 0
    nt = rows // NA_TILE_ROWS
    u = np.arange(NA_TILE_ROWS * GRID_W) // GRID_W
    c = np.arange(NA_TILE_ROWS * GRID_W) % GRID_W
    ki = np.arange(NA_KEY_ROWS * GRID_W) // GRID_W
    kc = np.arange(NA_KEY_ROWS * GRID_W) % GRID_W
    cs = np.clip(c - NA_COLS // 2, 0, GRID_W - NA_COLS)
    col_ok = (kc[None, :] >= cs[:, None]) & (kc[None, :] < cs[:, None] + NA_COLS)
    dc = np.clip(kc[None, :] - c[:, None] + NA_COLS - 1, 0, 2 * NA_COLS - 2)
    types, type_of, kstart = [], [], []
    for t in range(nt):
        r0 = t * NA_TILE_ROWS
        k0 = int(np.clip(r0 - kh // 2, 0, rows - NA_KEY_ROWS))
        k0 -= k0 % 2
        r = r0 + u
        rs = np.clip(r - kh // 2, 0, rows - kh)
        kr = k0 + ki
        row_ok = (kr[None, :] >= rs[:, None]) & (kr[None, :] < rs[:, None] + kh)
        assert (rs >= k0).all() and (rs + kh <= k0 + NA_KEY_ROWS).all()
        dr = np.clip(kr[None, :] - r[:, None] + NA_ROWS - 1, 0, 2 * NA_ROWS - 2)
        key = (dr.tobytes(), row_ok.tobytes())
        for idx, (k_, _, _) in enumerate(types):
            if k_ == key:
                type_of.append(idx)
                break
        else:
            type_of.append(len(types))
            types.append((key, dr, row_ok & col_ok))
        kstart.append(k0)
    dr_rows = np.stack([t_[1][::GRID_W, ::GRID_W] for t_ in types])
    ok_all = np.stack([t_[2] for t_ in types])
    dc_cols = dc[:GRID_W, :GRID_W]
    return (np.asarray(type_of, np.int32), np.asarray(kstart, np.int32), dr_rows, dc_cols, ok_all)


def _na_bias(rpb, seq):
    _, _, dr_rows, dc_cols, ok = _na_tables(seq)
    nt = dr_rows.shape[0]
    hi = lax.Precision.HIGHEST
    col_sel = jnp.asarray(np.eye(2 * NA_COLS - 1, dtype=np.float32)[dc_cols])
    row_sel = jnp.asarray(np.eye(2 * NA_ROWS - 1, dtype=np.float32)[dr_rows])
    by_col = jnp.einsum("hdj,ckj->hdck", rpb.astype(F32), col_sel, precision=hi)
    bias = jnp.einsum("tuid,hdck->thucik", row_sel, by_col, precision=hi)
    bias = bias.reshape(nt, B_HEADS, NA_TILE_ROWS * GRID_W, NA_KEY_ROWS * GRID_W)
    return jnp.where(ok[:, None], bias * LOG2E, NEG)


def _na_kernel(type_ref, kstart_ref, q_ref, k_ref, v_ref, *rest, seq):
    del type_ref
    bias_refs = rest[:NA_STEP_TILES]
    g_ref, o_ref, ks_ref, vs_ref, s_ref, p_ref = rest[NA_STEP_TILES:]
    i = pl.program_id(1)
    lo = _lo_mask()
    chunk = 512

    @pl.when(i == 0)
    def _build():
        def body(c, carry):
            rows = pl.ds(pl.multiple_of(c * chunk, chunk), chunk)
            for src, dst in ((k_ref, ks_ref), (v_ref, vs_ref)):
                for p in range(2):
                    x = src[rows, p * LANES:(p + 1) * LANES]
                    dst[2 * p, rows, :] = jnp.where(lo, x, jnp.zeros_like(x))
                    dst[2 * p + 1, rows, :] = jnp.where(lo, jnp.zeros_like(x), x)
            return carry

        lax.fori_loop(0, seq // chunk, body, 0)

    nk = NA_KEY_ROWS * GRID_W
    tq = NA_TILE_ROWS * GRID_W
    units = [(t, head) for t in range(NA_STEP_TILES) for head in range(B_HEADS)]
    wins = [pl.ds(pl.multiple_of(kstart_ref[i * NA_STEP_TILES + t] * GRID_W, LANES), nk)
            for t in range(NA_STEP_TILES)]
    for u, (t, head) in enumerate(units):
        q2 = q_ref[t * tq:(t + 1) * tq, (head // 2) * LANES:(head // 2 + 1) * LANES]
        s_ref[u] = _dot_nt(q2, ks_ref[head, wins[t], :]) + bias_refs[t][head]
    invs = []
    for u in range(len(units)):
        s = s_ref[u]
        m = jnp.max(s, axis=-1, keepdims=True)
        e = jnp.exp2(s - m)
        invs.append(1.0 / jnp.sum(e, axis=-1, keepdims=True))
        p_ref[u] = e.astype(BF16)
    g = g_ref[...]
    for t in range(NA_STEP_TILES):
        outs = []
        for p in range(2):
            u = t * B_HEADS + 2 * p
            num = _dot(p_ref[u], vs_ref[2 * p, wins[t], :]) + _dot(p_ref[u + 1], vs_ref[2 * p + 1, wins[t], :])
            outs.append(num * jnp.where(lo, invs[u], invs[u + 1]))
        of = jnp.concatenate(outs, axis=1)
        o_ref[t * tq:(t + 1) * tq, :] = _group_rms(of, g).astype(o_ref.dtype)


def _neighborhood_attention(proj, bias, g):
    b, s, _ = proj.shape
    type_of, kstart, _, _, _ = _na_tables(s)
    tq = NA_TILE_ROWS * GRID_W
    nk = NA_KEY_ROWS * GRID_W
    step_q = NA_STEP_TILES * tq
    assert s % step_q == 0
    kern = functools.partial(_na_kernel, seq=s)

    def bias_spec(t):
        return pl.BlockSpec((None, B_HEADS, tq, nk), lambda bi, i, ty, ks: (ty[i * NA_STEP_TILES + t], 0, 0, 0))

    grid_spec = pltpu.PrefetchScalarGridSpec(
        num_scalar_prefetch=2,
        grid=(b, s // step_q),
        in_specs=[
            pl.BlockSpec((None, step_q, B_W), lambda bi, i, ty, ks: (bi, i, 3)),
            pl.BlockSpec((None, s, B_W), lambda bi, i, ty, ks: (bi, 0, 4)),
            pl.BlockSpec((None, s, B_W), lambda bi, i, ty, ks: (bi, 0, 5)),
            *[bias_spec(t) for t in range(NA_STEP_TILES)],
            pl.BlockSpec((1, B_W), lambda bi, i, ty, ks: (0, 0)),
        ],
        out_specs=pl.BlockSpec((None, step_q, B_W), lambda bi, i, ty, ks: (bi, i, 0)),
        scratch_shapes=[pltpu.VMEM((4, s, LANES), BF16), pltpu.VMEM((4, s, LANES), BF16),
                        pltpu.VMEM((NA_STEP_TILES * B_HEADS, tq, nk), F32),
                        pltpu.VMEM((NA_STEP_TILES * B_HEADS, tq, nk), BF16)],
    )
    return pl.pallas_call(
        kern,
        grid_spec=grid_spec,
        out_shape=jax.ShapeDtypeStruct((b, s, B_W), BF16),
        compiler_params=_params(2),
        name="neighborhood_attention",
    )(jnp.asarray(type_of), jnp.asarray(kstart), proj, proj, proj, *([bias] * NA_STEP_TILES), g)


def _mem_kernel(q_ref, mkv_ref, g_ref, o_ref, s_ref, p_ref):
    lo = _lo_mask()
    hi = jnp.logical_not(lo)

    def masked(col0, head):
        x = mkv_ref[:, col0 + (head // 2) * LANES:col0 + (head // 2 + 1) * LANES]
        return jnp.where(lo if head % 2 == 0 else hi, x, jnp.zeros_like(x))

    for head in range(M_HEADS):
        q2 = q_ref[:, (head // 2) * LANES:(head // 2 + 1) * LANES]
        s_ref[head] = _dot_nt(q2, masked(0, head))
    invs = []
    for head in range(M_HEADS):
        s = s_ref[head]
        m = jnp.max(s, axis=-1, keepdims=True)
        e = jnp.exp2(s - m)
        invs.append(1.0 / jnp.sum(e, axis=-1, keepdims=True))
        p_ref[head] = e.astype(BF16)
    outs = []
    for p in range(2):
        num = _dot(p_ref[2 * p], masked(M_W, 2 * p)) + _dot(p_ref[2 * p + 1], masked(M_W, 2 * p + 1))
        outs.append(num * jnp.where(lo, invs[2 * p], invs[2 * p + 1]))
    of = jnp.concatenate(outs, axis=1)
    o_ref[...] = _group_rms(of, g_ref[...]).astype(o_ref.dtype)


def _memory_attention(proj, mkv, g, tq=1024):
    b, s, _ = proj.shape
    mlen = mkv.shape[1]
    return pl.pallas_call(
        _mem_kernel,
        grid=(b, s // tq),
        in_specs=[
            pl.BlockSpec((None, tq, M_W), lambda bi, i: (bi, i, 6)),
            pl.BlockSpec((None, mlen, 2 * M_W), lambda bi, i: (bi, 0, 0)),
            pl.BlockSpec((1, M_W), lambda bi, i: (0, 0)),
        ],
        out_specs=pl.BlockSpec((None, tq, M_W), lambda bi, i: (bi, i, 0)),
        out_shape=jax.ShapeDtypeStruct((b, s, M_W), BF16),
        scratch_shapes=[pltpu.VMEM((M_HEADS, tq, mlen), F32), pltpu.VMEM((M_HEADS, tq, mlen), BF16)],
        compiler_params=_params(2),
        name="memory_attention",
    )(proj, mkv, g)


def _layer_norm(z, g, b):
    mu = jnp.mean(z, axis=-1, keepdims=True)
    zc = z - mu
    var = jnp.mean(zc * zc, axis=-1, keepdims=True)
    return zc * lax.rsqrt(var + LN_EPS) * g + b


def _route(lg_t, rbias):
    e, t = lg_t.shape
    eg = e // N_GROUPS
    scores = 1.0 / (1.0 + jnp.exp(-lg_t))
    sel = scores + rbias
    row = lax.broadcasted_iota(jnp.int32, (e, t), 0)
    best = None
    gidx = None
    for gi in range(N_GROUPS):
        r = [sel[gi * eg + k:gi * eg + k + 1, :] for k in range(eg)]
        top2 = None
        for a in range(eg):
            for b in range(a + 1, eg):
                pair = r[a] + r[b]
                top2 = pair if top2 is None else jnp.maximum(top2, pair)
        if best is None:
            best, gidx = top2, jnp.zeros((1, t), jnp.int32)
        else:
            better = top2 > best
            gidx = jnp.where(better, gi, gidx)
            best = jnp.maximum(best, top2)
    in_group = (row // eg) == gidx
    masked = jnp.where(in_group, sel, -jnp.inf)
    m1 = jnp.max(masked, axis=0, keepdims=True)
    i1 = jnp.min(jnp.where(masked == m1, row, e), axis=0, keepdims=True)
    rest = jnp.where(row == i1, -jnp.inf, masked)
    m2 = jnp.max(rest, axis=0, keepdims=True)
    i2 = jnp.min(jnp.where(rest == m2, row, e), axis=0, keepdims=True)
    pick1 = row == i1
    pick2 = row == i2
    w1 = jnp.sum(jnp.where(pick1, scores, 0.0), axis=0, keepdims=True)
    w2 = jnp.sum(jnp.where(pick2, scores, 0.0), axis=0, keepdims=True)
    tot = w1 + w2
    return i1, i2, pick1, pick2, w1 / tot, w2 / tot


def _out_kernel(ma_ref, mb_ref, mm_ref, w_ref, x_ref, g_ref, b_ref, wr_ref, rb_ref, tri_ref,
                x1_ref, info_ref, wcol_ref, cnt_ref, carry_ref):
    @pl.when(pl.program_id(0) == 0)
    def _init():
        carry_ref[...] = jnp.zeros_like(carry_ref)

    y = _dot(ma_ref[...], w_ref[0:A_Q_W, :])
    y = y + _dot(mb_ref[...], w_ref[A_Q_W:A_Q_W + B_W, :])
    y = y + _dot(mm_ref[...], w_ref[A_Q_W + B_W:, :])
    x1 = _layer_norm(ALPHA * x_ref[...] + y, g_ref[...], b_ref[...])
    x1_ref[...] = x1
    x_hi = x1.astype(BF16)
    x_lo = (x1 - x_hi.astype(F32)).astype(BF16)
    hi2 = _dot(x_hi, wr_ref[...])
    lg = hi2[:, 0:LANES] + hi2[:, LANES:] + _dot(x_lo, wr_ref[:, 0:LANES])
    lg_t = lg.T[0:N_EXPERTS, :]
    i1, i2, _, _, w1, w2 = _route(lg_t, rb_ref[...])
    tm = lg_t.shape[1]
    eg = N_EXPERTS // N_GROUPS
    a = jnp.bitwise_and(i1, eg - 1)
    b = jnp.bitwise_and(i2, eg - 1)
    lo = jnp.minimum(a, b)
    hi = jnp.maximum(a, b)
    pair = jnp.where(lo == 0, 0, jnp.where(lo == 1, 3, 5)) + hi - lo - 1
    cls = lax.shift_right_logical(i1, 2) * PAIRS_PER_GROUP + pair
    first_is_lo = a < b
    row = lax.broadcasted_iota(jnp.int32, (N_CLASSES, tm), 0)
    mine = row == cls
    member = jnp.where(mine, 1.0, 0.0)
    before = carry_ref[:, 0:1] + _dot(member.astype(BF16), tri_ref[...])
    rank = jnp.sum(jnp.where(mine, before, 0.0), axis=0, keepdims=True)
    total = carry_ref[...] + jnp.sum(member, axis=1, keepdims=True)
    carry_ref[...] = total
    cnt_ref[...] = total
    info_ref[...] = jnp.concatenate([cls, rank.astype(jnp.int32), jnp.zeros((6, tm), jnp.int32)], axis=0)
    w_pad = jnp.concatenate([jnp.where(first_is_lo, w1, w2), jnp.where(first_is_lo, w2, w1),
                             jnp.zeros((LANES - 2, tm), F32)], axis=0)
    wcol_ref[...] = w_pad.T


def _out_proj(ma, mb, mm, w_out, x, g, b, wr_cat, rbias, tm=1024):
    n = x.shape[0]
    row = lambda i: (i, 0)
    fix = lambda i: (0, 0)
    tri = jnp.asarray(np.triu(np.ones((tm, tm), np.float32), k=1), BF16)
    return pl.pallas_call(
        _out_kernel,
        grid=(n // tm,),
        in_specs=[
            pl.BlockSpec((tm, A_Q_W), row), pl.BlockSpec((tm, B_W), row), pl.BlockSpec((tm, M_W), row),
            pl.BlockSpec((D_MODEL, D_MODEL), fix), pl.BlockSpec((tm, D_MODEL), row),
            pl.BlockSpec((1, D_MODEL), fix), pl.BlockSpec((1, D_MODEL), fix),
            pl.BlockSpec((D_MODEL, 2 * LANES), fix), pl.BlockSpec((N_EXPERTS, 1), fix),
            pl.BlockSpec((tm, tm), fix),
        ],
        out_specs=[pl.BlockSpec((tm, D_MODEL), row), pl.BlockSpec((8, tm), lambda i: (0, i)),
                   pl.BlockSpec((tm, LANES), row), pl.BlockSpec((N_CLASSES, LANES), fix)],
        out_shape=[jax.ShapeDtypeStruct((n, D_MODEL), F32), jax.ShapeDtypeStruct((8, n), jnp.int32),
                   jax.ShapeDtypeStruct((n, LANES), F32), jax.ShapeDtypeStruct((N_CLASSES, LANES), F32)],
        scratch_shapes=[pltpu.VMEM((N_CLASSES, LANES), F32)],
        compiler_params=_params(1),
        name="out_proj_ln_router",
    )(ma, mb, mm, w_out, x, g, b, wr_cat, rbias, tri)


MOE_ROW_TILE = 256
MOE_TOKEN_TILE = 1024


SUBLANES = 8


def _index_copy(pos_hbm, idx_ref, isem, step, slot):
    width = pos_hbm.shape[1]
    dst = idx_ref.at[pl.ds(pl.multiple_of(slot * width, width), width)]
    return pltpu.make_async_copy(pos_hbm.at[step], dst, isem.at[slot])


assert D_MODEL == SUBLANES * LANES
ZERO_CHUNK = 64


def _token_tile(ref, p):
    return ref.at[pl.ds(pl.multiple_of(p * SUBLANES, SUBLANES), SUBLANES)]


def _store_token_tiles(dst_ref, x):
    t = x.shape[0]
    for c in range(SUBLANES):
        dst_ref[pl.ds(c, t, stride=SUBLANES), :] = x[:, c * LANES:(c + 1) * LANES]


def _load_token_tiles(src_ref, t):
    return jnp.concatenate([src_ref[pl.ds(c, t, stride=SUBLANES), :] for c in range(SUBLANES)], axis=1)


def _dispatch_kernel(lo_ref, hi_ref, pos_hbm, x_ref, xs_hbm, idx_ref, xt_ref, z_ref, isem, rsem, zsem, *, td):
    i = pl.program_id(0)
    n = pl.num_programs(0)
    slot = lax.rem(i, 2)

    @pl.when(i == 0)
    def _first():
        _index_copy(pos_hbm, idx_ref, isem, 0, 0).start()
        z_ref[...] = jnp.zeros_like(z_ref)
        one = z_ref.at[pl.ds(0, SUBLANES)]

        def fill(e, carry):
            a = lo_ref[e]
            b = hi_ref[e]
            n_single = jnp.minimum(b - a, jnp.bitwise_and(-a, ZERO_CHUNK - 1))
            a2 = a + n_single
            n_chunk = lax.shift_right_logical(b - a2, 6)

            def single(r, c):
                pltpu.make_async_copy(one, _token_tile(xs_hbm, a + r), zsem).start()
                return c

            def chunk(r, c):
                dst = xs_hbm.at[pl.ds(pl.multiple_of((a2 + r * ZERO_CHUNK) * SUBLANES, SUBLANES),
                                      ZERO_CHUNK * SUBLANES)]
                pltpu.make_async_copy(z_ref, dst, zsem).start()
                return c

            def single_wait(r, c):
                pltpu.make_async_copy(one, _token_tile(xs_hbm, 0), zsem).wait()
                return c

            def chunk_wait(r, c):
                pltpu.make_async_copy(z_ref, xs_hbm.at[pl.ds(0, ZERO_CHUNK * SUBLANES)], zsem).wait()
                return c

            lax.fori_loop(0, n_single, single, 0)
            lax.fori_loop(0, n_chunk, chunk, 0)
            lax.fori_loop(0, n_single, single_wait, 0)
            lax.fori_loop(0, n_chunk, chunk_wait, 0)
            return carry

        lax.fori_loop(0, N_CLASSES, fill, 0)

    _index_copy(pos_hbm, idx_ref, isem, i, slot).wait()

    @pl.when(i + 1 < n)
    def _next():
        _index_copy(pos_hbm, idx_ref, isem, i + 1, 1 - slot).start()

    _store_token_tiles(xt_ref, x_ref[...])
    base = slot * td

    def body(j, carry):
        for u in range(SUBLANES):
            t = j * SUBLANES + u
            p = idx_ref[base + t]
            pltpu.make_async_copy(_token_tile(xt_ref, t), _token_tile(xs_hbm, p), rsem).start(priority=u % 2)
        return carry

    lax.fori_loop(0, td // SUBLANES, body, 0)
    pltpu.make_async_copy(xt_ref, xs_hbm.at[pl.ds(0, td * SUBLANES)], rsem).wait()


def _dispatch(x1, pos_t, pad_lo, pad_hi, n_rows, td):
    n = x1.shape[0]
    kern = functools.partial(_dispatch_kernel, td=td)
    grid_spec = pltpu.PrefetchScalarGridSpec(
        num_scalar_prefetch=2,
        grid=(n // td,),
        in_specs=[pl.BlockSpec(memory_space=pl.ANY),
                  pl.BlockSpec((td, D_MODEL), lambda i, lo, hi: (i, 0))],
        out_specs=pl.BlockSpec(memory_space=pl.ANY),
        scratch_shapes=[pltpu.SMEM((2 * td,), jnp.int32),
                        pltpu.VMEM((td * SUBLANES, LANES), F32),
                        pltpu.VMEM((ZERO_CHUNK * SUBLANES, LANES), F32),
                        pltpu.SemaphoreType.DMA((2,)), pltpu.SemaphoreType.DMA(()),
                        pltpu.SemaphoreType.DMA(())],
    )
    return pl.pallas_call(
        kern,
        grid_spec=grid_spec,
        out_shape=jax.ShapeDtypeStruct((n_rows * SUBLANES, LANES), F32),
        compiler_params=_params(1),
        name="moe_dispatch",
    )(pad_lo, pad_hi, pos_t, x1)


def _gmm_kernel(elo_ref, ehi_ref, rows_ref, xs_ref, wg0_ref, wu0_ref, wd0_ref, wg1_ref, wu1_ref, wd1_ref, ys_ref):
    del elo_ref, ehi_ref
    nrows = rows_ref[pl.program_id(0)]

    @pl.when(nrows > 0)
    def _compute():
        t = xs_ref.shape[0] // SUBLANES
        x = _load_token_tiles(xs_ref, t).astype(BF16)
        for k, (wg_ref, wu_ref, wd_ref) in enumerate(((wg0_ref, wu0_ref, wd0_ref), (wg1_ref, wu1_ref, wd1_ref))):
            gate = _dot(x, wg_ref[...])
            up = _dot(x, wu_ref[...])
            h = (gate * (1.0 / (1.0 + jnp.exp(-gate))) * up).astype(BF16)
            y = _dot(h, wd_ref[...])
            for c in range(SUBLANES):
                ys_ref[pl.ds(k * SUBLANES + c, t, stride=2 * SUBLANES), :] = y[:, c * LANES:(c + 1) * LANES]

    @pl.when(nrows == 0)
    def _empty():
        ys_ref[...] = jnp.zeros_like(ys_ref)


def _grouped_mlp(xs, tile_lo, tile_hi, tile_rows, wg, wu, wd, layer, tm):
    n_rows = xs.shape[0] // SUBLANES

    def wspec(shape, which):
        return pl.BlockSpec((None, None) + shape, lambda g, lo, hi, tr: (layer, (lo, hi)[which][g], 0, 0))

    grid_spec = pltpu.PrefetchScalarGridSpec(
        num_scalar_prefetch=3,
        grid=(n_rows // tm,),
        in_specs=[pl.BlockSpec((tm * SUBLANES, LANES), lambda g, lo, hi, tr: (g, 0)),
                  wspec((D_MODEL, D_EXPERT), 0), wspec((D_MODEL, D_EXPERT), 0), wspec((D_EXPERT, D_MODEL), 0),
                  wspec((D_MODEL, D_EXPERT), 1), wspec((D_MODEL, D_EXPERT), 1), wspec((D_EXPERT, D_MODEL), 1)],
        out_specs=pl.BlockSpec((tm * 2 * SUBLANES, LANES), lambda g, lo, hi, tr: (g, 0)),
    )
    return pl.pallas_call(
        _gmm_kernel,
        grid_spec=grid_spec,
        out_shape=jax.ShapeDtypeStruct((n_rows * 2 * SUBLANES, LANES), F32),
        compiler_params=_params(1),
        name="moe_grouped_mlp",
    )(tile_lo, tile_hi, tile_rows, xs, wg, wu, wd, wg, wu, wd)


def _combine_kernel(pos_hbm, ys_hbm, x1_ref, w_ref, g_ref, b_ref, o_ref, idx_ref, buf_ref, isem, rsem, *, td):
    i = pl.program_id(0)
    n = pl.num_programs(0)
    slot = lax.rem(i, 2)

    pair_rows = 2 * SUBLANES

    def pair_tile(ref, p):
        return ref.at[pl.ds(pl.multiple_of(p * pair_rows, pair_rows), pair_rows)]

    def issue_rows(sl):
        base = sl * td

        def body(j, carry):
            for u in range(SUBLANES):
                t = j * SUBLANES + u
                p = idx_ref[base + t]
                pltpu.make_async_copy(pair_tile(ys_hbm, p), pair_tile(buf_ref.at[sl], t),
                                      rsem.at[sl]).start(priority=u % 2)
            return carry

        lax.fori_loop(0, td // SUBLANES, body, 0)

    @pl.when(i == 0)
    def _first():
        _index_copy(pos_hbm, idx_ref, isem, 0, 0).start()
        _index_copy(pos_hbm, idx_ref, isem, 0, 0).wait()
        issue_rows(0)

        @pl.when(n > 1)
        def _():
            _index_copy(pos_hbm, idx_ref, isem, 1, 1).start()

    @pl.when(i + 1 < n)
    def _next():
        _index_copy(pos_hbm, idx_ref, isem, i + 1, 1 - slot).wait()
        issue_rows(1 - slot)

        @pl.when(i + 2 < n)
        def _():
            _index_copy(pos_hbm, idx_ref, isem, i + 2, slot).start()

    pltpu.make_async_copy(ys_hbm.at[pl.ds(0, td * pair_rows)], buf_ref.at[slot], rsem.at[slot]).wait()
    w = w_ref[...]
    src = buf_ref.at[slot]
    y_lo = jnp.concatenate([src[pl.ds(c, td, stride=pair_rows), :] for c in range(SUBLANES)], axis=1)
    y_hi = jnp.concatenate([src[pl.ds(SUBLANES + c, td, stride=pair_rows), :] for c in range(SUBLANES)], axis=1)
    y = w[:, 0:1] * y_lo + w[:, 1:2] * y_hi
    o_ref[...] = _layer_norm(ALPHA * x1_ref[...] + y, g_ref[...], b_ref[...])


def _combine(ys, pos_t, x1, wcol, g, b, td):
    n = x1.shape[0]
    row = lambda i: (i, 0)
    fix = lambda i: (0, 0)
    kern = functools.partial(_combine_kernel, td=td)
    return pl.pallas_call(
        kern,
        grid=(n // td,),
        in_specs=[pl.BlockSpec(memory_space=pl.ANY), pl.BlockSpec(memory_space=pl.ANY),
                  pl.BlockSpec((td, D_MODEL), row), pl.BlockSpec((td, LANES), row),
                  pl.BlockSpec((1, D_MODEL), fix), pl.BlockSpec((1, D_MODEL), fix)],
        out_specs=pl.BlockSpec((td, D_MODEL), row),
        out_shape=jax.ShapeDtypeStruct((n, D_MODEL), F32),
        scratch_shapes=[pltpu.SMEM((2 * td,), jnp.int32),
                        pltpu.VMEM((2, td * 2 * SUBLANES, LANES), F32),
                        pltpu.SemaphoreType.DMA((2,)), pltpu.SemaphoreType.DMA((2,))],
        compiler_params=_params(1),
        name="moe_combine_ln2",
    )(pos_t, ys, x1, wcol, g, b)


def _moe_routed(x1, info, wcol, cnt, p):
    n = x1.shape[0]
    tm, td = MOE_ROW_TILE, MOE_TOKEN_TILE
    n_tiles = n // tm + N_CLASSES
    counts = cnt[:, 0].astype(jnp.int32)
    padded = (counts + (tm - 1)) // tm * tm
    ends = jnp.cumsum(padded)
    offs = ends - padded
    starts = jnp.arange(n_tiles, dtype=jnp.int32) * tm
    tc = jnp.sum((ends[None, :] <= starts[:, None]).astype(jnp.int32), axis=1)
    tile_class = jnp.minimum(tc, N_CLASSES - 1)
    classes = jnp.arange(N_CLASSES, dtype=jnp.int32)
    pick = tile_class[:, None] == classes[None, :]
    seg_end = jnp.sum(jnp.where(pick, (offs + counts)[None, :], 0), axis=1)
    tile_rows = jnp.where(tc < N_CLASSES, jnp.clip(seg_end - starts, 0, tm), 0).astype(jnp.int32)
    eg = N_EXPERTS // N_GROUPS
    pairs = [(a, b) for a in range(eg) for b in range(a + 1, eg)]
    class_lo = np.asarray([g * eg + a for g in range(N_GROUPS) for a, _ in pairs], np.int32)
    class_hi = np.asarray([g * eg + b for g in range(N_GROUPS) for _, b in pairs], np.int32)
    tile_lo = jnp.sum(jnp.where(pick, class_lo[None, :], 0), axis=1).astype(jnp.int32)
    tile_hi = jnp.sum(jnp.where(pick, class_hi[None, :], 0), axis=1).astype(jnp.int32)
    cls, rank = info[0], info[1]
    pos = rank + jnp.sum(jnp.where(cls[None, :] == classes[:, None], offs[:, None], 0), axis=0)
    pos_t = pos.reshape(n // td, td)
    pad_lo = offs + counts
    pad_hi = ends.at[N_CLASSES - 1].set(n_tiles * tm)
    xs = _dispatch(x1, pos_t, pad_lo, pad_hi, n_tiles * tm, td)
    ys = _grouped_mlp(xs, tile_lo, tile_hi, tile_rows, p["w_gate"], p["w_up"], p["w_down"], p["layer"], tm)
    return _combine(ys, pos_t, x1, wcol, p["ln2_g"], p["ln2_b"], td)


def _layer(x, mem, p):
    b, s, d = x.shape
    n = b * s
    xf = x.reshape(n, d)
    proj = _matmul(xf, p["w_in"], 1024, "in_proj").reshape(b, s, IN_W)
    mkv = _matmul(mem.reshape(-1, d), p["w_mem_kv"], 512, "mem_kv").reshape(b, -1, 2 * M_W)
    oa = _window_attention(proj, p["sink"], p["g_a"])
    ob = _neighborhood_attention(proj, p["na_bias"], p["g_b"])
    om = _memory_attention(proj, mkv, p["g_m"])
    x1, info, wcol, cnt = _out_proj(oa.reshape(n, A_Q_W), ob.reshape(n, B_W), om.reshape(n, M_W), p["w_out"], xf,
                                    p["ln1_g"], p["ln1_b"], p["wr_cat"], p["rbias"])
    y = _moe_routed(x1, info, wcol, cnt, p)
    return y.reshape(b, s, d)


def kernel(x_prompt, x_sample, mem_prompt, mem_sample, w_in, w_mem_kv, sink_logits, rpb, grp_norm_g, w_out,
           ln1_g, ln1_b, w_router, router_bias, w_gate, w_up, w_down, ln2_g, ln2_b):
    seq = x_prompt.shape[1]
    assert x_sample.shape[1] == seq
    wr = jnp.pad(w_router.astype(F32), ((0, 0), (0, LANES - N_EXPERTS)))
    wr_hi = wr.astype(BF16)
    wr_cat = jnp.concatenate([wr_hi, (wr - wr_hi.astype(F32)).astype(BF16)], axis=1)
    q_scale = np.ones((IN_W,), np.float32)
    for c0, width in ((0, A_Q_W), (A_Q_W + 2 * LANES, B_W), (IN_W - M_W, M_W)):
        q_scale[c0:c0 + width] = SCALE * LOG2E
    rbias = router_bias.astype(F32).reshape(N_EXPERTS, 1)
    w_gate_b, w_up_b, w_down_b = w_gate.astype(BF16), w_up.astype(BF16), w_down.astype(BF16)
    layers = []
    for l in range(DEPTH):
        g = grp_norm_g[l].astype(F32)
        layers.append(dict(
            w_in=(w_in[l] * q_scale).astype(BF16), w_mem_kv=w_mem_kv[l].astype(BF16),
            sink=sink_logits[l].astype(F32), na_bias=_na_bias(rpb[l], seq),
            g_a=g[:A_Q_W].reshape(1, -1), g_b=g[A_Q_W:A_Q_W + B_W].reshape(1, -1),
            g_m=g[A_Q_W + B_W:].reshape(1, -1),
            w_out=w_out[l].astype(BF16),
            ln1_g=ln1_g[l].astype(F32).reshape(1, -1), ln1_b=ln1_b[l].astype(F32).reshape(1, -1),
            wr_cat=wr_cat, rbias=rbias,
            w_gate=w_gate_b, w_up=w_up_b, w_down=w_down_b, layer=l,
            ln2_g=ln2_g[l].astype(F32).reshape(1, -1), ln2_b=ln2_b[l].astype(F32).reshape(1, -1)))

    def trunk(x, mem):
        for p in layers:
            x = _layer(x, mem, p)
        return x

    return (trunk(x_prompt, mem_prompt), trunk(x_sample, mem_sample))
```

```python
import functools

import jax
import jax.numpy as jnp
import numpy as np
from jax import lax
from jax.experimental import pallas as pl
from jax.experimental.pallas import tpu as pltpu

F32 = jnp.float32
BF16 = jnp.bfloat16

D_MODEL = 1024
DEPTH = 2
HEAD_DIM = 64
A_HEADS = 8
WINDOW = 128
B_HEADS = 4
GRID_W = 64
NA_ROWS = 8
NA_COLS = 16
M_HEADS = 4
N_EXPERTS = 16
N_GROUPS = 4
D_EXPERT = 512
PAIRS_PER_GROUP = 6
N_CLASSES = N_GROUPS * PAIRS_PER_GROUP
LN_EPS = 1e-5
A_Q_W = 512
B_W = 256
M_W = 256
IN_W = 1792
ALPHA = (2 * DEPTH) ** 0.25
SCALE = HEAD_DIM ** -0.5
LOG2E = 1.4426950408889634
NEG = -1e30

LANES = 128
SOFTMAX_HEADS = 2
NA_TILE_ROWS = 2
NA_STEP_TILES = 4
NA_KEY_ROWS = 10
VMEM_LIMIT = 56 * 1024 * 1024


def _params(n_axes, vmem=VMEM_LIMIT):
    return pltpu.CompilerParams(dimension_semantics=("arbitrary",) * n_axes, vmem_limit_bytes=vmem)


def _dot_nt(a, b):
    return lax.dot_general(a, b, (((1,), (1,)), ((), ())), preferred_element_type=F32)


def _dot(a, b):
    return jnp.dot(a, b, preferred_element_type=F32)


def _lo_mask():
    return lax.broadcasted_iota(jnp.int32, (1, LANES), 1) < HEAD_DIM


def _mm_kernel(x_ref, w_ref, o_ref):
    o_ref[...] = _dot(x_ref[...].astype(BF16), w_ref[...]).astype(o_ref.dtype)


def _matmul(x, w, tm, name):
    n, k = x.shape
    m = w.shape[1]
    tm = min(tm, n)
    assert n % tm == 0
    return pl.pallas_call(
        _mm_kernel,
        grid=(n // tm,),
        in_specs=[pl.BlockSpec((tm, k), lambda i: (i, 0)), pl.BlockSpec((k, m), lambda i: (0, 0))],
        out_specs=pl.BlockSpec((tm, m), lambda i: (i, 0)),
        out_shape=jax.ShapeDtypeStruct((n, m), BF16),
        compiler_params=_params(1),
        name=name,
    )(x, w)


def _group_rms(of, g):
    ms = jnp.mean(of * of, axis=-1, keepdims=True)
    return of * lax.rsqrt(ms + LN_EPS) * g


def _win_kernel(sink_ref, q_ref, k_ref, v_ref, g_ref, o_ref, ks_ref, vs_ref, bias_ref, s_ref, p_ref, d_ref, *,
                seq, tq):
    i = pl.program_id(1)
    lo = _lo_mask()
    chunk = 512

    @pl.when(i == 0)
    def _build():
        zeros = jnp.zeros((WINDOW, LANES), BF16)
        for t in range(4):
            for dst in (ks_ref, vs_ref):
                dst[t, 0:WINDOW, :] = zeros
                dst[t, seq + WINDOW:seq + 2 * WINDOW, :] = zeros

        def body(c, carry):
            r0 = pl.multiple_of(c * chunk, chunk)
            for src, dst in ((k_ref, ks_ref), (v_ref, vs_ref)):
                x = src[pl.ds(r0, chunk), :].astype(F32)
                xr = pltpu.roll(x, HEAD_DIM, 1)
                rows = pl.ds(r0 + WINDOW, chunk)
                dst[0, rows, :] = jnp.where(lo, x, 0.0).astype(BF16)
                dst[1, rows, :] = jnp.where(lo, 0.0, xr).astype(BF16)
                dst[2, rows, :] = jnp.where(lo, xr, 0.0).astype(BF16)
                dst[3, rows, :] = jnp.where(lo, 0.0, x).astype(BF16)
            return carry

        lax.fori_loop(0, seq // chunk, body, 0)

    kw = 3 * WINDOW

    @pl.when(jnp.logical_and(pl.program_id(0) == 0, i == 0))
    def _build_bias():
        a_i = lax.broadcasted_iota(jnp.int32, (WINDOW, kw), 0)
        j_i = lax.broadcasted_iota(jnp.int32, (WINDOW, kw), 1)
        dist = jnp.abs(a_i + WINDOW - j_i)
        absd = dist.astype(F32)
        for variant in range(3):
            valid = dist <= WINDOW
            if variant == 0:
                valid = valid & (j_i >= WINDOW)
            if variant == 2:
                valid = valid & (j_i < 2 * WINDOW)
            for head in range(A_HEADS):
                slope = 2.0 ** (-8.0 * (head + 1) / A_HEADS)
                bias_ref[variant * A_HEADS + head] = jnp.where(valid, (-slope * LOG2E) * absd, NEG)

    g = g_ref[...]

    nblk = tq // WINDOW
    wins, variants = [], []
    for j in range(nblk):
        qs = pl.multiple_of(i * tq + j * WINDOW, WINDOW)
        wins.append(pl.ds(qs, kw))
        variants.append(jnp.where(qs == 0, 0, jnp.where(qs == seq - WINDOW, 2, 1)))
    for j in range(nblk):
        for head in range(A_HEADS):
            c, t = divmod(head, 2)
            q2 = q_ref[j * WINDOW:(j + 1) * WINDOW, c * LANES:(c + 1) * LANES]
            s_ref[j * A_HEADS + head] = (_dot_nt(q2, ks_ref[2 * (c // 2) + t, wins[j], :])
                                         + bias_ref[variants[j] * A_HEADS + head])
    for j in range(nblk):
        for head in range(A_HEADS):
            s = s_ref[j * A_HEADS + head]
            sk = sink_ref[head] * LOG2E
            m = jnp.maximum(jnp.max(s, axis=-1, keepdims=True), sk)
            p = jnp.exp2(s - m)
            den = jnp.sum(p, axis=-1, keepdims=True) + jnp.exp2(sk - m)
            p_ref[j * A_HEADS + head] = p.astype(BF16)
            d_ref[j * A_HEADS + head] = jnp.broadcast_to(den, (WINDOW, LANES))
    for j in range(nblk):
        outs = []
        for c in range(4):
            h = c // 2
            u = j * A_HEADS + 2 * c
            num = (_dot(p_ref[u], vs_ref[2 * h, wins[j], :])
                   + _dot(p_ref[u + 1], vs_ref[2 * h + 1, wins[j], :]))
            outs.append(num / jnp.where(lo, d_ref[u], d_ref[u + 1]))
        of = jnp.concatenate(outs, axis=1)
        o_ref[j * WINDOW:(j + 1) * WINDOW, :] = _group_rms(of, g).astype(o_ref.dtype)


def _window_attention(proj, sink, g, tq=512):
    b, s, _ = proj.shape
    assert s % tq == 0 and s >= 2 * WINDOW
    kern = functools.partial(_win_kernel, seq=s, tq=tq)
    return pl.pallas_call(
        kern,
        grid=(b, s // tq),
        in_specs=[
            pl.BlockSpec(memory_space=pltpu.SMEM),
            pl.BlockSpec((None, tq, A_Q_W), lambda bi, i: (bi, i, 0)),
            pl.BlockSpec((None, s, LANES), lambda bi, i: (bi, 0, 4)),
            pl.BlockSpec((None, s, LANES), lambda bi, i: (bi, 0, 5)),
            pl.BlockSpec((1, A_Q_W), lambda bi, i: (0, 0)),
        ],
        out_specs=pl.BlockSpec((None, tq, A_Q_W), lambda bi, i: (bi, i, 0)),
        out_shape=jax.ShapeDtypeStruct((b, s, A_Q_W), BF16),
        scratch_shapes=[pltpu.VMEM((4, s + 2 * WINDOW, LANES), BF16),
                        pltpu.VMEM((4, s + 2 * WINDOW, LANES), BF16),
                        pltpu.VMEM((3 * A_HEADS, WINDOW, 3 * WINDOW), F32),
                        pltpu.VMEM((tq // WINDOW * A_HEADS, WINDOW, 3 * WINDOW), F32),
                        pltpu.VMEM((tq // WINDOW * A_HEADS, WINDOW, 3 * WINDOW), BF16),
                        pltpu.VMEM((tq // WINDOW * A_HEADS, WINDOW, LANES), F32)],
        compiler_params=_params(2),
        name="window_attention",
    )(sink, proj, proj, proj, g)


def _na_tables(seq):
    rows = seq // GRID_W
    kh = min(NA_ROWS, rows)
    assert rows % NA_TILE_ROWS == 0 and rows >= NA_KEY_ROWS and NA_KEY_ROWS % 2 == 0
    nt = rows // NA_TILE_ROWS
    u = np.arange(NA_TILE_ROWS * GRID_W) // GRID_W
    c = np.arange(NA_TILE_ROWS * GRID_W) % GRID_W
    ki = np.arange(NA_KEY_ROWS * GRID_W) // GRID_W
    kc = np.arange(NA_KEY_ROWS * GRID_W) % GRID_W
    cs = np.clip(c - NA_COLS // 2, 0, GRID_W - NA_COLS)
    col_ok = (kc[None, :] >= cs[:, None]) & (kc[None, :] < cs[:, None] + NA_COLS)
    dc = np.clip(kc[None, :] - c[:, None] + NA_COLS - 1, 0, 2 * NA_COLS - 2)
    types, type_of, kstart = [], [], []
    for t in range(nt):
        r0 = t * NA_TILE_ROWS
        k0 = int(np.clip(r0 - kh // 2, 0, rows - NA_KEY_ROWS))
        k0 -= k0 % 2
        r = r0 + u
        rs = np.clip(r - kh // 2, 0, rows - kh)
        kr = k0 + ki
        row_ok = (kr[None, :] >= rs[:, None]) & (kr[None, :] < rs[:, None] + kh)
        assert (rs >= k0).all() and (rs + kh <= k0 + NA_KEY_ROWS).all()
        dr = np.clip(kr[None, :] - r[:, None] + NA_ROWS - 1, 0, 2 * NA_ROWS - 2)
        key = (dr.tobytes(), row_ok.tobytes())
        for idx, (k_, _, _) in enumerate(types):
            if k_ == key:
                type_of.append(idx)
                break
        else:
            type_of.append(len(types))
            types.append((key, dr, row_ok & col_ok))
        kstart.append(k0)
    dr_rows = np.stack([t_[1][::GRID_W, ::GRID_W] for t_ in types])
    ok_all = np.stack([t_[2] for t_ in types])
    dc_cols = dc[:GRID_W, :GRID_W]
    return (np.asarray(type_of, np.int32), np.asarray(kstart, np.int32), dr_rows, dc_cols, ok_all)


def _na_bias(rpb, seq):
    _, _, dr_rows, dc_cols, ok = _na_tables(seq)
    nt = dr_rows.shape[0]
    hi = lax.Precision.HIGHEST
    col_sel = jnp.asarray(np.eye(2 * NA_COLS - 1, dtype=np.float32)[dc_cols])
    row_sel = jnp.asarray(np.eye(2 * NA_ROWS - 1, dtype=np.float32)[dr_rows])
    by_col = jnp.einsum("hdj,ckj->hdck", rpb.astype(F32), col_sel, precision=hi)
    bias = jnp.einsum("tuid,hdck->thucik", row_sel, by_col, precision=hi)
    bias = bias.reshape(nt, B_HEADS, NA_TILE_ROWS * GRID_W, NA_KEY_ROWS * GRID_W)
    return jnp.where(ok[:, None], bias * LOG2E, NEG)


def _na_kernel(type_ref, kstart_ref, q_ref, k_ref, v_ref, *rest, seq):
    del type_ref
    bias_refs = rest[:NA_STEP_TILES]
    g_ref, o_ref, ks_ref, vs_ref, s_ref, p_ref = rest[NA_STEP_TILES:]
    i = pl.program_id(1)
    lo = _lo_mask()
    chunk = 512

    @pl.when(i == 0)
    def _build():
        def body(c, carry):
            rows = pl.ds(pl.multiple_of(c * chunk, chunk), chunk)
            for src, dst in ((k_ref, ks_ref), (v_ref, vs_ref)):
                for p in range(2):
                    x = src[rows, p * LANES:(p + 1) * LANES]
                    dst[2 * p, rows, :] = jnp.where(lo, x, jnp.zeros_like(x))
                    dst[2 * p + 1, rows, :] = jnp.where(lo, jnp.zeros_like(x), x)
            return carry

        lax.fori_loop(0, seq // chunk, body, 0)

    nk = NA_KEY_ROWS * GRID_W
    tq = NA_TILE_ROWS * GRID_W
    units = [(t, head) for t in range(NA_STEP_TILES) for head in range(B_HEADS)]
    wins = [pl.ds(pl.multiple_of(kstart_ref[i * NA_STEP_TILES + t] * GRID_W, LANES), nk)
            for t in range(NA_STEP_TILES)]
    for u, (t, head) in enumerate(units):
        q2 = q_ref[t * tq:(t + 1) * tq, (head // 2) * LANES:(head // 2 + 1) * LANES]
        s_ref[u] = _dot_nt(q2, ks_ref[head, wins[t], :]) + bias_refs[t][head]
    invs = []
    for u in range(len(units)):
        s = s_ref[u]
        m = jnp.max(s, axis=-1, keepdims=True)
        e = jnp.exp2(s - m)
        invs.append(1.0 / jnp.sum(e, axis=-1, keepdims=True))
        p_ref[u] = e.astype(BF16)
    g = g_ref[...]
    for t in range(NA_STEP_TILES):
        outs = []
        for p in range(2):
            u = t * B_HEADS + 2 * p
            num = _dot(p_ref[u], vs_ref[2 * p, wins[t], :]) + _dot(p_ref[u + 1], vs_ref[2 * p + 1, wins[t], :])
            outs.append(num * jnp.where(lo, invs[u], invs[u + 1]))
        of = jnp.concatenate(outs, axis=1)
        o_ref[t * tq:(t + 1) * tq, :] = _group_rms(of, g).astype(o_ref.dtype)


def _neighborhood_attention(proj, bias, g):
    b, s, _ = proj.shape
    type_of, kstart, _, _, _ = _na_tables(s)
    tq = NA_TILE_ROWS * GRID_W
    nk = NA_KEY_ROWS * GRID_W
    step_q = NA_STEP_TILES * tq
    assert s % step_q == 0
    kern = functools.partial(_na_kernel, seq=s)

    def bias_spec(t):
        return pl.BlockSpec((None, B_HEADS, tq, nk), lambda bi, i, ty, ks: (ty[i * NA_STEP_TILES + t], 0, 0, 0))

    grid_spec = pltpu.PrefetchScalarGridSpec(
        num_scalar_prefetch=2,
        grid=(b, s // step_q),
        in_specs=[
            pl.BlockSpec((None, step_q, B_W), lambda bi, i, ty, ks: (bi, i, 3)),
            pl.BlockSpec((None, s, B_W), lambda bi, i, ty, ks: (bi, 0, 4)),
            pl.BlockSpec((None, s, B_W), lambda bi, i, ty, ks: (bi, 0, 5)),
            *[bias_spec(t) for t in range(NA_STEP_TILES)],
            pl.BlockSpec((1, B_W), lambda bi, i, ty, ks: (0, 0)),
        ],
        out_specs=pl.BlockSpec((None, step_q, B_W), lambda bi, i, ty, ks: (bi, i, 0)),
        scratch_shapes=[pltpu.VMEM((4, s, LANES), BF16), pltpu.VMEM((4, s, LANES), BF16),
                        pltpu.VMEM((NA_STEP_TILES * B_HEADS, tq, nk), F32),
                        pltpu.VMEM((NA_STEP_TILES * B_HEADS, tq, nk), BF16)],
    )
    return pl.pallas_call(
        kern,
        grid_spec=grid_spec,
        out_shape=jax.ShapeDtypeStruct((b, s, B_W), BF16),
        compiler_params=_params(2),
        name="neighborhood_attention",
    )(jnp.asarray(type_of), jnp.asarray(kstart), proj, proj, proj, *([bias] * NA_STEP_TILES), g)


def _mem_kernel(q_ref, mkv_ref, g_ref, o_ref, s_ref, p_ref):
    lo = _lo_mask()
    hi = jnp.logical_not(lo)

    def masked(col0, head):
        x = mkv_ref[:, col0 + (head // 2) * LANES:col0 + (head // 2 + 1) * LANES]
        return jnp.where(lo if head % 2 == 0 else hi, x, jnp.zeros_like(x))

    for head in range(M_HEADS):
        q2 = q_ref[:, (head // 2) * LANES:(head // 2 + 1) * LANES]
        s_ref[head] = _dot_nt(q2, masked(0, head))
    invs = []
    for head in range(M_HEADS):
        s = s_ref[head]
        m = jnp.max(s, axis=-1, keepdims=True)
        e = jnp.exp2(s - m)
        invs.append(1.0 / jnp.sum(e, axis=-1, keepdims=True))
        p_ref[head] = e.astype(BF16)
    outs = []
    for p in range(2):
        num = _dot(p_ref[2 * p], masked(M_W, 2 * p)) + _dot(p_ref[2 * p + 1], masked(M_W, 2 * p + 1))
        outs.append(num * jnp.where(lo, invs[2 * p], invs[2 * p + 1]))
    of = jnp.concatenate(outs, axis=1)
    o_ref[...] = _group_rms(of, g_ref[...]).astype(o_ref.dtype)


def _memory_attention(proj, mkv, g, tq=1024):
    b, s, _ = proj.shape
    mlen = mkv.shape[1]
    return pl.pallas_call(
        _mem_kernel,
        grid=(b, s // tq),
        in_specs=[
            pl.BlockSpec((None, tq, M_W), lambda bi, i: (bi, i, 6)),
            pl.BlockSpec((None, mlen, 2 * M_W), lambda bi, i: (bi, 0, 0)),
            pl.BlockSpec((1, M_W), lambda bi, i: (0, 0)),
        ],
        out_specs=pl.BlockSpec((None, tq, M_W), lambda bi, i: (bi, i, 0)),
        out_shape=jax.ShapeDtypeStruct((b, s, M_W), BF16),
        scratch_shapes=[pltpu.VMEM((M_HEADS, tq, mlen), F32), pltpu.VMEM((M_HEADS, tq, mlen), BF16)],
        compiler_params=_params(2),
        name="memory_attention",
    )(proj, mkv, g)


def _layer_norm(z, g, b):
    mu = jnp.mean(z, axis=-1, keepdims=True)
    zc = z - mu
    var = jnp.mean(zc * zc, axis=-1, keepdims=True)
    return zc * lax.rsqrt(var + LN_EPS) * g + b


def _route(lg_t, rbias):
    e, t = lg_t.shape
    eg = e // N_GROUPS
    scores = 1.0 / (1.0 + jnp.exp(-lg_t))
    sel = scores + rbias
    row = lax.broadcasted_iota(jnp.int32, (e, t), 0)
    best = None
    gidx = None
    for gi in range(N_GROUPS):
        r = [sel[gi * eg + k:gi * eg + k + 1, :] for k in range(eg)]
        top2 = None
        for a in range(eg):
            for b in range(a + 1, eg):
                pair = r[a] + r[b]
                top2 = pair if top2 is None else jnp.maximum(top2, pair)
        if best is None:
            best, gidx = top2, jnp.zeros((1, t), jnp.int32)
        else:
            better = top2 > best
            gidx = jnp.where(better, gi, gidx)
            best = jnp.maximum(best, top2)
    in_group = (row // eg) == gidx
    masked = jnp.where(in_group, sel, -jnp.inf)
    m1 = jnp.max(masked, axis=0, keepdims=True)
    i1 = jnp.min(jnp.where(masked == m1, row, e), axis=0, keepdims=True)
    rest = jnp.where(row == i1, -jnp.inf, masked)
    m2 = jnp.max(rest, axis=0, keepdims=True)
    i2 = jnp.min(jnp.where(rest == m2, row, e), axis=0, keepdims=True)
    pick1 = row == i1
    pick2 = row == i2
    w1 = jnp.sum(jnp.where(pick1, scores, 0.0), axis=0, keepdims=True)
    w2 = jnp.sum(jnp.where(pick2, scores, 0.0), axis=0, keepdims=True)
    tot = w1 + w2
    return i1, i2, pick1, pick2, w1 / tot, w2 / tot


def _out_kernel(ma_ref, mb_ref, mm_ref, w_ref, x_ref, g_ref, b_ref, wr_ref, rb_ref, tri_ref,
                x1_ref, info_ref, wcol_ref, cnt_ref, carry_ref):
    @pl.when(pl.program_id(0) == 0)
    def _init():
        carry_ref[...] = jnp.zeros_like(carry_ref)

    y = _dot(ma_ref[...], w_ref[0:A_Q_W, :])
    y = y + _dot(mb_ref[...], w_ref[A_Q_W:A_Q_W + B_W, :])
    y = y + _dot(mm_ref[...], w_ref[A_Q_W + B_W:, :])
    x1 = _layer_norm(ALPHA * x_ref[...] + y, g_ref[...], b_ref[...])
    x1_ref[...] = x1
    x_hi = x1.astype(BF16)
    x_lo = (x1 - x_hi.astype(F32)).astype(BF16)
    hi2 = _dot(x_hi, wr_ref[...])
    lg = hi2[:, 0:LANES] + hi2[:, LANES:] + _dot(x_lo, wr_ref[:, 0:LANES])
    lg_t = lg.T[0:N_EXPERTS, :]
    i1, i2, _, _, w1, w2 = _route(lg_t, rb_ref[...])
    tm = lg_t.shape[1]
    eg = N_EXPERTS // N_GROUPS
    a = jnp.bitwise_and(i1, eg - 1)
    b = jnp.bitwise_and(i2, eg - 1)
    lo = jnp.minimum(a, b)
    hi = jnp.maximum(a, b)
    pair = jnp.where(lo == 0, 0, jnp.where(lo == 1, 3, 5)) + hi - lo - 1
    cls = lax.shift_right_logical(i1, 2) * PAIRS_PER_GROUP + pair
    first_is_lo = a < b
    row = lax.broadcasted_iota(jnp.int32, (N_CLASSES, tm), 0)
    mine = row == cls
    member = jnp.where(mine, 1.0, 0.0)
    before = carry_ref[:, 0:1] + _dot(member.astype(BF16), tri_ref[...])
    rank = jnp.sum(jnp.where(mine, before, 0.0), axis=0, keepdims=True)
    total = carry_ref[...] + jnp.sum(member, axis=1, keepdims=True)
    carry_ref[...] = total
    cnt_ref[...] = total
    info_ref[...] = jnp.concatenate([cls, rank.astype(jnp.int32), jnp.zeros((6, tm), jnp.int32)], axis=0)
    w_pad = jnp.concatenate([jnp.where(first_is_lo, w1, w2), jnp.where(first_is_lo, w2, w1),
                             jnp.zeros((LANES - 2, tm), F32)], axis=0)
    wcol_ref[...] = w_pad.T


def _out_proj(ma, mb, mm, w_out, x, g, b, wr_cat, rbias, tm=1024):
    n = x.shape[0]
    row = lambda i: (i, 0)
    fix = lambda i: (0, 0)
    tri = jnp.asarray(np.triu(np.ones((tm, tm), np.float32), k=1), BF16)
    return pl.pallas_call(
        _out_kernel,
        grid=(n // tm,),
        in_specs=[
            pl.BlockSpec((tm, A_Q_W), row), pl.BlockSpec((tm, B_W), row), pl.BlockSpec((tm, M_W), row),
            pl.BlockSpec((D_MODEL, D_MODEL), fix), pl.BlockSpec((tm, D_MODEL), row),
            pl.BlockSpec((1, D_MODEL), fix), pl.BlockSpec((1, D_MODEL), fix),
            pl.BlockSpec((D_MODEL, 2 * LANES), fix), pl.BlockSpec((N_EXPERTS, 1), fix),
            pl.BlockSpec((tm, tm), fix),
        ],
        out_specs=[pl.BlockSpec((tm, D_MODEL), row), pl.BlockSpec((8, tm), lambda i: (0, i)),
                   pl.BlockSpec((tm, LANES), row), pl.BlockSpec((N_CLASSES, LANES), fix)],
        out_shape=[jax.ShapeDtypeStruct((n, D_MODEL), F32), jax.ShapeDtypeStruct((8, n), jnp.int32),
                   jax.ShapeDtypeStruct((n, LANES), F32), jax.ShapeDtypeStruct((N_CLASSES, LANES), F32)],
        scratch_shapes=[pltpu.VMEM((N_CLASSES, LANES), F32)],
        compiler_params=_params(1),
        name="out_proj_ln_router",
    )(ma, mb, mm, w_out, x, g, b, wr_cat, rbias, tri)


MOE_ROW_TILE = 256
MOE_TOKEN_TILE = 1024


SUBLANES = 8


def _index_copy(pos_hbm, idx_ref, isem, step, slot):
    width = pos_hbm.shape[1]
    dst = idx_ref.at[pl.ds(pl.multiple_of(slot * width, width), width)]
    return pltpu.make_async_copy(pos_hbm.at[step], dst, isem.at[slot])


assert D_MODEL == SUBLANES * LANES
ZERO_CHUNK = 64


XS_ROWS = 2 * SUBLANES


def _slab(ref, p, rows):
    return ref.at[pl.ds(pl.multiple_of(p * rows, rows), rows)]


def _token_tile(ref, p):
    return _slab(ref, p, SUBLANES)


def _store_token_tiles(dst_ref, x):
    t = x.shape[0]
    for c in range(SUBLANES):
        dst_ref[pl.ds(c, t, stride=SUBLANES), :] = x[:, c * LANES:(c + 1) * LANES]


def _load_token_tiles(src_ref, t):
    return jnp.concatenate([src_ref[pl.ds(c, t, stride=SUBLANES), :] for c in range(SUBLANES)], axis=1)


def _dispatch_kernel(lo_ref, hi_ref, pos_hbm, x_ref, w_ref, xs_hbm, idx_ref, xt_ref, z_ref, isem, rsem, zsem,
                     *, td):
    i = pl.program_id(0)
    n = pl.num_programs(0)
    slot = lax.rem(i, 2)

    @pl.when(i == 0)
    def _first():
        _index_copy(pos_hbm, idx_ref, isem, 0, 0).start()
        z_ref[...] = jnp.zeros_like(z_ref)
        one = z_ref.at[pl.ds(0, XS_ROWS)]

        def fill(e, carry):
            a = lo_ref[e]
            b = hi_ref[e]
            n_single = jnp.minimum(b - a, jnp.bitwise_and(-a, ZERO_CHUNK - 1))
            a2 = a + n_single
            n_chunk = lax.shift_right_logical(b - a2, 6)

            def single(r, c):
                pltpu.make_async_copy(one, _slab(xs_hbm, a + r, XS_ROWS), zsem).start()
                return c

            def chunk(r, c):
                dst = xs_hbm.at[pl.ds(pl.multiple_of((a2 + r * ZERO_CHUNK) * XS_ROWS, XS_ROWS),
                                      ZERO_CHUNK * XS_ROWS)]
                pltpu.make_async_copy(z_ref, dst, zsem).start()
                return c

            def single_wait(r, c):
                pltpu.make_async_copy(one, _slab(xs_hbm, 0, XS_ROWS), zsem).wait()
                return c

            def chunk_wait(r, c):
                pltpu.make_async_copy(z_ref, xs_hbm.at[pl.ds(0, ZERO_CHUNK * XS_ROWS)], zsem).wait()
                return c

            lax.fori_loop(0, n_single, single, 0)
            lax.fori_loop(0, n_chunk, chunk, 0)
            lax.fori_loop(0, n_single, single_wait, 0)
            lax.fori_loop(0, n_chunk, chunk_wait, 0)
            return carry

        lax.fori_loop(0, N_CLASSES, fill, 0)
        xt_ref[...] = jnp.zeros_like(xt_ref)

    _index_copy(pos_hbm, idx_ref, isem, i, slot).wait()

    @pl.when(i + 1 < n)
    def _next():
        _index_copy(pos_hbm, idx_ref, isem, i + 1, 1 - slot).start()

    xt = xt_ref.at[slot]
    x = x_ref[...]
    for c in range(SUBLANES):
        xt[pl.ds(c, td, stride=XS_ROWS), :] = x[:, c * LANES:(c + 1) * LANES]
    xt[pl.ds(SUBLANES, td, stride=XS_ROWS), :] = w_ref[...]

    def drain(sl):
        pltpu.make_async_copy(xt_ref.at[sl], xs_hbm.at[pl.ds(0, td * XS_ROWS)], rsem.at[sl]).wait()

    @pl.when(i > 0)
    def _prev():
        drain(1 - slot)

    base = slot * td

    def body(j, carry):
        for u in range(SUBLANES):
            t = j * SUBLANES + u
            p = idx_ref[base + t]
            pltpu.make_async_copy(_slab(xt, t, XS_ROWS), _slab(xs_hbm, p, XS_ROWS),
                                  rsem.at[slot]).start(priority=u % 2)
        return carry

    lax.fori_loop(0, td // SUBLANES, body, 0)

    @pl.when(i == n - 1)
    def _last():
        drain(slot)


def _dispatch(x1, wcol, pos_t, pad_lo, pad_hi, n_rows, td):
    n = x1.shape[0]
    kern = functools.partial(_dispatch_kernel, td=td)
    grid_spec = pltpu.PrefetchScalarGridSpec(
        num_scalar_prefetch=2,
        grid=(n // td,),
        in_specs=[pl.BlockSpec(memory_space=pl.ANY),
                  pl.BlockSpec((td, D_MODEL), lambda i, lo, hi: (i, 0)),
                  pl.BlockSpec((td, LANES), lambda i, lo, hi: (i, 0))],
        out_specs=pl.BlockSpec(memory_space=pl.ANY),
        scratch_shapes=[pltpu.SMEM((2 * td,), jnp.int32),
                        pltpu.VMEM((2, td * XS_ROWS, LANES), F32),
                        pltpu.VMEM((ZERO_CHUNK * XS_ROWS, LANES), F32),
                        pltpu.SemaphoreType.DMA((2,)), pltpu.SemaphoreType.DMA((2,)),
                        pltpu.SemaphoreType.DMA(())],
    )
    return pl.pallas_call(
        kern,
        grid_spec=grid_spec,
        out_shape=jax.ShapeDtypeStruct((n_rows * XS_ROWS, LANES), F32),
        compiler_params=_params(1),
        name="moe_dispatch",
    )(pad_lo, pad_hi, pos_t, x1, wcol)


def _gmm_kernel(elo_ref, ehi_ref, rows_ref, xs_ref, wg0_ref, wu0_ref, wd0_ref, wg1_ref, wu1_ref, wd1_ref, ys_ref):
    del elo_ref, ehi_ref
    nrows = rows_ref[pl.program_id(0)]

    @pl.when(nrows > 0)
    def _compute():
        t = xs_ref.shape[0] // XS_ROWS
        x = jnp.concatenate([xs_ref[pl.ds(c, t, stride=XS_ROWS), :] for c in range(SUBLANES)], axis=1)
        x = x.astype(BF16)
        w = xs_ref[pl.ds(SUBLANES, t, stride=XS_ROWS), :]
        y = None
        for k, (wg_ref, wu_ref, wd_ref) in enumerate(((wg0_ref, wu0_ref, wd0_ref), (wg1_ref, wu1_ref, wd1_ref))):
            gate = _dot(x, wg_ref[...])
            up = _dot(x, wu_ref[...])
            h = (gate * (1.0 / (1.0 + jnp.exp(-gate))) * up).astype(BF16)
            yk = w[:, k:k + 1] * _dot(h, wd_ref[...])
            y = yk if y is None else y + yk
        _store_token_tiles(ys_ref, y)

    @pl.when(nrows == 0)
    def _empty():
        ys_ref[...] = jnp.zeros_like(ys_ref)


def _grouped_mlp(xs, tile_lo, tile_hi, tile_rows, wg, wu, wd, layer, tm):
    n_rows = xs.shape[0] // XS_ROWS

    def wspec(shape, which):
        return pl.BlockSpec((None, None) + shape, lambda g, lo, hi, tr: (layer, (lo, hi)[which][g], 0, 0))

    grid_spec = pltpu.PrefetchScalarGridSpec(
        num_scalar_prefetch=3,
        grid=(n_rows // tm,),
        in_specs=[pl.BlockSpec((tm * XS_ROWS, LANES), lambda g, lo, hi, tr: (g, 0)),
                  wspec((D_MODEL, D_EXPERT), 0), wspec((D_MODEL, D_EXPERT), 0), wspec((D_EXPERT, D_MODEL), 0),
                  wspec((D_MODEL, D_EXPERT), 1), wspec((D_MODEL, D_EXPERT), 1), wspec((D_EXPERT, D_MODEL), 1)],
        out_specs=pl.BlockSpec((tm * SUBLANES, LANES), lambda g, lo, hi, tr: (g, 0)),
    )
    return pl.pallas_call(
        _gmm_kernel,
        grid_spec=grid_spec,
        out_shape=jax.ShapeDtypeStruct((n_rows * SUBLANES, LANES), F32),
        compiler_params=_params(1),
        name="moe_grouped_mlp",
    )(tile_lo, tile_hi, tile_rows, xs, wg, wu, wd, wg, wu, wd)


def _combine_kernel(pos_hbm, ys_hbm, x1_ref, g_ref, b_ref, o_ref, idx_ref, buf_ref, isem, rsem, *, td):
    i = pl.program_id(0)
    n = pl.num_programs(0)
    slot = lax.rem(i, 2)

    def issue_rows(sl):
        base = sl * td

        def body(j, carry):
            for u in range(SUBLANES):
                t = j * SUBLANES + u
                p = idx_ref[base + t]
                pltpu.make_async_copy(_token_tile(ys_hbm, p), _token_tile(buf_ref.at[sl], t),
                                      rsem.at[sl]).start(priority=u % 2)
            return carry

        lax.fori_loop(0, td // SUBLANES, body, 0)

    @pl.when(i == 0)
    def _first():
        _index_copy(pos_hbm, idx_ref, isem, 0, 0).start()
        _index_copy(pos_hbm, idx_ref, isem, 0, 0).wait()
        issue_rows(0)

        @pl.when(n > 1)
        def _():
            _index_copy(pos_hbm, idx_ref, isem, 1, 1).start()

    @pl.when(i + 1 < n)
    def _next():
        _index_copy(pos_hbm, idx_ref, isem, i + 1, 1 - slot).wait()
        issue_rows(1 - slot)

        @pl.when(i + 2 < n)
        def _():
            _index_copy(pos_hbm, idx_ref, isem, i + 2, slot).start()

    pltpu.make_async_copy(ys_hbm.at[pl.ds(0, td * SUBLANES)], buf_ref.at[slot], rsem.at[slot]).wait()
    y = _load_token_tiles(buf_ref.at[slot], td)
    o_ref[...] = _layer_norm(ALPHA * x1_ref[...] + y, g_ref[...], b_ref[...])


def _combine(ys, pos_t, x1, g, b, td):
    n = x1.shape[0]
    row = lambda i: (i, 0)
    fix = lambda i: (0, 0)
    kern = functools.partial(_combine_kernel, td=td)
    return pl.pallas_call(
        kern,
        grid=(n // td,),
        in_specs=[pl.BlockSpec(memory_space=pl.ANY), pl.BlockSpec(memory_space=pl.ANY),
                  pl.BlockSpec((td, D_MODEL), row),
                  pl.BlockSpec((1, D_MODEL), fix), pl.BlockSpec((1, D_MODEL), fix)],
        out_specs=pl.BlockSpec((td, D_MODEL), row),
        out_shape=jax.ShapeDtypeStruct((n, D_MODEL), F32),
        scratch_shapes=[pltpu.SMEM((2 * td,), jnp.int32),
                        pltpu.VMEM((2, td * SUBLANES, LANES), F32),
                        pltpu.SemaphoreType.DMA((2,)), pltpu.SemaphoreType.DMA((2,))],
        compiler_params=_params(1),
        name="moe_combine_ln2",
    )(pos_t, ys, x1, g, b)


def _moe_routed(x1, info, wcol, cnt, p):
    n = x1.shape[0]
    tm, td = MOE_ROW_TILE, MOE_TOKEN_TILE
    n_tiles = n // tm + N_CLASSES
    counts = cnt[:, 0].astype(jnp.int32)
    padded = (counts + (tm - 1)) // tm * tm
    ends = jnp.cumsum(padded)
    offs = ends - padded
    starts = jnp.arange(n_tiles, dtype=jnp.int32) * tm
    tc = jnp.sum((ends[None, :] <= starts[:, None]).astype(jnp.int32), axis=1)
    tile_class = jnp.minimum(tc, N_CLASSES - 1)
    classes = jnp.arange(N_CLASSES, dtype=jnp.int32)
    pick = tile_class[:, None] == classes[None, :]
    seg_end = jnp.sum(jnp.where(pick, (offs + counts)[None, :], 0), axis=1)
    tile_rows = jnp.where(tc < N_CLASSES, jnp.clip(seg_end - starts, 0, tm), 0).astype(jnp.int32)
    eg = N_EXPERTS // N_GROUPS
    pairs = [(a, b) for a in range(eg) for b in range(a + 1, eg)]
    class_lo = np.asarray([g * eg + a for g in range(N_GROUPS) for a, _ in pairs], np.int32)
    class_hi = np.asarray([g * eg + b for g in range(N_GROUPS) for _, b in pairs], np.int32)
    tile_lo = jnp.sum(jnp.where(pick, class_lo[None, :], 0), axis=1).astype(jnp.int32)
    tile_hi = jnp.sum(jnp.where(pick, class_hi[None, :], 0), axis=1).astype(jnp.int32)
    cls, rank = info[0], info[1]
    pos = rank + jnp.sum(jnp.where(cls[None, :] == classes[:, None], offs[:, None], 0), axis=0)
    pos_t = pos.reshape(n // td, td)
    pad_lo = offs + counts
    pad_hi = ends.at[N_CLASSES - 1].set(n_tiles * tm)
    xs = _dispatch(x1, wcol, pos_t, pad_lo, pad_hi, n_tiles * tm, td)
    ys = _grouped_mlp(xs, tile_lo, tile_hi, tile_rows, p["w_gate"], p["w_up"], p["w_down"], p["layer"], tm)
    return _combine(ys, pos_t, x1, p["ln2_g"], p["ln2_b"], td)


def _layer(x, mem, p):
    b, s, d = x.shape
    n = b * s
    xf = x.reshape(n, d)
    proj = _matmul(xf, p["w_in"], 1024, "in_proj").reshape(b, s, IN_W)
    mkv = _matmul(mem.reshape(-1, d), p["w_mem_kv"], 512, "mem_kv").reshape(b, -1, 2 * M_W)
    oa = _window_attention(proj, p["sink"], p["g_a"])
    ob = _neighborhood_attention(proj, p["na_bias"], p["g_b"])
    om = _memory_attention(proj, mkv, p["g_m"])
    x1, info, wcol, cnt = _out_proj(oa.reshape(n, A_Q_W), ob.reshape(n, B_W), om.reshape(n, M_W), p["w_out"], xf,
                                    p["ln1_g"], p["ln1_b"], p["wr_cat"], p["rbias"])
    y = _moe_routed(x1, info, wcol, cnt, p)
    return y.reshape(b, s, d)


def kernel(x_prompt, x_sample, mem_prompt, mem_sample, w_in, w_mem_kv, sink_logits, rpb, grp_norm_g, w_out,
           ln1_g, ln1_b, w_router, router_bias, w_gate, w_up, w_down, ln2_g, ln2_b):
    seq = x_prompt.shape[1]
    assert x_sample.shape[1] == seq
    wr = jnp.pad(w_router.astype(F32), ((0, 0), (0, LANES - N_EXPERTS)))
    wr_hi = wr.astype(BF16)
    wr_cat = jnp.concatenate([wr_hi, (wr - wr_hi.astype(F32)).astype(BF16)], axis=1)
    q_scale = np.ones((IN_W,), np.float32)
    for c0, width in ((0, A_Q_W), (A_Q_W + 2 * LANES, B_W), (IN_W - M_W, M_W)):
        q_scale[c0:c0 + width] = SCALE * LOG2E
    rbias = router_bias.astype(F32).reshape(N_EXPERTS, 1)
    w_gate_b, w_up_b, w_down_b = w_gate.astype(BF16), w_up.astype(BF16), w_down.astype(BF16)
    layers = []
    for l in range(DEPTH):
        g = grp_norm_g[l].astype(F32)
        layers.append(dict(
            w_in=(w_in[l] * q_scale).astype(BF16), w_mem_kv=w_mem_kv[l].astype(BF16),
            sink=sink_logits[l].astype(F32), na_bias=_na_bias(rpb[l], seq),
            g_a=g[:A_Q_W].reshape(1, -1), g_b=g[A_Q_W:A_Q_W + B_W].reshape(1, -1),
            g_m=g[A_Q_W + B_W:].reshape(1, -1),
            w_out=w_out[l].astype(BF16),
            ln1_g=ln1_g[l].astype(F32).reshape(1, -1), ln1_b=ln1_b[l].astype(F32).reshape(1, -1),
            wr_cat=wr_cat, rbias=rbias,
            w_gate=w_gate_b, w_up=w_up_b, w_down=w_down_b, layer=l,
            ln2_g=ln2_g[l].astype(F32).reshape(1, -1), ln2_b=ln2_b[l].astype(F32).reshape(1, -1)))

    def trunk(x, mem):
        for p in layers:
            x = _layer(x, mem, p)
        return x

    return (trunk(x_prompt, mem_prompt), trunk(x_sample, mem_sample))
```

```python
import functools

import jax
import jax.numpy as jnp
import numpy as np
from jax import lax
from jax.experimental import pallas as pl
from jax.experimental.pallas import tpu as pltpu

F32 = jnp.float32
BF16 = jnp.bfloat16

D_MODEL = 1024
DEPTH = 2
HEAD_DIM = 64
A_HEADS = 8
WINDOW = 128
B_HEADS = 4
GRID_W = 64
NA_ROWS = 8
NA_COLS = 16
M_HEADS = 4
N_EXPERTS = 16
N_GROUPS = 4
D_EXPERT = 512
PAIRS_PER_GROUP = 6
N_CLASSES = N_GROUPS * PAIRS_PER_GROUP
LN_EPS = 1e-5
A_Q_W = 512
B_W = 256
M_W = 256
IN_W = 1792
ALPHA = (2 * DEPTH) ** 0.25
SCALE = HEAD_DIM ** -0.5
LOG2E = 1.4426950408889634
NEG = -1e30

LANES = 128
SOFTMAX_HEADS = 2
NA_TILE_ROWS = 2
NA_STEP_TILES = 4
NA_KEY_ROWS = 10
VMEM_LIMIT = 56 * 1024 * 1024


def _params(n_axes, vmem=VMEM_LIMIT):
    return pltpu.CompilerParams(dimension_semantics=("arbitrary",) * n_axes, vmem_limit_bytes=vmem)


def _dot_nt(a, b):
    return lax.dot_general(a, b, (((1,), (1,)), ((), ())), preferred_element_type=F32)


def _dot(a, b):
    return jnp.dot(a, b, preferred_element_type=F32)


def _lo_mask():
    return lax.broadcasted_iota(jnp.int32, (1, LANES), 1) < HEAD_DIM


def _mm_kernel(x_ref, w_ref, o_ref):
    o_ref[...] = _dot(x_ref[...].astype(BF16), w_ref[...]).astype(o_ref.dtype)


def _matmul(x, w, tm, name):
    n, k = x.shape
    m = w.shape[1]
    tm = min(tm, n)
    assert n % tm == 0
    return pl.pallas_call(
        _mm_kernel,
        grid=(n // tm,),
        in_specs=[pl.BlockSpec((tm, k), lambda i: (i, 0)), pl.BlockSpec((k, m), lambda i: (0, 0))],
        out_specs=pl.BlockSpec((tm, m), lambda i: (i, 0)),
        out_shape=jax.ShapeDtypeStruct((n, m), BF16),
        compiler_params=_params(1),
        name=name,
    )(x, w)


def _group_rms(of, g):
    ms = jnp.mean(of * of, axis=-1, keepdims=True)
    return of * lax.rsqrt(ms + LN_EPS) * g


def _win_kernel(sink_ref, q_ref, k_ref, v_ref, g_ref, o_ref, ks_ref, vs_ref, bias_ref, s_ref, p_ref, d_ref, *,
                seq, tq):
    i = pl.program_id(1)
    lo = _lo_mask()
    chunk = 512

    @pl.when(i == 0)
    def _build():
        zeros = jnp.zeros((WINDOW, LANES), BF16)
        for t in range(4):
            for dst in (ks_ref, vs_ref):
                dst[t, 0:WINDOW, :] = zeros
                dst[t, seq + WINDOW:seq + 2 * WINDOW, :] = zeros

        def body(c, carry):
            r0 = pl.multiple_of(c * chunk, chunk)
            for src, dst in ((k_ref, ks_ref), (v_ref, vs_ref)):
                x = src[pl.ds(r0, chunk), :].astype(F32)
                xr = pltpu.roll(x, HEAD_DIM, 1)
                rows = pl.ds(r0 + WINDOW, chunk)
                dst[0, rows, :] = jnp.where(lo, x, 0.0).astype(BF16)
                dst[1, rows, :] = jnp.where(lo, 0.0, xr).astype(BF16)
                dst[2, rows, :] = jnp.where(lo, xr, 0.0).astype(BF16)
                dst[3, rows, :] = jnp.where(lo, 0.0, x).astype(BF16)
            return carry

        lax.fori_loop(0, seq // chunk, body, 0)

    kw = 3 * WINDOW

    @pl.when(jnp.logical_and(pl.program_id(0) == 0, i == 0))
    def _build_bias():
        a_i = lax.broadcasted_iota(jnp.int32, (WINDOW, kw), 0)
        j_i = lax.broadcasted_iota(jnp.int32, (WINDOW, kw), 1)
        dist = jnp.abs(a_i + WINDOW - j_i)
        absd = dist.astype(F32)
        for variant in range(3):
            valid = dist <= WINDOW
            if variant == 0:
                valid = valid & (j_i >= WINDOW)
            if variant == 2:
                valid = valid & (j_i < 2 * WINDOW)
            for head in range(A_HEADS):
                slope = 2.0 ** (-8.0 * (head + 1) / A_HEADS)
                bias_ref[variant * A_HEADS + head] = jnp.where(valid, (-slope * LOG2E) * absd, NEG)

    g = g_ref[...]

    nblk = tq // WINDOW
    wins, variants = [], []
    for j in range(nblk):
        qs = pl.multiple_of(i * tq + j * WINDOW, WINDOW)
        wins.append(pl.ds(qs, kw))
        variants.append(jnp.where(qs == 0, 0, jnp.where(qs == seq - WINDOW, 2, 1)))
    for j in range(nblk):
        for head in range(A_HEADS):
            c, t = divmod(head, 2)
            q2 = q_ref[j * WINDOW:(j + 1) * WINDOW, c * LANES:(c + 1) * LANES]
            s_ref[j * A_HEADS + head] = (_dot_nt(q2, ks_ref[2 * (c // 2) + t, wins[j], :])
                                         + bias_ref[variants[j] * A_HEADS + head])
    for j in range(nblk):
        for head in range(A_HEADS):
            s = s_ref[j * A_HEADS + head]
            sk = sink_ref[head] * LOG2E
            m = jnp.maximum(jnp.max(s, axis=-1, keepdims=True), sk)
            p = jnp.exp2(s - m)
            den = jnp.sum(p, axis=-1, keepdims=True) + jnp.exp2(sk - m)
            p_ref[j * A_HEADS + head] = p.astype(BF16)
            d_ref[j * A_HEADS + head] = jnp.broadcast_to(den, (WINDOW, LANES))
    for j in range(nblk):
        outs = []
        for c in range(4):
            h = c // 2
            u = j * A_HEADS + 2 * c
            num = (_dot(p_ref[u], vs_ref[2 * h, wins[j], :])
                   + _dot(p_ref[u + 1], vs_ref[2 * h + 1, wins[j], :]))
            outs.append(num / jnp.where(lo, d_ref[u], d_ref[u + 1]))
        of = jnp.concatenate(outs, axis=1)
        o_ref[j * WINDOW:(j + 1) * WINDOW, :] = _group_rms(of, g).astype(o_ref.dtype)


def _window_attention(proj, sink, g, tq=512):
    b, s, _ = proj.shape
    assert s % tq == 0 and s >= 2 * WINDOW
    kern = functools.partial(_win_kernel, seq=s, tq=tq)
    return pl.pallas_call(
        kern,
        grid=(b, s // tq),
        in_specs=[
            pl.BlockSpec(memory_space=pltpu.SMEM),
            pl.BlockSpec((None, tq, A_Q_W), lambda bi, i: (bi, i, 0)),
            pl.BlockSpec((None, s, LANES), lambda bi, i: (bi, 0, 4)),
            pl.BlockSpec((None, s, LANES), lambda bi, i: (bi, 0, 5)),
            pl.BlockSpec((1, A_Q_W), lambda bi, i: (0, 0)),
        ],
        out_specs=pl.BlockSpec((None, tq, A_Q_W), lambda bi, i: (bi, i, 0)),
        out_shape=jax.ShapeDtypeStruct((b, s, A_Q_W), BF16),
        scratch_shapes=[pltpu.VMEM((4, s + 2 * WINDOW, LANES), BF16),
                        pltpu.VMEM((4, s + 2 * WINDOW, LANES), BF16),
                        pltpu.VMEM((3 * A_HEADS, WINDOW, 3 * WINDOW), F32),
                        pltpu.VMEM((tq // WINDOW * A_HEADS, WINDOW, 3 * WINDOW), F32),
                        pltpu.VMEM((tq // WINDOW * A_HEADS, WINDOW, 3 * WINDOW), BF16),
                        pltpu.VMEM((tq // WINDOW * A_HEADS, WINDOW, LANES), F32)],
        compiler_params=_params(2),
        name="window_attention",
    )(sink, proj, proj, proj, g)


def _na_tables(seq):
    rows = seq // GRID_W
    kh = min(NA_ROWS, rows)
    assert rows % NA_TILE_ROWS == 0 and rows >= NA_KEY_ROWS and NA_KEY_ROWS % 2 == 0
    nt = rows // NA_TILE_ROWS
    u = np.arange(NA_TILE_ROWS * GRID_W) // GRID_W
    c = np.arange(NA_TILE_ROWS * GRID_W) % GRID_W
    ki = np.arange(NA_KEY_ROWS * GRID_W) // GRID_W
    kc = np.arange(NA_KEY_ROWS * GRID_W) % GRID_W
    cs = np.clip(c - NA_COLS // 2, 0, GRID_W - NA_COLS)
    col_ok = (kc[None, :] >= cs[:, None]) & (kc[None, :] < cs[:, None] + NA_COLS)
    dc = np.clip(kc[None, :] - c[:, None] + NA_COLS - 1, 0, 2 * NA_COLS - 2)
    types, type_of, kstart = [], [], []
    for t in range(nt):
        r0 = t * NA_TILE_ROWS
        k0 = int(np.clip(r0 - kh // 2, 0, rows - NA_KEY_ROWS))
        k0 -= k0 % 2
        r = r0 + u
        rs = np.clip(r - kh // 2, 0, rows - kh)
        kr = k0 + ki
        row_ok = (kr[None, :] >= rs[:, None]) & (kr[None, :] < rs[:, None] + kh)
        assert (rs >= k0).all() and (rs + kh <= k0 + NA_KEY_ROWS).all()
        dr = np.clip(kr[None, :] - r[:, None] + NA_ROWS - 1, 0, 2 * NA_ROWS - 2)
        key = (dr.tobytes(), row_ok.tobytes())
        for idx, (k_, _, _) in enumerate(types):
            if k_ == key:
                type_of.append(idx)
                break
        else:
            type_of.append(len(types))
            types.append((key, dr, row_ok & col_ok))
        kstart.append(k0)
    dr_rows = np.stack([t_[1][::GRID_W, ::GRID_W] for t_ in types])
    ok_all = np.stack([t_[2] for t_ in types])
    dc_cols = dc[:GRID_W, :GRID_W]
    return (np.asarray(type_of, np.int32), np.asarray(kstart, np.int32), dr_rows, dc_cols, ok_all)


def _na_bias(rpb, seq):
    _, _, dr_rows, dc_cols, ok = _na_tables(seq)
    nt = dr_rows.shape[0]
    hi = lax.Precision.HIGHEST
    col_sel = jnp.asarray(np.eye(2 * NA_COLS - 1, dtype=np.float32)[dc_cols])
    row_sel = jnp.asarray(np.eye(2 * NA_ROWS - 1, dtype=np.float32)[dr_rows])
    by_col = jnp.einsum("hdj,ckj->hdck", rpb.astype(F32), col_sel, precision=hi)
    bias = jnp.einsum("tuid,hdck->thucik", row_sel, by_col, precision=hi)
    bias = bias.reshape(nt, B_HEADS, NA_TILE_ROWS * GRID_W, NA_KEY_ROWS * GRID_W)
    return jnp.where(ok[:, None], bias * LOG2E, NEG)


def _na_kernel(type_ref, kstart_ref, q_ref, k_ref, v_ref, *rest, seq):
    del type_ref
    bias_refs = rest[:NA_STEP_TILES]
    g_ref, o_ref, ks_ref, vs_ref, s_ref, p_ref = rest[NA_STEP_TILES:]
    i = pl.program_id(1)
    lo = _lo_mask()
    chunk = 512

    @pl.when(i == 0)
    def _build():
        def body(c, carry):
            rows = pl.ds(pl.multiple_of(c * chunk, chunk), chunk)
            for src, dst in ((k_ref, ks_ref), (v_ref, vs_ref)):
                for p in range(2):
                    x = src[rows, p * LANES:(p + 1) * LANES]
                    dst[2 * p, rows, :] = jnp.where(lo, x, jnp.zeros_like(x))
                    dst[2 * p + 1, rows, :] = jnp.where(lo, jnp.zeros_like(x), x)
            return carry

        lax.fori_loop(0, seq // chunk, body, 0)

    nk = NA_KEY_ROWS * GRID_W
    tq = NA_TILE_ROWS * GRID_W
    units = [(t, head) for t in range(NA_STEP_TILES) for head in range(B_HEADS)]
    wins = [pl.ds(pl.multiple_of(kstart_ref[i * NA_STEP_TILES + t] * GRID_W, LANES), nk)
            for t in range(NA_STEP_TILES)]
    for u, (t, head) in enumerate(units):
        q2 = q_ref[t * tq:(t + 1) * tq, (head // 2) * LANES:(head // 2 + 1) * LANES]
        s_ref[u] = _dot_nt(q2, ks_ref[head, wins[t], :]) + bias_refs[t][head]
    invs = []
    for u in range(len(units)):
        s = s_ref[u]
        m = jnp.max(s, axis=-1, keepdims=True)
        e = jnp.exp2(s - m)
        invs.append(1.0 / jnp.sum(e, axis=-1, keepdims=True))
        p_ref[u] = e.astype(BF16)
    g = g_ref[...]
    for t in range(NA_STEP_TILES):
        outs = []
        for p in range(2):
            u = t * B_HEADS + 2 * p
            num = _dot(p_ref[u], vs_ref[2 * p, wins[t], :]) + _dot(p_ref[u + 1], vs_ref[2 * p + 1, wins[t], :])
            outs.append(num * jnp.where(lo, invs[u], invs[u + 1]))
        of = jnp.concatenate(outs, axis=1)
        o_ref[t * tq:(t + 1) * tq, :] = _group_rms(of, g).astype(o_ref.dtype)


def _neighborhood_attention(proj, bias, g):
    b, s, _ = proj.shape
    type_of, kstart, _, _, _ = _na_tables(s)
    tq = NA_TILE_ROWS * GRID_W
    nk = NA_KEY_ROWS * GRID_W
    step_q = NA_STEP_TILES * tq
    assert s % step_q == 0
    kern = functools.partial(_na_kernel, seq=s)

    def bias_spec(t):
        return pl.BlockSpec((None, B_HEADS, tq, nk), lambda bi, i, ty, ks: (ty[i * NA_STEP_TILES + t], 0, 0, 0))

    grid_spec = pltpu.PrefetchScalarGridSpec(
        num_scalar_prefetch=2,
        grid=(b, s // step_q),
        in_specs=[
            pl.BlockSpec((None, step_q, B_W), lambda bi, i, ty, ks: (bi, i, 3)),
            pl.BlockSpec((None, s, B_W), lambda bi, i, ty, ks: (bi, 0, 4)),
            pl.BlockSpec((None, s, B_W), lambda bi, i, ty, ks: (bi, 0, 5)),
            *[bias_spec(t) for t in range(NA_STEP_TILES)],
            pl.BlockSpec((1, B_W), lambda bi, i, ty, ks: (0, 0)),
        ],
        out_specs=pl.BlockSpec((None, step_q, B_W), lambda bi, i, ty, ks: (bi, i, 0)),
        scratch_shapes=[pltpu.VMEM((4, s, LANES), BF16), pltpu.VMEM((4, s, LANES), BF16),
                        pltpu.VMEM((NA_STEP_TILES * B_HEADS, tq, nk), F32),
                        pltpu.VMEM((NA_STEP_TILES * B_HEADS, tq, nk), BF16)],
    )
    return pl.pallas_call(
        kern,
        grid_spec=grid_spec,
        out_shape=jax.ShapeDtypeStruct((b, s, B_W), BF16),
        compiler_params=_params(2),
        name="neighborhood_attention",
    )(jnp.asarray(type_of), jnp.asarray(kstart), proj, proj, proj, *([bias] * NA_STEP_TILES), g)


def _mem_kernel(q_ref, mkv_ref, g_ref, o_ref, s_ref, p_ref):
    lo = _lo_mask()
    hi = jnp.logical_not(lo)

    def masked(col0, head):
        x = mkv_ref[:, col0 + (head // 2) * LANES:col0 + (head // 2 + 1) * LANES]
        return jnp.where(lo if head % 2 == 0 else hi, x, jnp.zeros_like(x))

    for head in range(M_HEADS):
        q2 = q_ref[:, (head // 2) * LANES:(head // 2 + 1) * LANES]
        s_ref[head] = _dot_nt(q2, masked(0, head))
    invs = []
    for head in range(M_HEADS):
        s = s_ref[head]
        m = jnp.max(s, axis=-1, keepdims=True)
        e = jnp.exp2(s - m)
        invs.append(1.0 / jnp.sum(e, axis=-1, keepdims=True))
        p_ref[head] = e.astype(BF16)
    outs = []
    for p in range(2):
        num = _dot(p_ref[2 * p], masked(M_W, 2 * p)) + _dot(p_ref[2 * p + 1], masked(M_W, 2 * p + 1))
        outs.append(num * jnp.where(lo, invs[2 * p], invs[2 * p + 1]))
    of = jnp.concatenate(outs, axis=1)
    o_ref[...] = _group_rms(of, g_ref[...]).astype(o_ref.dtype)


def _memory_attention(proj, mkv, g, tq=1024):
    b, s, _ = proj.shape
    mlen = mkv.shape[1]
    return pl.pallas_call(
        _mem_kernel,
        grid=(b, s // tq),
        in_specs=[
            pl.BlockSpec((None, tq, M_W), lambda bi, i: (bi, i, 6)),
            pl.BlockSpec((None, mlen, 2 * M_W), lambda bi, i: (bi, 0, 0)),
            pl.BlockSpec((1, M_W), lambda bi, i: (0, 0)),
        ],
        out_specs=pl.BlockSpec((None, tq, M_W), lambda bi, i: (bi, i, 0)),
        out_shape=jax.ShapeDtypeStruct((b, s, M_W), BF16),
        scratch_shapes=[pltpu.VMEM((M_HEADS, tq, mlen), F32), pltpu.VMEM((M_HEADS, tq, mlen), BF16)],
        compiler_params=_params(2),
        name="memory_attention",
    )(proj, mkv, g)


def _layer_norm(z, g, b):
    mu = jnp.mean(z, axis=-1, keepdims=True)
    zc = z - mu
    var = jnp.mean(zc * zc, axis=-1, keepdims=True)
    return zc * lax.rsqrt(var + LN_EPS) * g + b


def _route(lg_t, rbias):
    e, t = lg_t.shape
    eg = e // N_GROUPS
    scores = 1.0 / (1.0 + jnp.exp(-lg_t))
    sel = scores + rbias
    row = lax.broadcasted_iota(jnp.int32, (e, t), 0)
    best = None
    gidx = None
    for gi in range(N_GROUPS):
        r = [sel[gi * eg + k:gi * eg + k + 1, :] for k in range(eg)]
        top2 = None
        for a in range(eg):
            for b in range(a + 1, eg):
                pair = r[a] + r[b]
                top2 = pair if top2 is None else jnp.maximum(top2, pair)
        if best is None:
            best, gidx = top2, jnp.zeros((1, t), jnp.int32)
        else:
            better = top2 > best
            gidx = jnp.where(better, gi, gidx)
            best = jnp.maximum(best, top2)
    in_group = (row // eg) == gidx
    masked = jnp.where(in_group, sel, -jnp.inf)
    m1 = jnp.max(masked, axis=0, keepdims=True)
    i1 = jnp.min(jnp.where(masked == m1, row, e), axis=0, keepdims=True)
    rest = jnp.where(row == i1, -jnp.inf, masked)
    m2 = jnp.max(rest, axis=0, keepdims=True)
    i2 = jnp.min(jnp.where(rest == m2, row, e), axis=0, keepdims=True)
    pick1 = row == i1
    pick2 = row == i2
    w1 = jnp.sum(jnp.where(pick1, scores, 0.0), axis=0, keepdims=True)
    w2 = jnp.sum(jnp.where(pick2, scores, 0.0), axis=0, keepdims=True)
    tot = w1 + w2
    return i1, i2, pick1, pick2, w1 / tot, w2 / tot


def _out_kernel(ma_ref, mb_ref, mm_ref, w_ref, x_ref, g_ref, b_ref, wr_ref, rb_ref, tri_ref,
                x1_ref, info_ref, wcol_ref, cnt_ref, carry_ref):
    @pl.when(pl.program_id(0) == 0)
    def _init():
        carry_ref[...] = jnp.zeros_like(carry_ref)

    y = _dot(ma_ref[...], w_ref[0:A_Q_W, :])
    y = y + _dot(mb_ref[...], w_ref[A_Q_W:A_Q_W + B_W, :])
    y = y + _dot(mm_ref[...], w_ref[A_Q_W + B_W:, :])
    x1 = _layer_norm(ALPHA * x_ref[...] + y, g_ref[...], b_ref[...])
    x1_ref[...] = x1
    x_hi = x1.astype(BF16)
    x_lo = (x1 - x_hi.astype(F32)).astype(BF16)
    hi2 = _dot(x_hi, wr_ref[...])
    lg = hi2[:, 0:LANES] + hi2[:, LANES:] + _dot(x_lo, wr_ref[:, 0:LANES])
    lg_t = lg.T[0:N_EXPERTS, :]
    i1, i2, _, _, w1, w2 = _route(lg_t, rb_ref[...])
    tm = lg_t.shape[1]
    eg = N_EXPERTS // N_GROUPS
    a = jnp.bitwise_and(i1, eg - 1)
    b = jnp.bitwise_and(i2, eg - 1)
    lo = jnp.minimum(a, b)
    hi = jnp.maximum(a, b)
    pair = jnp.where(lo == 0, 0, jnp.where(lo == 1, 3, 5)) + hi - lo - 1
    cls = lax.shift_right_logical(i1, 2) * PAIRS_PER_GROUP + pair
    first_is_lo = a < b
    row = lax.broadcasted_iota(jnp.int32, (N_CLASSES, tm), 0)
    mine = row == cls
    member = jnp.where(mine, 1.0, 0.0)
    before = carry_ref[:, 0:1] + _dot(member.astype(BF16), tri_ref[...])
    rank = jnp.sum(jnp.where(mine, before, 0.0), axis=0, keepdims=True)
    total = carry_ref[...] + jnp.sum(member, axis=1, keepdims=True)
    carry_ref[...] = total
    cnt_ref[...] = total
    info_ref[...] = jnp.concatenate([cls, rank.astype(jnp.int32), jnp.zeros((6, tm), jnp.int32)], axis=0)
    w_pad = jnp.concatenate([jnp.where(first_is_lo, w1, w2), jnp.where(first_is_lo, w2, w1),
                             jnp.zeros((LANES - 2, tm), F32)], axis=0)
    wcol_ref[...] = w_pad.T


def _out_proj(ma, mb, mm, w_out, x, g, b, wr_cat, rbias, tm=1024):
    n = x.shape[0]
    row = lambda i: (i, 0)
    fix = lambda i: (0, 0)
    tri = jnp.asarray(np.triu(np.ones((tm, tm), np.float32), k=1), BF16)
    return pl.pallas_call(
        _out_kernel,
        grid=(n // tm,),
        in_specs=[
            pl.BlockSpec((tm, A_Q_W), row), pl.BlockSpec((tm, B_W), row), pl.BlockSpec((tm, M_W), row),
            pl.BlockSpec((D_MODEL, D_MODEL), fix), pl.BlockSpec((tm, D_MODEL), row),
            pl.BlockSpec((1, D_MODEL), fix), pl.BlockSpec((1, D_MODEL), fix),
            pl.BlockSpec((D_MODEL, 2 * LANES), fix), pl.BlockSpec((N_EXPERTS, 1), fix),
            pl.BlockSpec((tm, tm), fix),
        ],
        out_specs=[pl.BlockSpec((tm, D_MODEL), row), pl.BlockSpec((8, tm), lambda i: (0, i)),
                   pl.BlockSpec((tm, LANES), row), pl.BlockSpec((N_CLASSES, LANES), fix)],
        out_shape=[jax.ShapeDtypeStruct((n, D_MODEL), F32), jax.ShapeDtypeStruct((8, n), jnp.int32),
                   jax.ShapeDtypeStruct((n, LANES), F32), jax.ShapeDtypeStruct((N_CLASSES, LANES), F32)],
        scratch_shapes=[pltpu.VMEM((N_CLASSES, LANES), F32)],
        compiler_params=_params(1),
        name="out_proj_ln_router",
    )(ma, mb, mm, w_out, x, g, b, wr_cat, rbias, tri)


MOE_ROW_TILE = 256
MOE_TOKEN_TILE = 1024


SUBLANES = 8


def _index_copy(pos_hbm, idx_ref, isem, step, slot):
    width = pos_hbm.shape[1]
    dst = idx_ref.at[pl.ds(pl.multiple_of(slot * width, width), width)]
    return pltpu.make_async_copy(pos_hbm.at[step], dst, isem.at[slot])


assert D_MODEL == SUBLANES * LANES
ZERO_CHUNK = 64


XS_ROWS = SUBLANES + 1


def _slab(ref, p, rows):
    start = p * rows
    if rows % SUBLANES == 0:
        start = pl.multiple_of(start, SUBLANES)
    return ref.at[pl.ds(start, rows)]


def _token_tile(ref, p):
    return _slab(ref, p, SUBLANES)


def _store_token_tiles(dst_ref, x):
    t = x.shape[0]
    for c in range(SUBLANES):
        dst_ref[pl.ds(c, t, stride=SUBLANES), :] = x[:, c * LANES:(c + 1) * LANES]


def _load_token_tiles(src_ref, t):
    return jnp.concatenate([src_ref[pl.ds(c, t, stride=SUBLANES), :] for c in range(SUBLANES)], axis=1)


def _dispatch_kernel(lo_ref, hi_ref, pos_hbm, x_ref, w_ref, xs_hbm, idx_ref, xt_ref, z_ref, isem, rsem, zsem,
                     *, td):
    i = pl.program_id(0)
    n = pl.num_programs(0)
    slot = lax.rem(i, 2)

    @pl.when(i == 0)
    def _first():
        _index_copy(pos_hbm, idx_ref, isem, 0, 0).start()
        z_ref[...] = jnp.zeros_like(z_ref)
        one = z_ref.at[pl.ds(0, XS_ROWS)]

        def fill(e, carry):
            a = lo_ref[e]
            b = hi_ref[e]
            n_single = jnp.minimum(b - a, jnp.bitwise_and(-a, ZERO_CHUNK - 1))
            a2 = a + n_single
            n_chunk = lax.shift_right_logical(b - a2, 6)

            def single(r, c):
                pltpu.make_async_copy(one, _slab(xs_hbm, a + r, XS_ROWS), zsem).start()
                return c

            def chunk(r, c):
                dst = xs_hbm.at[pl.ds((a2 + r * ZERO_CHUNK) * XS_ROWS, ZERO_CHUNK * XS_ROWS)]
                pltpu.make_async_copy(z_ref, dst, zsem).start()
                return c

            def single_wait(r, c):
                pltpu.make_async_copy(one, _slab(xs_hbm, 0, XS_ROWS), zsem).wait()
                return c

            def chunk_wait(r, c):
                pltpu.make_async_copy(z_ref, xs_hbm.at[pl.ds(0, ZERO_CHUNK * XS_ROWS)], zsem).wait()
                return c

            lax.fori_loop(0, n_single, single, 0)
            lax.fori_loop(0, n_chunk, chunk, 0)
            lax.fori_loop(0, n_single, single_wait, 0)
            lax.fori_loop(0, n_chunk, chunk_wait, 0)
            return carry

        lax.fori_loop(0, N_CLASSES, fill, 0)

    _index_copy(pos_hbm, idx_ref, isem, i, slot).wait()

    @pl.when(i + 1 < n)
    def _next():
        _index_copy(pos_hbm, idx_ref, isem, i + 1, 1 - slot).start()

    xt = xt_ref.at[slot]
    x = x_ref[...]
    for c in range(SUBLANES):
        xt[pl.ds(c, td, stride=XS_ROWS), :] = x[:, c * LANES:(c + 1) * LANES]
    xt[pl.ds(SUBLANES, td, stride=XS_ROWS), :] = w_ref[...]

    def drain(sl):
        pltpu.make_async_copy(xt_ref.at[sl], xs_hbm.at[pl.ds(0, td * XS_ROWS)], rsem.at[sl]).wait()

    @pl.when(i > 0)
    def _prev():
        drain(1 - slot)

    base = slot * td

    def body(j, carry):
        for u in range(SUBLANES):
            t = j * SUBLANES + u
            p = idx_ref[base + t]
            pltpu.make_async_copy(_slab(xt, t, XS_ROWS), _slab(xs_hbm, p, XS_ROWS),
                                  rsem.at[slot]).start(priority=u % 2)
        return carry

    lax.fori_loop(0, td // SUBLANES, body, 0)

    @pl.when(i == n - 1)
    def _last():
        drain(slot)


def _dispatch(x1, wcol, pos_t, pad_lo, pad_hi, n_rows, td):
    n = x1.shape[0]
    kern = functools.partial(_dispatch_kernel, td=td)
    grid_spec = pltpu.PrefetchScalarGridSpec(
        num_scalar_prefetch=2,
        grid=(n // td,),
        in_specs=[pl.BlockSpec(memory_space=pl.ANY),
                  pl.BlockSpec((td, D_MODEL), lambda i, lo, hi: (i, 0)),
                  pl.BlockSpec((td, LANES), lambda i, lo, hi: (i, 0))],
        out_specs=pl.BlockSpec(memory_space=pl.ANY),
        scratch_shapes=[pltpu.SMEM((2 * td,), jnp.int32),
                        pltpu.VMEM((2, td * XS_ROWS, LANES), F32),
                        pltpu.VMEM((ZERO_CHUNK * XS_ROWS, LANES), F32),
                        pltpu.SemaphoreType.DMA((2,)), pltpu.SemaphoreType.DMA((2,)),
                        pltpu.SemaphoreType.DMA(())],
    )
    return pl.pallas_call(
        kern,
        grid_spec=grid_spec,
        out_shape=jax.ShapeDtypeStruct((n_rows * XS_ROWS, LANES), F32),
        compiler_params=_params(1),
        name="moe_dispatch",
    )(pad_lo, pad_hi, pos_t, x1, wcol)


def _gmm_kernel(elo_ref, ehi_ref, rows_ref, xs_ref, wg0_ref, wu0_ref, wd0_ref, wg1_ref, wu1_ref, wd1_ref, ys_ref):
    del elo_ref, ehi_ref
    nrows = rows_ref[pl.program_id(0)]

    @pl.when(nrows > 0)
    def _compute():
        t = xs_ref.shape[0] // XS_ROWS
        x = jnp.concatenate([xs_ref[pl.ds(c, t, stride=XS_ROWS), :] for c in range(SUBLANES)], axis=1)
        x = x.astype(BF16)
        w = xs_ref[pl.ds(SUBLANES, t, stride=XS_ROWS), :]
        y = None
        for k, (wg_ref, wu_ref, wd_ref) in enumerate(((wg0_ref, wu0_ref, wd0_ref), (wg1_ref, wu1_ref, wd1_ref))):
            gate = _dot(x, wg_ref[...])
            up = _dot(x, wu_ref[...])
            h = (gate * (1.0 / (1.0 + jnp.exp(-gate))) * up).astype(BF16)
            yk = w[:, k:k + 1] * _dot(h, wd_ref[...])
            y = yk if y is None else y + yk
        _store_token_tiles(ys_ref, y)

    @pl.when(nrows == 0)
    def _empty():
        ys_ref[...] = jnp.zeros_like(ys_ref)


def _grouped_mlp(xs, tile_lo, tile_hi, tile_rows, wg, wu, wd, layer, tm):
    n_rows = xs.shape[0] // XS_ROWS

    def wspec(shape, which):
        return pl.BlockSpec((None, None) + shape, lambda g, lo, hi, tr: (layer, (lo, hi)[which][g], 0, 0))

    grid_spec = pltpu.PrefetchScalarGridSpec(
        num_scalar_prefetch=3,
        grid=(n_rows // tm,),
        in_specs=[pl.BlockSpec((tm * XS_ROWS, LANES), lambda g, lo, hi, tr: (g, 0)),
                  wspec((D_MODEL, D_EXPERT), 0), wspec((D_MODEL, D_EXPERT), 0), wspec((D_EXPERT, D_MODEL), 0),
                  wspec((D_MODEL, D_EXPERT), 1), wspec((D_MODEL, D_EXPERT), 1), wspec((D_EXPERT, D_MODEL), 1)],
        out_specs=pl.BlockSpec((tm * SUBLANES, LANES), lambda g, lo, hi, tr: (g, 0)),
    )
    return pl.pallas_call(
        _gmm_kernel,
        grid_spec=grid_spec,
        out_shape=jax.ShapeDtypeStruct((n_rows * SUBLANES, LANES), F32),
        compiler_params=_params(1),
        name="moe_grouped_mlp",
    )(tile_lo, tile_hi, tile_rows, xs, wg, wu, wd, wg, wu, wd)


def _combine_kernel(pos_hbm, ys_hbm, x1_ref, g_ref, b_ref, o_ref, idx_ref, buf_ref, isem, rsem, *, td):
    i = pl.program_id(0)
    n = pl.num_programs(0)
    slot = lax.rem(i, 2)

    def issue_rows(sl):
        base = sl * td

        def body(j, carry):
            for u in range(SUBLANES):
                t = j * SUBLANES + u
                p = idx_ref[base + t]
                pltpu.make_async_copy(_token_tile(ys_hbm, p), _token_tile(buf_ref.at[sl], t),
                                      rsem.at[sl]).start(priority=u % 2)
            return carry

        lax.fori_loop(0, td // SUBLANES, body, 0)

    @pl.when(i == 0)
    def _first():
        _index_copy(pos_hbm, idx_ref, isem, 0, 0).start()
        _index_copy(pos_hbm, idx_ref, isem, 0, 0).wait()
        issue_rows(0)

        @pl.when(n > 1)
        def _():
            _index_copy(pos_hbm, idx_ref, isem, 1, 1).start()

    @pl.when(i + 1 < n)
    def _next():
        _index_copy(pos_hbm, idx_ref, isem, i + 1, 1 - slot).wait()
        issue_rows(1 - slot)

        @pl.when(i + 2 < n)
        def _():
            _index_copy(pos_hbm, idx_ref, isem, i + 2, slot).start()

    pltpu.make_async_copy(ys_hbm.at[pl.ds(0, td * SUBLANES)], buf_ref.at[slot], rsem.at[slot]).wait()
    y = _load_token_tiles(buf_ref.at[slot], td)
    o_ref[...] = _layer_norm(ALPHA * x1_ref[...] + y, g_ref[...], b_ref[...])


def _combine(ys, pos_t, x1, g, b, td):
    n = x1.shape[0]
    row = lambda i: (i, 0)
    fix = lambda i: (0, 0)
    kern = functools.partial(_combine_kernel, td=td)
    return pl.pallas_call(
        kern,
        grid=(n // td,),
        in_specs=[pl.BlockSpec(memory_space=pl.ANY), pl.BlockSpec(memory_space=pl.ANY),
                  pl.BlockSpec((td, D_MODEL), row),
                  pl.BlockSpec((1, D_MODEL), fix), pl.BlockSpec((1, D_MODEL), fix)],
        out_specs=pl.BlockSpec((td, D_MODEL), row),
        out_shape=jax.ShapeDtypeStruct((n, D_MODEL), F32),
        scratch_shapes=[pltpu.SMEM((2 * td,), jnp.int32),
                        pltpu.VMEM((2, td * SUBLANES, LANES), F32),
                        pltpu.SemaphoreType.DMA((2,)), pltpu.SemaphoreType.DMA((2,))],
        compiler_params=_params(1),
        name="moe_combine_ln2",
    )(pos_t, ys, x1, g, b)


def _moe_routed(x1, info, wcol, cnt, p):
    n = x1.shape[0]
    tm, td = MOE_ROW_TILE, MOE_TOKEN_TILE
    n_tiles = n // tm + N_CLASSES
    counts = cnt[:, 0].astype(jnp.int32)
    padded = (counts + (tm - 1)) // tm * tm
    ends = jnp.cumsum(padded)
    offs = ends - padded
    starts = jnp.arange(n_tiles, dtype=jnp.int32) * tm
    tc = jnp.sum((ends[None, :] <= starts[:, None]).astype(jnp.int32), axis=1)
    tile_class = jnp.minimum(tc, N_CLASSES - 1)
    classes = jnp.arange(N_CLASSES, dtype=jnp.int32)
    pick = tile_class[:, None] == classes[None, :]
    seg_end = jnp.sum(jnp.where(pick, (offs + counts)[None, :], 0), axis=1)
    tile_rows = jnp.where(tc < N_CLASSES, jnp.clip(seg_end - starts, 0, tm), 0).astype(jnp.int32)
    eg = N_EXPERTS // N_GROUPS
    pairs = [(a, b) for a in range(eg) for b in range(a + 1, eg)]
    class_lo = np.asarray([g * eg + a for g in range(N_GROUPS) for a, _ in pairs], np.int32)
    class_hi = np.asarray([g * eg + b for g in range(N_GROUPS) for _, b in pairs], np.int32)
    tile_lo = jnp.sum(jnp.where(pick, class_lo[None, :], 0), axis=1).astype(jnp.int32)
    tile_hi = jnp.sum(jnp.where(pick, class_hi[None, :], 0), axis=1).astype(jnp.int32)
    cls, rank = info[0], info[1]
    pos = rank + jnp.sum(jnp.where(cls[None, :] == classes[:, None], offs[:, None], 0), axis=0)
    pos_t = pos.reshape(n // td, td)
    pad_lo = offs + counts
    pad_hi = ends.at[N_CLASSES - 1].set(n_tiles * tm)
    xs = _dispatch(x1, wcol, pos_t, pad_lo, pad_hi, n_tiles * tm, td)
    ys = _grouped_mlp(xs, tile_lo, tile_hi, tile_rows, p["w_gate"], p["w_up"], p["w_down"], p["layer"], tm)
    return _combine(ys, pos_t, x1, p["ln2_g"], p["ln2_b"], td)


def _layer(x, mem, p):
    b, s, d = x.shape
    n = b * s
    xf = x.reshape(n, d)
    proj = _matmul(xf, p["w_in"], 1024, "in_proj").reshape(b, s, IN_W)
    mkv = _matmul(mem.reshape(-1, d), p["w_mem_kv"], 512, "mem_kv").reshape(b, -1, 2 * M_W)
    oa = _window_attention(proj, p["sink"], p["g_a"])
    ob = _neighborhood_attention(proj, p["na_bias"], p["g_b"])
    om = _memory_attention(proj, mkv, p["g_m"])
    x1, info, wcol, cnt = _out_proj(oa.reshape(n, A_Q_W), ob.reshape(n, B_W), om.reshape(n, M_W), p["w_out"], xf,
                                    p["ln1_g"], p["ln1_b"], p["wr_cat"], p["rbias"])
    y = _moe_routed(x1, info, wcol, cnt, p)
    return y.reshape(b, s, d)


def kernel(x_prompt, x_sample, mem_prompt, mem_sample, w_in, w_mem_kv, sink_logits, rpb, grp_norm_g, w_out,
           ln1_g, ln1_b, w_router, router_bias, w_gate, w_up, w_down, ln2_g, ln2_b):
    seq = x_prompt.shape[1]
    assert x_sample.shape[1] == seq
    wr = jnp.pad(w_router.astype(F32), ((0, 0), (0, LANES - N_EXPERTS)))
    wr_hi = wr.astype(BF16)
    wr_cat = jnp.concatenate([wr_hi, (wr - wr_hi.astype(F32)).astype(BF16)], axis=1)
    q_scale = np.ones((IN_W,), np.float32)
    for c0, width in ((0, A_Q_W), (A_Q_W + 2 * LANES, B_W), (IN_W - M_W, M_W)):
        q_scale[c0:c0 + width] = SCALE * LOG2E
    rbias = router_bias.astype(F32).reshape(N_EXPERTS, 1)
    w_gate_b, w_up_b, w_down_b = w_gate.astype(BF16), w_up.astype(BF16), w_down.astype(BF16)
    layers = []
    for l in range(DEPTH):
        g = grp_norm_g[l].astype(F32)
        layers.append(dict(
            w_in=(w_in[l] * q_scale).astype(BF16), w_mem_kv=w_mem_kv[l].astype(BF16),
            sink=sink_logits[l].astype(F32), na_bias=_na_bias(rpb[l], seq),
            g_a=g[:A_Q_W].reshape(1, -1), g_b=g[A_Q_W:A_Q_W + B_W].reshape(1, -1),
            g_m=g[A_Q_W + B_W:].reshape(1, -1),
            w_out=w_out[l].astype(BF16),
            ln1_g=ln1_g[l].astype(F32).reshape(1, -1), ln1_b=ln1_b[l].astype(F32).reshape(1, -1),
            wr_cat=wr_cat, rbias=rbias,
            w_gate=w_gate_b, w_up=w_up_b, w_down=w_down_b, layer=l,
            ln2_g=ln2_g[l].astype(F32).reshape(1, -1), ln2_b=ln2_b[l].astype(F32).reshape(1, -1)))

    def trunk(x, mem):
        for p in layers:
            x = _layer(x, mem, p)
        return x

    return (trunk(x_prompt, mem_prompt), trunk(x_sample, mem_sample))
```

```python
import functools

import jax
import jax.numpy as jnp
import numpy as np
from jax import lax
from jax.experimental import pallas as pl
from jax.experimental.pallas import tpu as pltpu

F32 = jnp.float32
BF16 = jnp.bfloat16

D_MODEL = 1024
DEPTH = 2
HEAD_DIM = 64
A_HEADS = 8
WINDOW = 128
B_HEADS = 4
GRID_W = 64
NA_ROWS = 8
NA_COLS = 16
M_HEADS = 4
N_EXPERTS = 16
N_GROUPS = 4
D_EXPERT = 512
PAIRS_PER_GROUP = 6
N_CLASSES = N_GROUPS * PAIRS_PER_GROUP
LN_EPS = 1e-5
A_Q_W = 512
B_W = 256
M_W = 256
IN_W = 1792
ALPHA = (2 * DEPTH) ** 0.25
SCALE = HEAD_DIM ** -0.5
LOG2E = 1.4426950408889634
NEG = -1e30

LANES = 128
NA_TILE_ROWS = 2
NA_STEP_TILES = 4
NA_KEY_ROWS = 10
VMEM_LIMIT = 56 * 1024 * 1024


def _params(n_axes, vmem=VMEM_LIMIT):
    return pltpu.CompilerParams(dimension_semantics=("arbitrary",) * n_axes, vmem_limit_bytes=vmem)


def _dot_nt(a, b):
    return lax.dot_general(a, b, (((1,), (1,)), ((), ())), preferred_element_type=F32)


def _dot(a, b):
    return jnp.dot(a, b, preferred_element_type=F32)


def _lo_mask():
    return lax.broadcasted_iota(jnp.int32, (1, LANES), 1) < HEAD_DIM


def _mm_kernel(x_ref, w_ref, o_ref):
    o_ref[...] = _dot(x_ref[...].astype(BF16), w_ref[...]).astype(o_ref.dtype)


def _matmul(x, w, tm, name):
    n, k = x.shape
    m = w.shape[1]
    tm = min(tm, n)
    assert n % tm == 0
    return pl.pallas_call(
        _mm_kernel,
        grid=(n // tm,),
        in_specs=[pl.BlockSpec((tm, k), lambda i: (i, 0)), pl.BlockSpec((k, m), lambda i: (0, 0))],
        out_specs=pl.BlockSpec((tm, m), lambda i: (i, 0)),
        out_shape=jax.ShapeDtypeStruct((n, m), BF16),
        compiler_params=_params(1),
        name=name,
    )(x, w)


def _group_rms(of, g):
    ms = jnp.mean(of * of, axis=-1, keepdims=True)
    return of * lax.rsqrt(ms + LN_EPS) * g


def _win_kernel(sink_ref, q_ref, k_ref, v_ref, g_ref, o_ref, ks_ref, vs_ref, bias_ref, s_ref, p_ref, d_ref, *,
                seq, tq):
    i = pl.program_id(1)
    lo = _lo_mask()
    chunk = 512

    @pl.when(i == 0)
    def _build():
        zeros = jnp.zeros((WINDOW, LANES), BF16)
        for t in range(4):
            for dst in (ks_ref, vs_ref):
                dst[t, 0:WINDOW, :] = zeros
                dst[t, seq + WINDOW:seq + 2 * WINDOW, :] = zeros

        def body(c, carry):
            r0 = pl.multiple_of(c * chunk, chunk)
            for src, dst in ((k_ref, ks_ref), (v_ref, vs_ref)):
                x = src[pl.ds(r0, chunk), :].astype(F32)
                xr = pltpu.roll(x, HEAD_DIM, 1)
                rows = pl.ds(r0 + WINDOW, chunk)
                dst[0, rows, :] = jnp.where(lo, x, 0.0).astype(BF16)
                dst[1, rows, :] = jnp.where(lo, 0.0, xr).astype(BF16)
                dst[2, rows, :] = jnp.where(lo, xr, 0.0).astype(BF16)
                dst[3, rows, :] = jnp.where(lo, 0.0, x).astype(BF16)
            return carry

        lax.fori_loop(0, seq // chunk, body, 0)

    kw = 3 * WINDOW

    @pl.when(jnp.logical_and(pl.program_id(0) == 0, i == 0))
    def _build_bias():
        a_i = lax.broadcasted_iota(jnp.int32, (WINDOW, kw), 0)
        j_i = lax.broadcasted_iota(jnp.int32, (WINDOW, kw), 1)
        dist = jnp.abs(a_i + WINDOW - j_i)
        absd = dist.astype(F32)
        for variant in range(3):
            valid = dist <= WINDOW
            if variant == 0:
                valid = valid & (j_i >= WINDOW)
            if variant == 2:
                valid = valid & (j_i < 2 * WINDOW)
            for head in range(A_HEADS):
                slope = 2.0 ** (-8.0 * (head + 1) / A_HEADS)
                bias_ref[variant * A_HEADS + head] = jnp.where(valid, (-slope * LOG2E) * absd, NEG)

    g = g_ref[...]

    nblk = tq // WINDOW
    wins, variants = [], []
    for j in range(nblk):
        qs = pl.multiple_of(i * tq + j * WINDOW, WINDOW)
        wins.append(pl.ds(qs, kw))
        variants.append(jnp.where(qs == 0, 0, jnp.where(qs == seq - WINDOW, 2, 1)))
    for j in range(nblk):
        for head in range(A_HEADS):
            c, t = divmod(head, 2)
            q2 = q_ref[j * WINDOW:(j + 1) * WINDOW, c * LANES:(c + 1) * LANES]
            s_ref[j * A_HEADS + head] = (_dot_nt(q2, ks_ref[2 * (c // 2) + t, wins[j], :])
                                         + bias_ref[variants[j] * A_HEADS + head])
    for j in range(nblk):
        for head in range(A_HEADS):
            s = s_ref[j * A_HEADS + head]
            sk = sink_ref[head] * LOG2E
            m = jnp.maximum(jnp.max(s, axis=-1, keepdims=True), sk)
            p = jnp.exp2(s - m)
            den = jnp.sum(p, axis=-1, keepdims=True) + jnp.exp2(sk - m)
            p_ref[j * A_HEADS + head] = p.astype(BF16)
            d_ref[j * A_HEADS + head] = jnp.broadcast_to(den, (WINDOW, LANES))
    for j in range(nblk):
        outs = []
        for c in range(4):
            h = c // 2
            u = j * A_HEADS + 2 * c
            num = (_dot(p_ref[u], vs_ref[2 * h, wins[j], :])
                   + _dot(p_ref[u + 1], vs_ref[2 * h + 1, wins[j], :]))
            outs.append(num / jnp.where(lo, d_ref[u], d_ref[u + 1]))
        of = jnp.concatenate(outs, axis=1)
        o_ref[j * WINDOW:(j + 1) * WINDOW, :] = _group_rms(of, g).astype(o_ref.dtype)


def _window_attention(proj, sink, g, tq=1024):
    b, s, _ = proj.shape
    assert s % tq == 0 and s >= 2 * WINDOW
    kern = functools.partial(_win_kernel, seq=s, tq=tq)
    return pl.pallas_call(
        kern,
        grid=(b, s // tq),
        in_specs=[
            pl.BlockSpec(memory_space=pltpu.SMEM),
            pl.BlockSpec((None, tq, A_Q_W), lambda bi, i: (bi, i, 0)),
            pl.BlockSpec((None, s, LANES), lambda bi, i: (bi, 0, 4)),
            pl.BlockSpec((None, s, LANES), lambda bi, i: (bi, 0, 5)),
            pl.BlockSpec((1, A_Q_W), lambda bi, i: (0, 0)),
        ],
        out_specs=pl.BlockSpec((None, tq, A_Q_W), lambda bi, i: (bi, i, 0)),
        out_shape=jax.ShapeDtypeStruct((b, s, A_Q_W), BF16),
        scratch_shapes=[pltpu.VMEM((4, s + 2 * WINDOW, LANES), BF16),
                        pltpu.VMEM((4, s + 2 * WINDOW, LANES), BF16),
                        pltpu.VMEM((3 * A_HEADS, WINDOW, 3 * WINDOW), F32),
                        pltpu.VMEM((tq // WINDOW * A_HEADS, WINDOW, 3 * WINDOW), F32),
                        pltpu.VMEM((tq // WINDOW * A_HEADS, WINDOW, 3 * WINDOW), BF16),
                        pltpu.VMEM((tq // WINDOW * A_HEADS, WINDOW, LANES), F32)],
        compiler_params=_params(2),
        name="window_attention",
    )(sink, proj, proj, proj, g)


def _na_tables(seq):
    rows = seq // GRID_W
    kh = min(NA_ROWS, rows)
    assert rows % NA_TILE_ROWS == 0 and rows >= NA_KEY_ROWS and NA_KEY_ROWS % 2 == 0
    nt = rows // NA_TILE_ROWS
    u = np.arange(NA_TILE_ROWS * GRID_W) // GRID_W
    c = np.arange(NA_TILE_ROWS * GRID_W) % GRID_W
    ki = np.arange(NA_KEY_ROWS * GRID_W) // GRID_W
    kc = np.arange(NA_KEY_ROWS * GRID_W) % GRID_W
    cs = np.clip(c - NA_COLS // 2, 0, GRID_W - NA_COLS)
    col_ok = (kc[None, :] >= cs[:, None]) & (kc[None, :] < cs[:, None] + NA_COLS)
    dc = np.clip(kc[None, :] - c[:, None] + NA_COLS - 1, 0, 2 * NA_COLS - 2)
    types, type_of, kstart = [], [], []
    for t in range(nt):
        r0 = t * NA_TILE_ROWS
        k0 = int(np.clip(r0 - kh // 2, 0, rows - NA_KEY_ROWS))
        k0 -= k0 % 2
        r = r0 + u
        rs = np.clip(r - kh // 2, 0, rows - kh)
        kr = k0 + ki
        row_ok = (kr[None, :] >= rs[:, None]) & (kr[None, :] < rs[:, None] + kh)
        assert (rs >= k0).all() and (rs + kh <= k0 + NA_KEY_ROWS).all()
        dr = np.clip(kr[None, :] - r[:, None] + NA_ROWS - 1, 0, 2 * NA_ROWS - 2)
        key = (dr.tobytes(), row_ok.tobytes())
        for idx, (k_, _, _) in enumerate(types):
            if k_ == key:
                type_of.append(idx)
                break
        else:
            type_of.append(len(types))
            types.append((key, dr, row_ok & col_ok))
        kstart.append(k0)
    dr_rows = np.stack([t_[1][::GRID_W, ::GRID_W] for t_ in types])
    ok_all = np.stack([t_[2] for t_ in types])
    dc_cols = dc[:GRID_W, :GRID_W]
    return (np.asarray(type_of, np.int32), np.asarray(kstart, np.int32), dr_rows, dc_cols, ok_all)


def _na_bias(rpb, seq):
    _, _, dr_rows, dc_cols, ok = _na_tables(seq)
    nt = dr_rows.shape[0]
    hi = lax.Precision.HIGHEST
    col_sel = jnp.asarray(np.eye(2 * NA_COLS - 1, dtype=np.float32)[dc_cols])
    row_sel = jnp.asarray(np.eye(2 * NA_ROWS - 1, dtype=np.float32)[dr_rows])
    by_col = jnp.einsum("hdj,ckj->hdck", rpb.astype(F32), col_sel, precision=hi)
    bias = jnp.einsum("tuid,hdck->thucik", row_sel, by_col, precision=hi)
    bias = bias.reshape(nt, B_HEADS, NA_TILE_ROWS * GRID_W, NA_KEY_ROWS * GRID_W)
    return jnp.where(ok[:, None], bias * LOG2E, NEG)


def _na_kernel(type_ref, kstart_ref, q_ref, k_ref, v_ref, *rest, seq):
    del type_ref
    bias_refs = rest[:NA_STEP_TILES]
    g_ref, o_ref, ks_ref, vs_ref, s_ref, p_ref = rest[NA_STEP_TILES:]
    i = pl.program_id(1)
    lo = _lo_mask()
    chunk = 512

    @pl.when(i == 0)
    def _build():
        def body(c, carry):
            rows = pl.ds(pl.multiple_of(c * chunk, chunk), chunk)
            for src, dst in ((k_ref, ks_ref), (v_ref, vs_ref)):
                for p in range(2):
                    x = src[rows, p * LANES:(p + 1) * LANES]
                    dst[2 * p, rows, :] = jnp.where(lo, x, jnp.zeros_like(x))
                    dst[2 * p + 1, rows, :] = jnp.where(lo, jnp.zeros_like(x), x)
            return carry

        lax.fori_loop(0, seq // chunk, body, 0)

    nk = NA_KEY_ROWS * GRID_W
    tq = NA_TILE_ROWS * GRID_W
    units = [(t, head) for t in range(NA_STEP_TILES) for head in range(B_HEADS)]
    wins = [pl.ds(pl.multiple_of(kstart_ref[i * NA_STEP_TILES + t] * GRID_W, LANES), nk)
            for t in range(NA_STEP_TILES)]
    for u, (t, head) in enumerate(units):
        q2 = q_ref[t * tq:(t + 1) * tq, (head // 2) * LANES:(head // 2 + 1) * LANES]
        s_ref[u] = _dot_nt(q2, ks_ref[head, wins[t], :]) + bias_refs[t][head]
    invs = []
    for u in range(len(units)):
        s = s_ref[u]
        m = jnp.max(s, axis=-1, keepdims=True)
        e = jnp.exp2(s - m)
        invs.append(1.0 / jnp.sum(e, axis=-1, keepdims=True))
        p_ref[u] = e.astype(BF16)
    g = g_ref[...]
    for t in range(NA_STEP_TILES):
        outs = []
        for p in range(2):
            u = t * B_HEADS + 2 * p
            num = _dot(p_ref[u], vs_ref[2 * p, wins[t], :]) + _dot(p_ref[u + 1], vs_ref[2 * p + 1, wins[t], :])
            outs.append(num * jnp.where(lo, invs[u], invs[u + 1]))
        of = jnp.concatenate(outs, axis=1)
        o_ref[t * tq:(t + 1) * tq, :] = _group_rms(of, g).astype(o_ref.dtype)


def _neighborhood_attention(proj, bias, g):
    b, s, _ = proj.shape
    type_of, kstart, _, _, _ = _na_tables(s)
    tq = NA_TILE_ROWS * GRID_W
    nk = NA_KEY_ROWS * GRID_W
    step_q = NA_STEP_TILES * tq
    assert s % step_q == 0
    kern = functools.partial(_na_kernel, seq=s)

    def bias_spec(t):
        return pl.BlockSpec((None, B_HEADS, tq, nk), lambda bi, i, ty, ks: (ty[i * NA_STEP_TILES + t], 0, 0, 0))

    grid_spec = pltpu.PrefetchScalarGridSpec(
        num_scalar_prefetch=2,
        grid=(b, s // step_q),
        in_specs=[
            pl.BlockSpec((None, step_q, B_W), lambda bi, i, ty, ks: (bi, i, 3)),
            pl.BlockSpec((None, s, B_W), lambda bi, i, ty, ks: (bi, 0, 4)),
            pl.BlockSpec((None, s, B_W), lambda bi, i, ty, ks: (bi, 0, 5)),
            *[bias_spec(t) for t in range(NA_STEP_TILES)],
            pl.BlockSpec((1, B_W), lambda bi, i, ty, ks: (0, 0)),
        ],
        out_specs=pl.BlockSpec((None, step_q, B_W), lambda bi, i, ty, ks: (bi, i, 0)),
        scratch_shapes=[pltpu.VMEM((4, s, LANES), BF16), pltpu.VMEM((4, s, LANES), BF16),
                        pltpu.VMEM((NA_STEP_TILES * B_HEADS, tq, nk), F32),
                        pltpu.VMEM((NA_STEP_TILES * B_HEADS, tq, nk), BF16)],
    )
    return pl.pallas_call(
        kern,
        grid_spec=grid_spec,
        out_shape=jax.ShapeDtypeStruct((b, s, B_W), BF16),
        compiler_params=_params(2),
        name="neighborhood_attention",
    )(jnp.asarray(type_of), jnp.asarray(kstart), proj, proj, proj, *([bias] * NA_STEP_TILES), g)


def _mem_kernel(q_ref, mkv_ref, g_ref, o_ref, s_ref, p_ref):
    lo = _lo_mask()
    hi = jnp.logical_not(lo)

    def masked(col0, head):
        x = mkv_ref[:, col0 + (head // 2) * LANES:col0 + (head // 2 + 1) * LANES]
        return jnp.where(lo if head % 2 == 0 else hi, x, jnp.zeros_like(x))

    for head in range(M_HEADS):
        q2 = q_ref[:, (head // 2) * LANES:(head // 2 + 1) * LANES]
        s_ref[head] = _dot_nt(q2, masked(0, head))
    invs = []
    for head in range(M_HEADS):
        s = s_ref[head]
        m = jnp.max(s, axis=-1, keepdims=True)
        e = jnp.exp2(s - m)
        invs.append(1.0 / jnp.sum(e, axis=-1, keepdims=True))
        p_ref[head] = e.astype(BF16)
    outs = []
    for p in range(2):
        num = _dot(p_ref[2 * p], masked(M_W, 2 * p)) + _dot(p_ref[2 * p + 1], masked(M_W, 2 * p + 1))
        outs.append(num * jnp.where(lo, invs[2 * p], invs[2 * p + 1]))
    of = jnp.concatenate(outs, axis=1)
    o_ref[...] = _group_rms(of, g_ref[...]).astype(o_ref.dtype)


def _memory_attention(proj, mkv, g, tq=1024):
    b, s, _ = proj.shape
    mlen = mkv.shape[1]
    return pl.pallas_call(
        _mem_kernel,
        grid=(b, s // tq),
        in_specs=[
            pl.BlockSpec((None, tq, M_W), lambda bi, i: (bi, i, 6)),
            pl.BlockSpec((None, mlen, 2 * M_W), lambda bi, i: (bi, 0, 0)),
            pl.BlockSpec((1, M_W), lambda bi, i: (0, 0)),
        ],
        out_specs=pl.BlockSpec((None, tq, M_W), lambda bi, i: (bi, i, 0)),
        out_shape=jax.ShapeDtypeStruct((b, s, M_W), BF16),
        scratch_shapes=[pltpu.VMEM((M_HEADS, tq, mlen), F32), pltpu.VMEM((M_HEADS, tq, mlen), BF16)],
        compiler_params=_params(2),
        name="memory_attention",
    )(proj, mkv, g)


def _layer_norm(z, g, b):
    mu = jnp.mean(z, axis=-1, keepdims=True)
    zc = z - mu
    var = jnp.mean(zc * zc, axis=-1, keepdims=True)
    return zc * lax.rsqrt(var + LN_EPS) * g + b


def _route(lg_t, rbias):
    e, t = lg_t.shape
    eg = e // N_GROUPS
    scores = 1.0 / (1.0 + jnp.exp(-lg_t))
    sel = scores + rbias
    row = lax.broadcasted_iota(jnp.int32, (e, t), 0)
    best = None
    gidx = None
    for gi in range(N_GROUPS):
        r = [sel[gi * eg + k:gi * eg + k + 1, :] for k in range(eg)]
        top2 = None
        for a in range(eg):
            for b in range(a + 1, eg):
                pair = r[a] + r[b]
                top2 = pair if top2 is None else jnp.maximum(top2, pair)
        if best is None:
            best, gidx = top2, jnp.zeros((1, t), jnp.int32)
        else:
            better = top2 > best
            gidx = jnp.where(better, gi, gidx)
            best = jnp.maximum(best, top2)
    in_group = (row // eg) == gidx
    masked = jnp.where(in_group, sel, -jnp.inf)
    m1 = jnp.max(masked, axis=0, keepdims=True)
    i1 = jnp.min(jnp.where(masked == m1, row, e), axis=0, keepdims=True)
    rest = jnp.where(row == i1, -jnp.inf, masked)
    m2 = jnp.max(rest, axis=0, keepdims=True)
    i2 = jnp.min(jnp.where(rest == m2, row, e), axis=0, keepdims=True)
    pick1 = row == i1
    pick2 = row == i2
    w1 = jnp.sum(jnp.where(pick1, scores, 0.0), axis=0, keepdims=True)
    w2 = jnp.sum(jnp.where(pick2, scores, 0.0), axis=0, keepdims=True)
    tot = w1 + w2
    return i1, i2, pick1, pick2, w1 / tot, w2 / tot


def _out_kernel(ma_ref, mb_ref, mm_ref, w_ref, x_ref, g_ref, b_ref, wr_ref, rb_ref, tri_ref,
                x1_ref, info_ref, wcol_ref, cnt_ref, carry_ref):
    @pl.when(pl.program_id(0) == 0)
    def _init():
        carry_ref[...] = jnp.zeros_like(carry_ref)

    y = _dot(ma_ref[...], w_ref[0:A_Q_W, :])
    y = y + _dot(mb_ref[...], w_ref[A_Q_W:A_Q_W + B_W, :])
    y = y + _dot(mm_ref[...], w_ref[A_Q_W + B_W:, :])
    x1 = _layer_norm(ALPHA * x_ref[...] + y, g_ref[...], b_ref[...])
    x1_ref[...] = x1
    x_hi = x1.astype(BF16)
    x_lo = (x1 - x_hi.astype(F32)).astype(BF16)
    hi2 = _dot(x_hi, wr_ref[...])
    lg = hi2[:, 0:LANES] + hi2[:, LANES:] + _dot(x_lo, wr_ref[:, 0:LANES])
    lg_t = lg.T[0:N_EXPERTS, :]
    i1, i2, _, _, w1, w2 = _route(lg_t, rb_ref[...])
    tm = lg_t.shape[1]
    eg = N_EXPERTS // N_GROUPS
    a = jnp.bitwise_and(i1, eg - 1)
    b = jnp.bitwise_and(i2, eg - 1)
    lo = jnp.minimum(a, b)
    hi = jnp.maximum(a, b)
    pair = jnp.where(lo == 0, 0, jnp.where(lo == 1, 3, 5)) + hi - lo - 1
    cls = lax.shift_right_logical(i1, 2) * PAIRS_PER_GROUP + pair
    first_is_lo = a < b
    row = lax.broadcasted_iota(jnp.int32, (N_CLASSES, tm), 0)
    mine = row == cls
    member = jnp.where(mine, 1.0, 0.0)
    before = carry_ref[:, 0:1] + _dot(member.astype(BF16), tri_ref[...])
    rank = jnp.sum(jnp.where(mine, before, 0.0), axis=0, keepdims=True)
    total = carry_ref[...] + jnp.sum(member, axis=1, keepdims=True)
    carry_ref[...] = total
    cnt_ref[...] = total
    info_ref[...] = jnp.concatenate([cls, rank.astype(jnp.int32), jnp.zeros((6, tm), jnp.int32)], axis=0)
    w_pad = jnp.concatenate([jnp.where(first_is_lo, w1, w2), jnp.where(first_is_lo, w2, w1),
                             jnp.zeros((LANES - 2, tm), F32)], axis=0)
    wcol_ref[...] = w_pad.T


def _out_proj(ma, mb, mm, w_out, x, g, b, wr_cat, rbias, tm=1024):
    n = x.shape[0]
    row = lambda i: (i, 0)
    fix = lambda i: (0, 0)
    tri = jnp.asarray(np.triu(np.ones((tm, tm), np.float32), k=1), BF16)
    return pl.pallas_call(
        _out_kernel,
        grid=(n // tm,),
        in_specs=[
            pl.BlockSpec((tm, A_Q_W), row), pl.BlockSpec((tm, B_W), row), pl.BlockSpec((tm, M_W), row),
            pl.BlockSpec((D_MODEL, D_MODEL), fix), pl.BlockSpec((tm, D_MODEL), row),
            pl.BlockSpec((1, D_MODEL), fix), pl.BlockSpec((1, D_MODEL), fix),
            pl.BlockSpec((D_MODEL, 2 * LANES), fix), pl.BlockSpec((N_EXPERTS, 1), fix),
            pl.BlockSpec((tm, tm), fix),
        ],
        out_specs=[pl.BlockSpec((tm, D_MODEL), row), pl.BlockSpec((8, tm), lambda i: (0, i)),
                   pl.BlockSpec((tm, LANES), row), pl.BlockSpec((N_CLASSES, LANES), fix)],
        out_shape=[jax.ShapeDtypeStruct((n, D_MODEL), F32), jax.ShapeDtypeStruct((8, n), jnp.int32),
                   jax.ShapeDtypeStruct((n, LANES), F32), jax.ShapeDtypeStruct((N_CLASSES, LANES), F32)],
        scratch_shapes=[pltpu.VMEM((N_CLASSES, LANES), F32)],
        compiler_params=_params(1),
        name="out_proj_ln_router",
    )(ma, mb, mm, w_out, x, g, b, wr_cat, rbias, tri)


MOE_ROW_TILE = 256
MOE_TOKEN_TILE = 1024


SUBLANES = 8


def _index_copy(pos_hbm, idx_ref, isem, step, slot):
    width = pos_hbm.shape[1]
    dst = idx_ref.at[pl.ds(pl.multiple_of(slot * width, width), width)]
    return pltpu.make_async_copy(pos_hbm.at[step], dst, isem.at[slot])


assert D_MODEL == SUBLANES * LANES
ZERO_CHUNK = 64
XS_ROWS = SUBLANES + 1


def _slab(ref, p, rows):
    start = p * rows
    if rows % SUBLANES == 0:
        start = pl.multiple_of(start, SUBLANES)
    return ref.at[pl.ds(start, rows)]


def _token_tile(ref, p):
    return _slab(ref, p, SUBLANES)


def _store_token_tiles(dst_ref, x):
    t = x.shape[0]
    for c in range(SUBLANES):
        dst_ref[pl.ds(c, t, stride=SUBLANES), :] = x[:, c * LANES:(c + 1) * LANES]


def _load_token_tiles(src_ref, t):
    return jnp.concatenate([src_ref[pl.ds(c, t, stride=SUBLANES), :] for c in range(SUBLANES)], axis=1)


def _dispatch_kernel(lo_ref, hi_ref, pos_hbm, x_ref, w_ref, xs_hbm, idx_ref, xt_ref, z_ref, isem, rsem, zsem,
                     *, td):
    i = pl.program_id(0)
    n = pl.num_programs(0)
    slot = lax.rem(i, 2)

    @pl.when(i == 0)
    def _first():
        _index_copy(pos_hbm, idx_ref, isem, 0, 0).start()
        z_ref[...] = jnp.zeros_like(z_ref)
        one = z_ref.at[pl.ds(0, XS_ROWS)]

        def fill(e, carry):
            a = lo_ref[e]
            b = hi_ref[e]
            n_single = jnp.minimum(b - a, jnp.bitwise_and(-a, ZERO_CHUNK - 1))
            a2 = a + n_single
            n_chunk = lax.shift_right_logical(b - a2, 6)

            def single(r, c):
                pltpu.make_async_copy(one, _slab(xs_hbm, a + r, XS_ROWS), zsem).start()
                return c

            def chunk(r, c):
                dst = xs_hbm.at[pl.ds((a2 + r * ZERO_CHUNK) * XS_ROWS, ZERO_CHUNK * XS_ROWS)]
                pltpu.make_async_copy(z_ref, dst, zsem).start()
                return c

            def single_wait(r, c):
                pltpu.make_async_copy(one, _slab(xs_hbm, 0, XS_ROWS), zsem).wait()
                return c

            def chunk_wait(r, c):
                pltpu.make_async_copy(z_ref, xs_hbm.at[pl.ds(0, ZERO_CHUNK * XS_ROWS)], zsem).wait()
                return c

            lax.fori_loop(0, n_single, single, 0)
            lax.fori_loop(0, n_chunk, chunk, 0)
            lax.fori_loop(0, n_single, single_wait, 0)
            lax.fori_loop(0, n_chunk, chunk_wait, 0)
            return carry

        lax.fori_loop(0, N_CLASSES, fill, 0)

    _index_copy(pos_hbm, idx_ref, isem, i, slot).wait()

    @pl.when(i + 1 < n)
    def _next():
        _index_copy(pos_hbm, idx_ref, isem, i + 1, 1 - slot).start()

    xt = xt_ref.at[slot]
    x = x_ref[...]
    for c in range(SUBLANES):
        xt[pl.ds(c, td, stride=XS_ROWS), :] = x[:, c * LANES:(c + 1) * LANES]
    xt[pl.ds(SUBLANES, td, stride=XS_ROWS), :] = w_ref[...]

    def drain(sl):
        pltpu.make_async_copy(xt_ref.at[sl], xs_hbm.at[pl.ds(0, td * XS_ROWS)], rsem.at[sl]).wait()

    @pl.when(i > 0)
    def _prev():
        drain(1 - slot)

    base = slot * td

    def body(j, carry):
        for u in range(SUBLANES):
            t = j * SUBLANES + u
            p = idx_ref[base + t]
            pltpu.make_async_copy(_slab(xt, t, XS_ROWS), _slab(xs_hbm, p, XS_ROWS),
                                  rsem.at[slot]).start(priority=u % 2)
        return carry

    lax.fori_loop(0, td // SUBLANES, body, 0)

    @pl.when(i == n - 1)
    def _last():
        drain(slot)


def _dispatch(x1, wcol, pos_t, pad_lo, pad_hi, n_rows, td):
    n = x1.shape[0]
    kern = functools.partial(_dispatch_kernel, td=td)
    grid_spec = pltpu.PrefetchScalarGridSpec(
        num_scalar_prefetch=2,
        grid=(n // td,),
        in_specs=[pl.BlockSpec(memory_space=pl.ANY),
                  pl.BlockSpec((td, D_MODEL), lambda i, lo, hi: (i, 0)),
                  pl.BlockSpec((td, LANES), lambda i, lo, hi: (i, 0))],
        out_specs=pl.BlockSpec(memory_space=pl.ANY),
        scratch_shapes=[pltpu.SMEM((2 * td,), jnp.int32),
                        pltpu.VMEM((2, td * XS_ROWS, LANES), F32),
                        pltpu.VMEM((ZERO_CHUNK * XS_ROWS, LANES), F32),
                        pltpu.SemaphoreType.DMA((2,)), pltpu.SemaphoreType.DMA((2,)),
                        pltpu.SemaphoreType.DMA(())],
    )
    return pl.pallas_call(
        kern,
        grid_spec=grid_spec,
        out_shape=jax.ShapeDtypeStruct((n_rows * XS_ROWS, LANES), F32),
        compiler_params=_params(1),
        name="moe_dispatch",
    )(pad_lo, pad_hi, pos_t, x1, wcol)


def _gmm_kernel(elo_ref, ehi_ref, rows_ref, xs_ref, wg0_ref, wu0_ref, wd0_ref, wg1_ref, wu1_ref, wd1_ref, ys_ref):
    del elo_ref, ehi_ref
    nrows = rows_ref[pl.program_id(0)]

    @pl.when(nrows > 0)
    def _compute():
        t = xs_ref.shape[0] // XS_ROWS
        x = jnp.concatenate([xs_ref[pl.ds(c, t, stride=XS_ROWS), :] for c in range(SUBLANES)], axis=1)
        x = x.astype(BF16)
        w = xs_ref[pl.ds(SUBLANES, t, stride=XS_ROWS), :]
        y = None
        for k, (wg_ref, wu_ref, wd_ref) in enumerate(((wg0_ref, wu0_ref, wd0_ref), (wg1_ref, wu1_ref, wd1_ref))):
            gate = _dot(x, wg_ref[...])
            up = _dot(x, wu_ref[...])
            h = (gate * (1.0 / (1.0 + jnp.exp(-gate))) * up).astype(BF16)
            yk = w[:, k:k + 1] * _dot(h, wd_ref[...])
            y = yk if y is None else y + yk
        _store_token_tiles(ys_ref, y)

    @pl.when(nrows == 0)
    def _empty():
        ys_ref[...] = jnp.zeros_like(ys_ref)


def _grouped_mlp(xs, tile_lo, tile_hi, tile_rows, wg, wu, wd, layer, tm):
    n_rows = xs.shape[0] // XS_ROWS

    def wspec(shape, which):
        return pl.BlockSpec((None, None) + shape, lambda g, lo, hi, tr: (layer, (lo, hi)[which][g], 0, 0))

    grid_spec = pltpu.PrefetchScalarGridSpec(
        num_scalar_prefetch=3,
        grid=(n_rows // tm,),
        in_specs=[pl.BlockSpec((tm * XS_ROWS, LANES), lambda g, lo, hi, tr: (g, 0)),
                  wspec((D_MODEL, D_EXPERT), 0), wspec((D_MODEL, D_EXPERT), 0), wspec((D_EXPERT, D_MODEL), 0),
                  wspec((D_MODEL, D_EXPERT), 1), wspec((D_MODEL, D_EXPERT), 1), wspec((D_EXPERT, D_MODEL), 1)],
        out_specs=pl.BlockSpec((tm * SUBLANES, LANES), lambda g, lo, hi, tr: (g, 0)),
    )
    return pl.pallas_call(
        _gmm_kernel,
        grid_spec=grid_spec,
        out_shape=jax.ShapeDtypeStruct((n_rows * SUBLANES, LANES), F32),
        compiler_params=_params(1),
        name="moe_grouped_mlp",
    )(tile_lo, tile_hi, tile_rows, xs, wg, wu, wd, wg, wu, wd)


def _combine_kernel(pos_hbm, ys_hbm, x1_ref, g_ref, b_ref, o_ref, idx_ref, buf_ref, isem, rsem, *, td):
    i = pl.program_id(0)
    n = pl.num_programs(0)
    slot = lax.rem(i, 2)

    def issue_rows(sl):
        base = sl * td

        def body(j, carry):
            for u in range(SUBLANES):
                t = j * SUBLANES + u
                p = idx_ref[base + t]
                pltpu.make_async_copy(_token_tile(ys_hbm, p), _token_tile(buf_ref.at[sl], t),
                                      rsem.at[sl]).start(priority=u % 2)
            return carry

        lax.fori_loop(0, td // SUBLANES, body, 0)

    @pl.when(i == 0)
    def _first():
        _index_copy(pos_hbm, idx_ref, isem, 0, 0).start()
        _index_copy(pos_hbm, idx_ref, isem, 0, 0).wait()
        issue_rows(0)

        @pl.when(n > 1)
        def _():
            _index_copy(pos_hbm, idx_ref, isem, 1, 1).start()

    @pl.when(i + 1 < n)
    def _next():
        _index_copy(pos_hbm, idx_ref, isem, i + 1, 1 - slot).wait()
        issue_rows(1 - slot)

        @pl.when(i + 2 < n)
        def _():
            _index_copy(pos_hbm, idx_ref, isem, i + 2, slot).start()

    pltpu.make_async_copy(ys_hbm.at[pl.ds(0, td * SUBLANES)], buf_ref.at[slot], rsem.at[slot]).wait()
    y = _load_token_tiles(buf_ref.at[slot], td)
    o_ref[...] = _layer_norm(ALPHA * x1_ref[...] + y, g_ref[...], b_ref[...])


def _combine(ys, pos_t, x1, g, b, td):
    n = x1.shape[0]
    row = lambda i: (i, 0)
    fix = lambda i: (0, 0)
    kern = functools.partial(_combine_kernel, td=td)
    return pl.pallas_call(
        kern,
        grid=(n // td,),
        in_specs=[pl.BlockSpec(memory_space=pl.ANY), pl.BlockSpec(memory_space=pl.ANY),
                  pl.BlockSpec((td, D_MODEL), row),
                  pl.BlockSpec((1, D_MODEL), fix), pl.BlockSpec((1, D_MODEL), fix)],
        out_specs=pl.BlockSpec((td, D_MODEL), row),
        out_shape=jax.ShapeDtypeStruct((n, D_MODEL), F32),
        scratch_shapes=[pltpu.SMEM((2 * td,), jnp.int32),
                        pltpu.VMEM((2, td * SUBLANES, LANES), F32),
                        pltpu.SemaphoreType.DMA((2,)), pltpu.SemaphoreType.DMA((2,))],
        compiler_params=_params(1),
        name="moe_combine_ln2",
    )(pos_t, ys, x1, g, b)


def _moe_routed(x1, info, wcol, cnt, p):
    n = x1.shape[0]
    tm, td = MOE_ROW_TILE, MOE_TOKEN_TILE
    n_tiles = n // tm + N_CLASSES
    counts = cnt[:, 0].astype(jnp.int32)
    padded = (counts + (tm - 1)) // tm * tm
    ends = jnp.cumsum(padded)
    offs = ends - padded
    starts = jnp.arange(n_tiles, dtype=jnp.int32) * tm
    tc = jnp.sum((ends[None, :] <= starts[:, None]).astype(jnp.int32), axis=1)
    tile_class = jnp.minimum(tc, N_CLASSES - 1)
    classes = jnp.arange(N_CLASSES, dtype=jnp.int32)
    pick = tile_class[:, None] == classes[None, :]
    seg_end = jnp.sum(jnp.where(pick, (offs + counts)[None, :], 0), axis=1)
    tile_rows = jnp.where(tc < N_CLASSES, jnp.clip(seg_end - starts, 0, tm), 0).astype(jnp.int32)
    eg = N_EXPERTS // N_GROUPS
    pairs = [(a, b) for a in range(eg) for b in range(a + 1, eg)]
    class_lo = np.asarray([g * eg + a for g in range(N_GROUPS) for a, _ in pairs], np.int32)
    class_hi = np.asarray([g * eg + b for g in range(N_GROUPS) for _, b in pairs], np.int32)
    tile_lo = jnp.sum(jnp.where(pick, class_lo[None, :], 0), axis=1).astype(jnp.int32)
    tile_hi = jnp.sum(jnp.where(pick, class_hi[None, :], 0), axis=1).astype(jnp.int32)
    cls, rank = info[0], info[1]
    pos = rank + jnp.sum(jnp.where(cls[None, :] == classes[:, None], offs[:, None], 0), axis=0)
    pos_t = pos.reshape(n // td, td)
    pad_lo = offs + counts
    pad_hi = ends.at[N_CLASSES - 1].set(n_tiles * tm)
    xs = _dispatch(x1, wcol, pos_t, pad_lo, pad_hi, n_tiles * tm, td)
    ys = _grouped_mlp(xs, tile_lo, tile_hi, tile_rows, p["w_gate"], p["w_up"], p["w_down"], p["layer"], tm)
    return _combine(ys, pos_t, x1, p["ln2_g"], p["ln2_b"], td)


def _layer(x, mem, p):
    b, s, d = x.shape
    n = b * s
    xf = x.reshape(n, d)
    proj = _matmul(xf, p["w_in"], 1024, "in_proj").reshape(b, s, IN_W)
    mkv = _matmul(mem.reshape(-1, d), p["w_mem_kv"], 512, "mem_kv").reshape(b, -1, 2 * M_W)
    oa = _window_attention(proj, p["sink"], p["g_a"])
    ob = _neighborhood_attention(proj, p["na_bias"], p["g_b"])
    om = _memory_attention(proj, mkv, p["g_m"])
    x1, info, wcol, cnt = _out_proj(oa.reshape(n, A_Q_W), ob.reshape(n, B_W), om.reshape(n, M_W), p["w_out"], xf,
                                    p["ln1_g"], p["ln1_b"], p["wr_cat"], p["rbias"])
    y = _moe_routed(x1, info, wcol, cnt, p)
    return y.reshape(b, s, d)


def kernel(x_prompt, x_sample, mem_prompt, mem_sample, w_in, w_mem_kv, sink_logits, rpb, grp_norm_g, w_out,
           ln1_g, ln1_b, w_router, router_bias, w_gate, w_up, w_down, ln2_g, ln2_b):
    seq = x_prompt.shape[1]
    assert x_sample.shape[1] == seq
    wr = jnp.pad(w_router.astype(F32), ((0, 0), (0, LANES - N_EXPERTS)))
    wr_hi = wr.astype(BF16)
    wr_cat = jnp.concatenate([wr_hi, (wr - wr_hi.astype(F32)).astype(BF16)], axis=1)
    q_scale = np.ones((IN_W,), np.float32)
    for c0, width in ((0, A_Q_W), (A_Q_W + 2 * LANES, B_W), (IN_W - M_W, M_W)):
        q_scale[c0:c0 + width] = SCALE * LOG2E
    rbias = router_bias.astype(F32).reshape(N_EXPERTS, 1)
    w_gate_b, w_up_b, w_down_b = w_gate.astype(BF16), w_up.astype(BF16), w_down.astype(BF16)
    layers = []
    for l in range(DEPTH):
        g = grp_norm_g[l].astype(F32)
        layers.append(dict(
            w_in=(w_in[l] * q_scale).astype(BF16), w_mem_kv=w_mem_kv[l].astype(BF16),
            sink=sink_logits[l].astype(F32), na_bias=_na_bias(rpb[l], seq),
            g_a=g[:A_Q_W].reshape(1, -1), g_b=g[A_Q_W:A_Q_W + B_W].reshape(1, -1),
            g_m=g[A_Q_W + B_W:].reshape(1, -1),
            w_out=w_out[l].astype(BF16),
            ln1_g=ln1_g[l].astype(F32).reshape(1, -1), ln1_b=ln1_b[l].astype(F32).reshape(1, -1),
            wr_cat=wr_cat, rbias=rbias,
            w_gate=w_gate_b, w_up=w_up_b, w_down=w_down_b, layer=l,
            ln2_g=ln2_g[l].astype(F32).reshape(1, -1), ln2_b=ln2_b[l].astype(F32).reshape(1, -1)))

    def trunk(x, mem):
        for p in layers:
            x = _layer(x, mem, p)
        return x

    return (trunk(x_prompt, mem_prompt), trunk(x_sample, mem_sample))
```

```python
import functools

import jax
import jax.numpy as jnp
import numpy as np
from jax import lax
from jax.experimental import pallas as pl
from jax.experimental.pallas import tpu as pltpu

F32 = jnp.float32
BF16 = jnp.bfloat16

D_MODEL = 1024
DEPTH = 2
HEAD_DIM = 64
A_HEADS = 8
WINDOW = 128
B_HEADS = 4
GRID_W = 64
NA_ROWS = 8
NA_COLS = 16
M_HEADS = 4
N_EXPERTS = 16
N_GROUPS = 4
D_EXPERT = 512
PAIRS_PER_GROUP = 6
N_CLASSES = N_GROUPS * PAIRS_PER_GROUP
LN_EPS = 1e-5
A_Q_W = 512
B_W = 256
M_W = 256
IN_W = 1792
ALPHA = (2 * DEPTH) ** 0.25
SCALE = HEAD_DIM ** -0.5
LOG2E = 1.4426950408889634
NEG = -1e30

LANES = 128
NA_TILE_ROWS = 2
NA_STEP_TILES = 4
NA_KEY_ROWS = 10
VMEM_LIMIT = 56 * 1024 * 1024


def _params(n_axes, vmem=VMEM_LIMIT):
    return pltpu.CompilerParams(dimension_semantics=("arbitrary",) * n_axes, vmem_limit_bytes=vmem)


def _dot_nt(a, b):
    return lax.dot_general(a, b, (((1,), (1,)), ((), ())), preferred_element_type=F32)


def _dot(a, b):
    return jnp.dot(a, b, preferred_element_type=F32)


def _lo_mask():
    return lax.broadcasted_iota(jnp.int32, (1, LANES), 1) < HEAD_DIM


def _mm_kernel(x_ref, w_ref, o_ref):
    o_ref[...] = _dot(x_ref[...].astype(BF16), w_ref[...]).astype(o_ref.dtype)


def _matmul(x, w, tm, name):
    n, k = x.shape
    m = w.shape[1]
    tm = min(tm, n)
    assert n % tm == 0
    return pl.pallas_call(
        _mm_kernel,
        grid=(n // tm,),
        in_specs=[pl.BlockSpec((tm, k), lambda i: (i, 0)), pl.BlockSpec((k, m), lambda i: (0, 0))],
        out_specs=pl.BlockSpec((tm, m), lambda i: (i, 0)),
        out_shape=jax.ShapeDtypeStruct((n, m), BF16),
        compiler_params=_params(1),
        name=name,
    )(x, w)


def _group_rms(of, g):
    ms = jnp.mean(of * of, axis=-1, keepdims=True)
    return of * lax.rsqrt(ms + LN_EPS) * g


def _win_kernel(sink_ref, q_ref, k_ref, v_ref, g_ref, o_ref, ks_ref, vs_ref, bias_ref, s_ref, p_ref, d_ref, *,
                seq, tq):
    i = pl.program_id(1)
    lo = _lo_mask()
    chunk = 512

    @pl.when(i == 0)
    def _build():
        zeros = jnp.zeros((WINDOW, LANES), BF16)
        for t in range(4):
            for dst in (ks_ref, vs_ref):
                dst[t, 0:WINDOW, :] = zeros
                dst[t, seq + WINDOW:seq + 2 * WINDOW, :] = zeros

        def body(c, carry):
            r0 = pl.multiple_of(c * chunk, chunk)
            for src, dst in ((k_ref, ks_ref), (v_ref, vs_ref)):
                x = src[pl.ds(r0, chunk), :].astype(F32)
                xr = pltpu.roll(x, HEAD_DIM, 1)
                rows = pl.ds(r0 + WINDOW, chunk)
                dst[0, rows, :] = jnp.where(lo, x, 0.0).astype(BF16)
                dst[1, rows, :] = jnp.where(lo, 0.0, xr).astype(BF16)
                dst[2, rows, :] = jnp.where(lo, xr, 0.0).astype(BF16)
                dst[3, rows, :] = jnp.where(lo, 0.0, x).astype(BF16)
            return carry

        lax.fori_loop(0, seq // chunk, body, 0)

    kw = 3 * WINDOW

    @pl.when(jnp.logical_and(pl.program_id(0) == 0, i == 0))
    def _build_bias():
        a_i = lax.broadcasted_iota(jnp.int32, (WINDOW, kw), 0)
        j_i = lax.broadcasted_iota(jnp.int32, (WINDOW, kw), 1)
        dist = jnp.abs(a_i + WINDOW - j_i)
        absd = dist.astype(F32)
        for variant in range(3):
            valid = dist <= WINDOW
            if variant == 0:
                valid = valid & (j_i >= WINDOW)
            if variant == 2:
                valid = valid & (j_i < 2 * WINDOW)
            for head in range(A_HEADS):
                slope = 2.0 ** (-8.0 * (head + 1) / A_HEADS)
                bias_ref[variant * A_HEADS + head] = jnp.where(valid, (-slope * LOG2E) * absd, NEG)

    g = g_ref[...]

    nblk = tq // WINDOW
    wins, variants = [], []
    for j in range(nblk):
        qs = pl.multiple_of(i * tq + j * WINDOW, WINDOW)
        wins.append(pl.ds(qs, kw))
        variants.append(jnp.where(qs == 0, 0, jnp.where(qs == seq - WINDOW, 2, 1)))
    for j in range(nblk):
        for head in range(A_HEADS):
            c, t = divmod(head, 2)
            q2 = q_ref[j * WINDOW:(j + 1) * WINDOW, c * LANES:(c + 1) * LANES]
            s_ref[j * A_HEADS + head] = (_dot_nt(q2, ks_ref[2 * (c // 2) + t, wins[j], :])
                                         + bias_ref[variants[j] * A_HEADS + head])
    for j in range(nblk):
        for head in range(A_HEADS):
            s = s_ref[j * A_HEADS + head]
            sk = sink_ref[head] * LOG2E
            m = jnp.maximum(jnp.max(s, axis=-1, keepdims=True), sk)
            p = jnp.exp2(s - m)
            den = jnp.sum(p, axis=-1, keepdims=True) + jnp.exp2(sk - m)
            p_ref[j * A_HEADS + head] = p.astype(BF16)
            d_ref[j * A_HEADS + head] = jnp.broadcast_to(den, (WINDOW, LANES))
    for j in range(nblk):
        outs = []
        for c in range(4):
            h = c // 2
            u = j * A_HEADS + 2 * c
            num = (_dot(p_ref[u], vs_ref[2 * h, wins[j], :])
                   + _dot(p_ref[u + 1], vs_ref[2 * h + 1, wins[j], :]))
            outs.append(num / jnp.where(lo, d_ref[u], d_ref[u + 1]))
        of = jnp.concatenate(outs, axis=1)
        o_ref[j * WINDOW:(j + 1) * WINDOW, :] = _group_rms(of, g).astype(o_ref.dtype)


def _window_attention(proj, sink, g, tq=1024):
    b, s, _ = proj.shape
    assert s % tq == 0 and s >= 2 * WINDOW
    kern = functools.partial(_win_kernel, seq=s, tq=tq)
    return pl.pallas_call(
        kern,
        grid=(b, s // tq),
        in_specs=[
            pl.BlockSpec(memory_space=pltpu.SMEM),
            pl.BlockSpec((None, tq, A_Q_W), lambda bi, i: (bi, i, 0)),
            pl.BlockSpec((None, s, LANES), lambda bi, i: (bi, 0, 4)),
            pl.BlockSpec((None, s, LANES), lambda bi, i: (bi, 0, 5)),
            pl.BlockSpec((1, A_Q_W), lambda bi, i: (0, 0)),
        ],
        out_specs=pl.BlockSpec((None, tq, A_Q_W), lambda bi, i: (bi, i, 0)),
        out_shape=jax.ShapeDtypeStruct((b, s, A_Q_W), BF16),
        scratch_shapes=[pltpu.VMEM((4, s + 2 * WINDOW, LANES), BF16),
                        pltpu.VMEM((4, s + 2 * WINDOW, LANES), BF16),
                        pltpu.VMEM((3 * A_HEADS, WINDOW, 3 * WINDOW), F32),
                        pltpu.VMEM((tq // WINDOW * A_HEADS, WINDOW, 3 * WINDOW), F32),
                        pltpu.VMEM((tq // WINDOW * A_HEADS, WINDOW, 3 * WINDOW), BF16),
                        pltpu.VMEM((tq // WINDOW * A_HEADS, WINDOW, LANES), F32)],
        compiler_params=_params(2),
        name="window_attention",
    )(sink, proj, proj, proj, g)


def _na_tables(seq):
    rows = seq // GRID_W
    kh = min(NA_ROWS, rows)
    assert rows % NA_TILE_ROWS == 0 and rows >= NA_KEY_ROWS and NA_KEY_ROWS % 2 == 0
    nt = rows // NA_TILE_ROWS
    u = np.arange(NA_TILE_ROWS * GRID_W) // GRID_W
    c = np.arange(NA_TILE_ROWS * GRID_W) % GRID_W
    ki = np.arange(NA_KEY_ROWS * GRID_W) // GRID_W
    kc = np.arange(NA_KEY_ROWS * GRID_W) % GRID_W
    cs = np.clip(c - NA_COLS // 2, 0, GRID_W - NA_COLS)
    col_ok = (kc[None, :] >= cs[:, None]) & (kc[None, :] < cs[:, None] + NA_COLS)
    dc = np.clip(kc[None, :] - c[:, None] + NA_COLS - 1, 0, 2 * NA_COLS - 2)
    types, type_of, kstart = [], [], []
    for t in range(nt):
        r0 = t * NA_TILE_ROWS
        k0 = int(np.clip(r0 - kh // 2, 0, rows - NA_KEY_ROWS))
        k0 -= k0 % 2
        r = r0 + u
        rs = np.clip(r - kh // 2, 0, rows - kh)
        kr = k0 + ki
        row_ok = (kr[None, :] >= rs[:, None]) & (kr[None, :] < rs[:, None] + kh)
        assert (rs >= k0).all() and (rs + kh <= k0 + NA_KEY_ROWS).all()
        dr = np.clip(kr[None, :] - r[:, None] + NA_ROWS - 1, 0, 2 * NA_ROWS - 2)
        key = (dr.tobytes(), row_ok.tobytes())
        for idx, (k_, _, _) in enumerate(types):
            if k_ == key:
                type_of.append(idx)
                break
        else:
            type_of.append(len(types))
            types.append((key, dr, row_ok & col_ok))
        kstart.append(k0)
    dr_rows = np.stack([t_[1][::GRID_W, ::GRID_W] for t_ in types])
    ok_all = np.stack([t_[2] for t_ in types])
    dc_cols = dc[:GRID_W, :GRID_W]
    return (np.asarray(type_of, np.int32), np.asarray(kstart, np.int32), dr_rows, dc_cols, ok_all)


def _na_bias(rpb, seq):
    _, _, dr_rows, dc_cols, ok = _na_tables(seq)
    nt = dr_rows.shape[0]
    hi = lax.Precision.HIGHEST
    col_sel = jnp.asarray(np.eye(2 * NA_COLS - 1, dtype=np.float32)[dc_cols])
    row_sel = jnp.asarray(np.eye(2 * NA_ROWS - 1, dtype=np.float32)[dr_rows])
    by_col = jnp.einsum("hdj,ckj->hdck", rpb.astype(F32), col_sel, precision=hi)
    bias = jnp.einsum("tuid,hdck->thucik", row_sel, by_col, precision=hi)
    bias = bias.reshape(nt, B_HEADS, NA_TILE_ROWS * GRID_W, NA_KEY_ROWS * GRID_W)
    return jnp.where(ok[:, None], bias * LOG2E, NEG)


def _na_kernel(type_ref, kstart_ref, q_ref, k_ref, v_ref, *rest, seq):
    del type_ref
    bias_refs = rest[:NA_STEP_TILES]
    g_ref, o_ref, ks_ref, vs_ref, s_ref, p_ref = rest[NA_STEP_TILES:]
    i = pl.program_id(1)
    lo = _lo_mask()
    chunk = 512

    @pl.when(i == 0)
    def _build():
        def body(c, carry):
            rows = pl.ds(pl.multiple_of(c * chunk, chunk), chunk)
            for src, dst in ((k_ref, ks_ref), (v_ref, vs_ref)):
                for p in range(2):
                    x = src[rows, p * LANES:(p + 1) * LANES]
                    dst[2 * p, rows, :] = jnp.where(lo, x, jnp.zeros_like(x))
                    dst[2 * p + 1, rows, :] = jnp.where(lo, jnp.zeros_like(x), x)
            return carry

        lax.fori_loop(0, seq // chunk, body, 0)

    nk = NA_KEY_ROWS * GRID_W
    tq = NA_TILE_ROWS * GRID_W
    units = [(t, head) for t in range(NA_STEP_TILES) for head in range(B_HEADS)]
    wins = [pl.ds(pl.multiple_of(kstart_ref[i * NA_STEP_TILES + t] * GRID_W, LANES), nk)
            for t in range(NA_STEP_TILES)]
    for u, (t, head) in enumerate(units):
        q2 = q_ref[t * tq:(t + 1) * tq, (head // 2) * LANES:(head // 2 + 1) * LANES]
        s_ref[u] = _dot_nt(q2, ks_ref[head, wins[t], :]) + bias_refs[t][head]
    invs = []
    for u in range(len(units)):
        s = s_ref[u]
        m = jnp.max(s, axis=-1, keepdims=True)
        e = jnp.exp2(s - m)
        invs.append(1.0 / jnp.sum(e, axis=-1, keepdims=True))
        p_ref[u] = e.astype(BF16)
    g = g_ref[...]
    for t in range(NA_STEP_TILES):
        outs = []
        for p in range(2):
            u = t * B_HEADS + 2 * p
            num = _dot(p_ref[u], vs_ref[2 * p, wins[t], :]) + _dot(p_ref[u + 1], vs_ref[2 * p + 1, wins[t], :])
            outs.append(num * jnp.where(lo, invs[u], invs[u + 1]))
        of = jnp.concatenate(outs, axis=1)
        o_ref[t * tq:(t + 1) * tq, :] = _group_rms(of, g).astype(o_ref.dtype)


def _neighborhood_attention(proj, bias, g):
    b, s, _ = proj.shape
    type_of, kstart, _, _, _ = _na_tables(s)
    tq = NA_TILE_ROWS * GRID_W
    nk = NA_KEY_ROWS * GRID_W
    step_q = NA_STEP_TILES * tq
    assert s % step_q == 0
    kern = functools.partial(_na_kernel, seq=s)

    def bias_spec(t):
        return pl.BlockSpec((None, B_HEADS, tq, nk), lambda bi, i, ty, ks: (ty[i * NA_STEP_TILES + t], 0, 0, 0))

    grid_spec = pltpu.PrefetchScalarGridSpec(
        num_scalar_prefetch=2,
        grid=(b, s // step_q),
        in_specs=[
            pl.BlockSpec((None, step_q, B_W), lambda bi, i, ty, ks: (bi, i, 3)),
            pl.BlockSpec((None, s, B_W), lambda bi, i, ty, ks: (bi, 0, 4)),
            pl.BlockSpec((None, s, B_W), lambda bi, i, ty, ks: (bi, 0, 5)),
            *[bias_spec(t) for t in range(NA_STEP_TILES)],
            pl.BlockSpec((1, B_W), lambda bi, i, ty, ks: (0, 0)),
        ],
        out_specs=pl.BlockSpec((None, step_q, B_W), lambda bi, i, ty, ks: (bi, i, 0)),
        scratch_shapes=[pltpu.VMEM((4, s, LANES), BF16), pltpu.VMEM((4, s, LANES), BF16),
                        pltpu.VMEM((NA_STEP_TILES * B_HEADS, tq, nk), F32),
                        pltpu.VMEM((NA_STEP_TILES * B_HEADS, tq, nk), BF16)],
    )
    return pl.pallas_call(
        kern,
        grid_spec=grid_spec,
        out_shape=jax.ShapeDtypeStruct((b, s, B_W), BF16),
        compiler_params=_params(2),
        name="neighborhood_attention",
    )(jnp.asarray(type_of), jnp.asarray(kstart), proj, proj, proj, *([bias] * NA_STEP_TILES), g)


def _mem_kernel(q_ref, mkv_ref, g_ref, o_ref, s_ref, p_ref):
    lo = _lo_mask()
    hi = jnp.logical_not(lo)

    def masked(col0, head):
        x = mkv_ref[:, col0 + (head // 2) * LANES:col0 + (head // 2 + 1) * LANES]
        return jnp.where(lo if head % 2 == 0 else hi, x, jnp.zeros_like(x))

    for head in range(M_HEADS):
        q2 = q_ref[:, (head // 2) * LANES:(head // 2 + 1) * LANES]
        s_ref[head] = _dot_nt(q2, masked(0, head))
    invs = []
    for head in range(M_HEADS):
        s = s_ref[head]
        m = jnp.max(s, axis=-1, keepdims=True)
        e = jnp.exp2(s - m)
        invs.append(1.0 / jnp.sum(e, axis=-1, keepdims=True))
        p_ref[head] = e.astype(BF16)
    outs = []
    for p in range(2):
        num = _dot(p_ref[2 * p], masked(M_W, 2 * p)) + _dot(p_ref[2 * p + 1], masked(M_W, 2 * p + 1))
        outs.append(num * jnp.where(lo, invs[2 * p], invs[2 * p + 1]))
    of = jnp.concatenate(outs, axis=1)
    o_ref[...] = _group_rms(of, g_ref[...]).astype(o_ref.dtype)


def _memory_attention(proj, mkv, g, tq=1024):
    b, s, _ = proj.shape
    mlen = mkv.shape[1]
    return pl.pallas_call(
        _mem_kernel,
        grid=(b, s // tq),
        in_specs=[
            pl.BlockSpec((None, tq, M_W), lambda bi, i: (bi, i, 6)),
            pl.BlockSpec((None, mlen, 2 * M_W), lambda bi, i: (bi, 0, 0)),
            pl.BlockSpec((1, M_W), lambda bi, i: (0, 0)),
        ],
        out_specs=pl.BlockSpec((None, tq, M_W), lambda bi, i: (bi, i, 0)),
        out_shape=jax.ShapeDtypeStruct((b, s, M_W), BF16),
        scratch_shapes=[pltpu.VMEM((M_HEADS, tq, mlen), F32), pltpu.VMEM((M_HEADS, tq, mlen), BF16)],
        compiler_params=_params(2),
        name="memory_attention",
    )(proj, mkv, g)


def _layer_norm(z, g, b):
    mu = jnp.mean(z, axis=-1, keepdims=True)
    zc = z - mu
    var = jnp.mean(zc * zc, axis=-1, keepdims=True)
    return zc * lax.rsqrt(var + LN_EPS) * g + b


def _route(lg_t, rbias):
    e, t = lg_t.shape
    eg = e // N_GROUPS
    scores = 1.0 / (1.0 + jnp.exp(-lg_t))
    sel = scores + rbias
    row = lax.broadcasted_iota(jnp.int32, (e, t), 0)
    best = None
    gidx = None
    for gi in range(N_GROUPS):
        r = [sel[gi * eg + k:gi * eg + k + 1, :] for k in range(eg)]
        top2 = None
        for a in range(eg):
            for b in range(a + 1, eg):
                pair = r[a] + r[b]
                top2 = pair if top2 is None else jnp.maximum(top2, pair)
        if best is None:
            best, gidx = top2, jnp.zeros((1, t), jnp.int32)
        else:
            better = top2 > best
            gidx = jnp.where(better, gi, gidx)
            best = jnp.maximum(best, top2)
    in_group = (row // eg) == gidx
    masked = jnp.where(in_group, sel, -jnp.inf)
    m1 = jnp.max(masked, axis=0, keepdims=True)
    i1 = jnp.min(jnp.where(masked == m1, row, e), axis=0, keepdims=True)
    rest = jnp.where(row == i1, -jnp.inf, masked)
    m2 = jnp.max(rest, axis=0, keepdims=True)
    i2 = jnp.min(jnp.where(rest == m2, row, e), axis=0, keepdims=True)
    pick1 = row == i1
    pick2 = row == i2
    w1 = jnp.sum(jnp.where(pick1, scores, 0.0), axis=0, keepdims=True)
    w2 = jnp.sum(jnp.where(pick2, scores, 0.0), axis=0, keepdims=True)
    tot = w1 + w2
    return i1, i2, pick1, pick2, w1 / tot, w2 / tot


def _out_kernel(ma_ref, mb_ref, mm_ref, w_ref, x_ref, g_ref, b_ref, wr_ref, rb_ref, tri_ref,
                x1_ref, info_ref, wcol_ref, cnt_ref, carry_ref):
    @pl.when(pl.program_id(0) == 0)
    def _init():
        carry_ref[...] = jnp.zeros_like(carry_ref)

    y = _dot(ma_ref[...], w_ref[0:A_Q_W, :])
    y = y + _dot(mb_ref[...], w_ref[A_Q_W:A_Q_W + B_W, :])
    y = y + _dot(mm_ref[...], w_ref[A_Q_W + B_W:, :])
    x1 = _layer_norm(ALPHA * x_ref[...] + y, g_ref[...], b_ref[...])
    x1_ref[...] = x1
    x_hi = x1.astype(BF16)
    x_lo = (x1 - x_hi.astype(F32)).astype(BF16)
    hi2 = _dot(x_hi, wr_ref[...])
    lg = hi2[:, 0:LANES] + hi2[:, LANES:] + _dot(x_lo, wr_ref[:, 0:LANES])
    lg_t = lg.T[0:N_EXPERTS, :]
    i1, i2, _, _, w1, w2 = _route(lg_t, rb_ref[...])
    tm = lg_t.shape[1]
    eg = N_EXPERTS // N_GROUPS
    a = jnp.bitwise_and(i1, eg - 1)
    b = jnp.bitwise_and(i2, eg - 1)
    lo = jnp.minimum(a, b)
    hi = jnp.maximum(a, b)
    pair = jnp.where(lo == 0, 0, jnp.where(lo == 1, 3, 5)) + hi - lo - 1
    cls = lax.shift_right_logical(i1, 2) * PAIRS_PER_GROUP + pair
    first_is_lo = a < b
    row = lax.broadcasted_iota(jnp.int32, (N_CLASSES, tm), 0)
    mine = row == cls
    member = jnp.where(mine, 1.0, 0.0)
    before = carry_ref[:, 0:1] + _dot(member.astype(BF16), tri_ref[...])
    rank = jnp.sum(jnp.where(mine, before, 0.0), axis=0, keepdims=True)
    total = carry_ref[...] + jnp.sum(member, axis=1, keepdims=True)
    carry_ref[...] = total
    cnt_ref[...] = total
    info_ref[...] = jnp.concatenate([cls, rank.astype(jnp.int32), jnp.zeros((6, tm), jnp.int32)], axis=0)
    w_pad = jnp.concatenate([jnp.where(first_is_lo, w1, w2), jnp.where(first_is_lo, w2, w1),
                             jnp.zeros((LANES - 2, tm), F32)], axis=0)
    wcol_ref[...] = w_pad.T


def _out_proj(ma, mb, mm, w_out, x, g, b, wr_cat, rbias, tm=1024):
    n = x.shape[0]
    row = lambda i: (i, 0)
    fix = lambda i: (0, 0)
    tri = jnp.asarray(np.triu(np.ones((tm, tm), np.float32), k=1), BF16)
    return pl.pallas_call(
        _out_kernel,
        grid=(n // tm,),
        in_specs=[
            pl.BlockSpec((tm, A_Q_W), row), pl.BlockSpec((tm, B_W), row), pl.BlockSpec((tm, M_W), row),
            pl.BlockSpec((D_MODEL, D_MODEL), fix), pl.BlockSpec((tm, D_MODEL), row),
            pl.BlockSpec((1, D_MODEL), fix), pl.BlockSpec((1, D_MODEL), fix),
            pl.BlockSpec((D_MODEL, 2 * LANES), fix), pl.BlockSpec((N_EXPERTS, 1), fix),
            pl.BlockSpec((tm, tm), fix),
        ],
        out_specs=[pl.BlockSpec((tm, D_MODEL), row), pl.BlockSpec((8, tm), lambda i: (0, i)),
                   pl.BlockSpec((tm, LANES), row), pl.BlockSpec((N_CLASSES, LANES), fix)],
        out_shape=[jax.ShapeDtypeStruct((n, D_MODEL), F32), jax.ShapeDtypeStruct((8, n), jnp.int32),
                   jax.ShapeDtypeStruct((n, LANES), F32), jax.ShapeDtypeStruct((N_CLASSES, LANES), F32)],
        scratch_shapes=[pltpu.VMEM((N_CLASSES, LANES), F32)],
        compiler_params=_params(1),
        name="out_proj_ln_router",
    )(ma, mb, mm, w_out, x, g, b, wr_cat, rbias, tri)


MOE_ROW_TILE = 256
MOE_TOKEN_TILE = 1024


SUBLANES = 8


def _index_copy(pos_hbm, idx_ref, isem, step, slot):
    width = pos_hbm.shape[1]
    dst = idx_ref.at[pl.ds(pl.multiple_of(slot * width, width), width)]
    return pltpu.make_async_copy(pos_hbm.at[step], dst, isem.at[slot])


assert D_MODEL == SUBLANES * LANES
ZERO_CHUNK = 64
XS_ROWS = SUBLANES + 1


def _slab(ref, p, rows):
    start = p * rows
    if rows % SUBLANES == 0:
        start = pl.multiple_of(start, SUBLANES)
    return ref.at[pl.ds(start, rows)]


def _token_tile(ref, p):
    return _slab(ref, p, SUBLANES)


def _store_token_tiles(dst_ref, x):
    t = x.shape[0]
    for c in range(SUBLANES):
        dst_ref[pl.ds(c, t, stride=SUBLANES), :] = x[:, c * LANES:(c + 1) * LANES]


def _load_token_tiles(src_ref, t):
    return jnp.concatenate([src_ref[pl.ds(c, t, stride=SUBLANES), :] for c in range(SUBLANES)], axis=1)


def _dispatch_kernel(lo_ref, hi_ref, pos_hbm, x_ref, w_ref, xs_hbm, idx_ref, xt_ref, z_ref, isem, rsem, zsem,
                     *, td):
    i = pl.program_id(0)
    n = pl.num_programs(0)
    slot = lax.rem(i, 2)

    @pl.when(i == 0)
    def _first():
        _index_copy(pos_hbm, idx_ref, isem, 0, 0).start()
        z_ref[...] = jnp.zeros_like(z_ref)
        one = z_ref.at[pl.ds(0, XS_ROWS)]

        def fill(e, carry):
            a = lo_ref[e]
            b = hi_ref[e]
            n_single = jnp.minimum(b - a, jnp.bitwise_and(-a, ZERO_CHUNK - 1))
            a2 = a + n_single
            n_chunk = lax.shift_right_logical(b - a2, 6)

            def single(r, c):
                pltpu.make_async_copy(one, _slab(xs_hbm, a + r, XS_ROWS), zsem).start()
                return c

            def chunk(r, c):
                dst = xs_hbm.at[pl.ds((a2 + r * ZERO_CHUNK) * XS_ROWS, ZERO_CHUNK * XS_ROWS)]
                pltpu.make_async_copy(z_ref, dst, zsem).start()
                return c

            def single_wait(r, c):
                pltpu.make_async_copy(one, _slab(xs_hbm, 0, XS_ROWS), zsem).wait()
                return c

            def chunk_wait(r, c):
                pltpu.make_async_copy(z_ref, xs_hbm.at[pl.ds(0, ZERO_CHUNK * XS_ROWS)], zsem).wait()
                return c

            lax.fori_loop(0, n_single, single, 0)
            lax.fori_loop(0, n_chunk, chunk, 0)
            lax.fori_loop(0, n_single, single_wait, 0)
            lax.fori_loop(0, n_chunk, chunk_wait, 0)
            return carry

        lax.fori_loop(0, N_CLASSES, fill, 0)

    _index_copy(pos_hbm, idx_ref, isem, i, slot).wait()

    @pl.when(i + 1 < n)
    def _next():
        _index_copy(pos_hbm, idx_ref, isem, i + 1, 1 - slot).start()

    xt = xt_ref.at[slot]
    x = x_ref[...]
    for c in range(SUBLANES):
        xt[pl.ds(c, td, stride=XS_ROWS), :] = x[:, c * LANES:(c + 1) * LANES]
    xt[pl.ds(SUBLANES, td, stride=XS_ROWS), :] = w_ref[...]

    def drain(sl):
        pltpu.make_async_copy(xt_ref.at[sl], xs_hbm.at[pl.ds(0, td * XS_ROWS)], rsem.at[sl]).wait()

    @pl.when(i > 0)
    def _prev():
        drain(1 - slot)

    base = slot * td

    def body(j, carry):
        for u in range(SUBLANES):
            t = j * SUBLANES + u
            p = idx_ref[base + t]
            pltpu.make_async_copy(_slab(xt, t, XS_ROWS), _slab(xs_hbm, p, XS_ROWS),
                                  rsem.at[slot]).start(priority=u % 2)
        return carry

    lax.fori_loop(0, td // SUBLANES, body, 0)

    @pl.when(i == n - 1)
    def _last():
        drain(slot)


def _dispatch(x1, wcol, pos_t, pad_lo, pad_hi, n_rows, td):
    n = x1.shape[0]
    kern = functools.partial(_dispatch_kernel, td=td)
    grid_spec = pltpu.PrefetchScalarGridSpec(
        num_scalar_prefetch=2,
        grid=(n // td,),
        in_specs=[pl.BlockSpec(memory_space=pl.ANY),
                  pl.BlockSpec((td, D_MODEL), lambda i, lo, hi: (i, 0)),
                  pl.BlockSpec((td, LANES), lambda i, lo, hi: (i, 0))],
        out_specs=pl.BlockSpec(memory_space=pl.ANY),
        scratch_shapes=[pltpu.SMEM((2 * td,), jnp.int32),
                        pltpu.VMEM((2, td * XS_ROWS, LANES), F32),
                        pltpu.VMEM((ZERO_CHUNK * XS_ROWS, LANES), F32),
                        pltpu.SemaphoreType.DMA((2,)), pltpu.SemaphoreType.DMA((2,)),
                        pltpu.SemaphoreType.DMA(())],
    )
    return pl.pallas_call(
        kern,
        grid_spec=grid_spec,
        out_shape=jax.ShapeDtypeStruct((n_rows * XS_ROWS, LANES), F32),
        compiler_params=_params(1),
        name="moe_dispatch",
    )(pad_lo, pad_hi, pos_t, x1, wcol)


def _gmm_kernel(elo_ref, ehi_ref, rows_ref, xs_ref, wg0_ref, wu0_ref, wd0_ref, wg1_ref, wu1_ref, wd1_ref, ys_ref):
    del elo_ref, ehi_ref
    nrows = rows_ref[pl.program_id(0)]

    @pl.when(nrows > 0)
    def _compute():
        t = xs_ref.shape[0] // XS_ROWS
        x = jnp.concatenate([xs_ref[pl.ds(c, t, stride=XS_ROWS), :] for c in range(SUBLANES)], axis=1)
        x = x.astype(BF16)
        w = xs_ref[pl.ds(SUBLANES, t, stride=XS_ROWS), :]
        y = None
        for k, (wg_ref, wu_ref, wd_ref) in enumerate(((wg0_ref, wu0_ref, wd0_ref), (wg1_ref, wu1_ref, wd1_ref))):
            gate = _dot(x, wg_ref[...])
            up = _dot(x, wu_ref[...])
            h = (gate * (1.0 / (1.0 + jnp.exp(-gate))) * up).astype(BF16)
            yk = w[:, k:k + 1] * _dot(h, wd_ref[...])
            y = yk if y is None else y + yk
        for c in range(SUBLANES):
            ys_ref[pl.ds(c, t, stride=XS_ROWS), :] = y[:, c * LANES:(c + 1) * LANES]
        ys_ref[pl.ds(SUBLANES, t, stride=XS_ROWS), :] = jnp.zeros((t, LANES), F32)

    @pl.when(nrows == 0)
    def _empty():
        ys_ref[...] = jnp.zeros_like(ys_ref)


def _grouped_mlp(xs, tile_lo, tile_hi, tile_rows, wg, wu, wd, layer, tm):
    n_rows = xs.shape[0] // XS_ROWS

    def wspec(shape, which):
        return pl.BlockSpec((None, None) + shape, lambda g, lo, hi, tr: (layer, (lo, hi)[which][g], 0, 0))

    grid_spec = pltpu.PrefetchScalarGridSpec(
        num_scalar_prefetch=3,
        grid=(n_rows // tm,),
        in_specs=[pl.BlockSpec((tm * XS_ROWS, LANES), lambda g, lo, hi, tr: (g, 0)),
                  wspec((D_MODEL, D_EXPERT), 0), wspec((D_MODEL, D_EXPERT), 0), wspec((D_EXPERT, D_MODEL), 0),
                  wspec((D_MODEL, D_EXPERT), 1), wspec((D_MODEL, D_EXPERT), 1), wspec((D_EXPERT, D_MODEL), 1)],
        out_specs=pl.BlockSpec((tm * XS_ROWS, LANES), lambda g, lo, hi, tr: (g, 0)),
    )
    return pl.pallas_call(
        _gmm_kernel,
        grid_spec=grid_spec,
        out_shape=jax.ShapeDtypeStruct((n_rows * XS_ROWS, LANES), F32),
        compiler_params=_params(1),
        name="moe_grouped_mlp",
    )(tile_lo, tile_hi, tile_rows, xs, wg, wu, wd, wg, wu, wd)


def _combine_kernel(pos_hbm, ys_hbm, x1_ref, g_ref, b_ref, o_ref, idx_ref, buf_ref, isem, rsem, *, td):
    i = pl.program_id(0)
    n = pl.num_programs(0)
    slot = lax.rem(i, 2)

    def issue_rows(sl):
        base = sl * td

        def body(j, carry):
            for u in range(SUBLANES):
                t = j * SUBLANES + u
                p = idx_ref[base + t]
                pltpu.make_async_copy(_slab(ys_hbm, p, XS_ROWS), _slab(buf_ref.at[sl], t, XS_ROWS),
                                      rsem.at[sl]).start(priority=u % 2)
            return carry

        lax.fori_loop(0, td // SUBLANES, body, 0)

    @pl.when(i == 0)
    def _first():
        _index_copy(pos_hbm, idx_ref, isem, 0, 0).start()
        _index_copy(pos_hbm, idx_ref, isem, 0, 0).wait()
        issue_rows(0)

        @pl.when(n > 1)
        def _():
            _index_copy(pos_hbm, idx_ref, isem, 1, 1).start()

    @pl.when(i + 1 < n)
    def _next():
        _index_copy(pos_hbm, idx_ref, isem, i + 1, 1 - slot).wait()
        issue_rows(1 - slot)

        @pl.when(i + 2 < n)
        def _():
            _index_copy(pos_hbm, idx_ref, isem, i + 2, slot).start()

    pltpu.make_async_copy(ys_hbm.at[pl.ds(0, td * XS_ROWS)], buf_ref.at[slot], rsem.at[slot]).wait()
    src = buf_ref.at[slot]
    y = jnp.concatenate([src[pl.ds(c, td, stride=XS_ROWS), :] for c in range(SUBLANES)], axis=1)
    o_ref[...] = _layer_norm(ALPHA * x1_ref[...] + y, g_ref[...], b_ref[...])


def _combine(ys, pos_t, x1, g, b, td):
    n = x1.shape[0]
    row = lambda i: (i, 0)
    fix = lambda i: (0, 0)
    kern = functools.partial(_combine_kernel, td=td)
    return pl.pallas_call(
        kern,
        grid=(n // td,),
        in_specs=[pl.BlockSpec(memory_space=pl.ANY), pl.BlockSpec(memory_space=pl.ANY),
                  pl.BlockSpec((td, D_MODEL), row),
                  pl.BlockSpec((1, D_MODEL), fix), pl.BlockSpec((1, D_MODEL), fix)],
        out_specs=pl.BlockSpec((td, D_MODEL), row),
        out_shape=jax.ShapeDtypeStruct((n, D_MODEL), F32),
        scratch_shapes=[pltpu.SMEM((2 * td,), jnp.int32),
                        pltpu.VMEM((2, td * XS_ROWS, LANES), F32),
                        pltpu.SemaphoreType.DMA((2,)), pltpu.SemaphoreType.DMA((2,))],
        compiler_params=_params(1),
        name="moe_combine_ln2",
    )(pos_t, ys, x1, g, b)


def _moe_routed(x1, info, wcol, cnt, p):
    n = x1.shape[0]
    tm, td = MOE_ROW_TILE, MOE_TOKEN_TILE
    n_tiles = n // tm + N_CLASSES
    counts = cnt[:, 0].astype(jnp.int32)
    padded = (counts + (tm - 1)) // tm * tm
    ends = jnp.cumsum(padded)
    offs = ends - padded
    starts = jnp.arange(n_tiles, dtype=jnp.int32) * tm
    tc = jnp.sum((ends[None, :] <= starts[:, None]).astype(jnp.int32), axis=1)
    tile_class = jnp.minimum(tc, N_CLASSES - 1)
    classes = jnp.arange(N_CLASSES, dtype=jnp.int32)
    pick = tile_class[:, None] == classes[None, :]
    seg_end = jnp.sum(jnp.where(pick, (offs + counts)[None, :], 0), axis=1)
    tile_rows = jnp.where(tc < N_CLASSES, jnp.clip(seg_end - starts, 0, tm), 0).astype(jnp.int32)
    eg = N_EXPERTS // N_GROUPS
    pairs = [(a, b) for a in range(eg) for b in range(a + 1, eg)]
    class_lo = np.asarray([g * eg + a for g in range(N_GROUPS) for a, _ in pairs], np.int32)
    class_hi = np.asarray([g * eg + b for g in range(N_GROUPS) for _, b in pairs], np.int32)
    tile_lo = jnp.sum(jnp.where(pick, class_lo[None, :], 0), axis=1).astype(jnp.int32)
    tile_hi = jnp.sum(jnp.where(pick, class_hi[None, :], 0), axis=1).astype(jnp.int32)
    cls, rank = info[0], info[1]
    pos = rank + jnp.sum(jnp.where(cls[None, :] == classes[:, None], offs[:, None], 0), axis=0)
    pos_t = pos.reshape(n // td, td)
    pad_lo = offs + counts
    pad_hi = ends.at[N_CLASSES - 1].set(n_tiles * tm)
    xs = _dispatch(x1, wcol, pos_t, pad_lo, pad_hi, n_tiles * tm, td)
    ys = _grouped_mlp(xs, tile_lo, tile_hi, tile_rows, p["w_gate"], p["w_up"], p["w_down"], p["layer"], tm)
    return _combine(ys, pos_t, x1, p["ln2_g"], p["ln2_b"], td)


def _layer(x, mem, p):
    b, s, d = x.shape
    n = b * s
    xf = x.reshape(n, d)
    proj = _matmul(xf, p["w_in"], 1024, "in_proj").reshape(b, s, IN_W)
    mkv = _matmul(mem.reshape(-1, d), p["w_mem_kv"], 512, "mem_kv").reshape(b, -1, 2 * M_W)
    oa = _window_attention(proj, p["sink"], p["g_a"])
    ob = _neighborhood_attention(proj, p["na_bias"], p["g_b"])
    om = _memory_attention(proj, mkv, p["g_m"])
    x1, info, wcol, cnt = _out_proj(oa.reshape(n, A_Q_W), ob.reshape(n, B_W), om.reshape(n, M_W), p["w_out"], xf,
                                    p["ln1_g"], p["ln1_b"], p["wr_cat"], p["rbias"])
    y = _moe_routed(x1, info, wcol, cnt, p)
    return y.reshape(b, s, d)


def kernel(x_prompt, x_sample, mem_prompt, mem_sample, w_in, w_mem_kv, sink_logits, rpb, grp_norm_g, w_out,
           ln1_g, ln1_b, w_router, router_bias, w_gate, w_up, w_down, ln2_g, ln2_b):
    seq = x_prompt.shape[1]
    assert x_sample.shape[1] == seq
    wr = jnp.pad(w_router.astype(F32), ((0, 0), (0, LANES - N_EXPERTS)))
    wr_hi = wr.astype(BF16)
    wr_cat = jnp.concatenate([wr_hi, (wr - wr_hi.astype(F32)).astype(BF16)], axis=1)
    q_scale = np.ones((IN_W,), np.float32)
    for c0, width in ((0, A_Q_W), (A_Q_W + 2 * LANES, B_W), (IN_W - M_W, M_W)):
        q_scale[c0:c0 + width] = SCALE * LOG2E
    rbias = router_bias.astype(F32).reshape(N_EXPERTS, 1)
    w_gate_b, w_up_b, w_down_b = w_gate.astype(BF16), w_up.astype(BF16), w_down.astype(BF16)
    layers = []
    for l in range(DEPTH):
        g = grp_norm_g[l].astype(F32)
        layers.append(dict(
            w_in=(w_in[l] * q_scale).astype(BF16), w_mem_kv=w_mem_kv[l].astype(BF16),
            sink=sink_logits[l].astype(F32), na_bias=_na_bias(rpb[l], seq),
            g_a=g[:A_Q_W].reshape(1, -1), g_b=g[A_Q_W:A_Q_W + B_W].reshape(1, -1),
            g_m=g[A_Q_W + B_W:].reshape(1, -1),
            w_out=w_out[l].astype(BF16),
            ln1_g=ln1_g[l].astype(F32).reshape(1, -1), ln1_b=ln1_b[l].astype(F32).reshape(1, -1),
            wr_cat=wr_cat, rbias=rbias,
            w_gate=w_gate_b, w_up=w_up_b, w_down=w_down_b, layer=l,
            ln2_g=ln2_g[l].astype(F32).reshape(1, -1), ln2_b=ln2_b[l].astype(F32).reshape(1, -1)))

    def trunk(x, mem):
        for p in layers:
            x = _layer(x, mem, p)
        return x

    return (trunk(x_prompt, mem_prompt), trunk(x_sample, mem_sample))
```
